```python
import math
import jax, jax.numpy as jnp
from jax import lax
import numpy as np

D_MODEL = 1024
BATCH = 32
SEQ = 256
DEPTH = 4
DEC_BATCH = 2
DEC_SEQ = 4096
PAST_LEN = 256

GRID_W = 64
HEAD_DIM = 64
D_MIX = D_MODEL
D_M = D_MIX // 4
H_M = D_M // HEAD_DIM
DK = HEAD_DIM
DV = HEAD_DIM
D_S = 3 * D_MIX // 8
H_S = D_S // HEAD_DIM
P_S = HEAD_DIM
G_S = 2
R_S = H_S // G_S
N_S = 64
D_XBC = D_S + 2 * G_S * N_S
CONV_K = 3
D_NA = D_MIX - D_M - D_S
H_NA = D_NA // HEAD_DIM
WIN_H = 8
WIN_W = 16
Q_COLS = 16
K_COLS = 2 * WIN_W
Q_BLOCK = 128
CHUNK = 64
N_EXPERTS = 16
EC_FACTOR = 2
D_FF = 1024
EPS = 1e-6
NEG_INIT = -1e30
SECTIONS = (D_M, D_M, D_M, D_M, 2 * H_M, 2 * H_M, D_S, D_XBC, 2 * H_S, D_NA, D_NA, D_NA)

kernel_name = 'hybrid_mlstm_ssd_natten_ec_dit'


def rmsnorm(x, g):
    xf = x.astype(jnp.float32)
    y = xf * lax.rsqrt(jnp.mean(xf * xf, axis=-1, keepdims=True) + EPS)
    return (y * g.astype(jnp.float32)).astype(x.dtype)


def modulate(x, g, shift, scale):
    return rmsnorm(x, g) * (1.0 + scale) + shift


def project(h, w_in):
    points = np.cumsum(SECTIONS)[:-1].tolist()
    return jnp.split(h @ w_in, points, axis=-1)


def depthwise_conv(x, w, b):
    ch = x.shape[-1]
    y = lax.conv_general_dilated(x, w[:, None, :].astype(x.dtype), window_strides=(1,),
                                 padding=[(CONV_K // 2, CONV_K // 2)],
                                 dimension_numbers=('NWC', 'WIO', 'NWC'), feature_group_count=ch)
    return y + b.astype(x.dtype)


def mlstm_chunked(q, k, v, ig, logf, C0, n0, m0):
    B, H, T, _ = q.shape
    nc = T // CHUNK

    def to_chunks(a):
        return jnp.moveaxis(a.reshape(a.shape[:2] + (nc, CHUNK) + a.shape[3:]), 2, 0)

    xs = (to_chunks(q), to_chunks(k), to_chunks(v), to_chunks(ig), to_chunks(logf))
    causal = jnp.tril(jnp.ones((CHUNK, CHUNK), dtype=bool))

    def step(carry, xc):
        C, n, m = carry
        qc, kc, vc, ic, fc = xc
        b = jnp.cumsum(fc, axis=-1)
        logD = jnp.where(causal, b[..., :, None] - b[..., None, :] + ic[..., None, :], -jnp.inf)
        m_inter = b + m[..., None]
        m_t = jnp.maximum(jnp.max(logD, axis=-1), m_inter)
        dmat = jnp.exp(logD - m_t[..., None])
        w_inter = jnp.exp(m_inter - m_t)
        s = jnp.einsum('bhtd,bhsd->bhts', qc, kc) * dmat
        num = jnp.einsum('bhts,bhsv->bhtv', s, vc) + w_inter[..., None] * jnp.einsum('bhtd,bhdv->bhtv', qc, C)
        den = jnp.sum(s, axis=-1) + w_inter * jnp.einsum('bhtd,bhd->bht', qc, n)
        h = num / jnp.maximum(jnp.abs(den), jnp.exp(-m_t))[..., None]
        bL = b[..., -1]
        lw = bL[..., None] - b + ic
        m_new = jnp.maximum(bL + m, jnp.max(lw, axis=-1))
        wk = jnp.exp(lw - m_new[..., None])
        decay = jnp.exp(bL + m - m_new)
        C_new = decay[..., None, None] * C + jnp.einsum('bhs,bhsd,bhsv->bhdv', wk, kc, vc)
        n_new = decay[..., None] * n + jnp.einsum('bhs,bhsd->bhd', wk, kc)
        return (C_new, n_new, m_new), h

    final, hs = lax.scan(step, (C0, n0, m0), xs)
    h = jnp.moveaxis(hs, 0, 2).reshape(B, H, T, -1)
    return h, final


def mlstm_mixer(mq, mk, mv, mo, mi, mf, f_bias, norm_g, C0, n0, m0):
    B, T, _ = mq.shape
    f32 = jnp.float32

    def heads(a):
        return a.reshape(B, T, H_M, -1).transpose(0, 2, 1, 3).astype(f32)

    q = heads(mq)
    k = heads(mk) * (DK ** -0.5)
    v = heads(mv)
    ig = mi.reshape(B, T, 2, H_M).astype(f32).transpose(2, 0, 3, 1)
    logf = jax.nn.log_sigmoid(mf.reshape(B, T, 2, H_M).astype(f32) + f_bias.astype(f32)).transpose(2, 0, 3, 1)
    C0 = C0.astype(f32)
    n0 = n0.astype(f32)
    m0 = m0.astype(f32)
    hf, sf = mlstm_chunked(q, k, v, ig[0], logf[0], C0[:, 0], n0[:, 0], m0[:, 0])
    fl = lambda a: jnp.flip(a, axis=2)
    hb, sb = mlstm_chunked(fl(q), fl(k), fl(v), fl(ig[1]), fl(logf[1]), C0[:, 1], n0[:, 1], m0[:, 1])
    h = (hf + fl(hb)).transpose(0, 2, 1, 3)
    h = h * lax.rsqrt(jnp.mean(h * h, axis=-1, keepdims=True) + EPS)
    h = h.reshape(B, T, D_M) * norm_g.astype(f32) * jax.nn.sigmoid(mo.astype(f32))
    C_fin = jnp.stack([sf[0], sb[0]], axis=1)
    n_fin = jnp.stack([sf[1], sb[1]], axis=1)
    m_fin = jnp.stack([sf[2], sb[2]], axis=1)
    return h.astype(mq.dtype), C_fin, n_fin, m_fin


def ssd_chunked(x, dt, A, Bm, Cm, h0):
    B, T = x.shape[:2]
    nc = T // CHUNK

    def to_chunks(a):
        return jnp.moveaxis(a.reshape((B, nc, CHUNK) + a.shape[2:]), 1, 0)

    xs = (to_chunks(x), to_chunks(dt), to_chunks(Bm), to_chunks(Cm))
    causal = jnp.tril(jnp.ones((CHUNK, CHUNK), dtype=bool))

    def step(h, xc):
        xq, dtc, Bc, Cc = xc
        a = jnp.cumsum(dtc * A, axis=1)
        at = jnp.moveaxis(a, 1, -1)
        seg = jnp.exp(jnp.where(causal, at[..., :, None] - at[..., None, :], -jnp.inf))
        cb = jnp.einsum('btgn,bsgn->bgts', Cc, Bc)
        y = jnp.einsum('bgts,bgrts,bsgr,bsgrp->btgrp', cb, seg, dtc, xq)
        y = y + jnp.einsum('btgn,bgrpn->btgrp', Cc, h) * jnp.exp(a)[..., None]
        aL = a[:, -1]
        wk = jnp.exp(aL[:, None] - a) * dtc
        h_new = jnp.exp(aL)[..., None, None] * h + jnp.einsum('bsgr,bsgrp,bsgn->bgrpn', wk, xq, Bc)
        return h_new, y

    hT, ys = lax.scan(step, h0, xs)
    y = jnp.moveaxis(ys, 0, 1).reshape(x.shape)
    return y, hT


def ssm_mixer(sz, sxbc, sdt, conv_w, conv_b, dt_bias, a_log, d_skip, norm_g, h0):
    B, T, _ = sxbc.shape
    f32 = jnp.float32
    xbc = jax.nn.silu(depthwise_conv(sxbc, conv_w, conv_b))
    xs, Bm, Cm = jnp.split(xbc, [D_S, D_S + G_S * N_S], axis=-1)
    x = xs.reshape(B, T, G_S, R_S, P_S).astype(f32)
    Bm = Bm.reshape(B, T, G_S, N_S).astype(f32)
    Cm = Cm.reshape(B, T, G_S, N_S).astype(f32)
    dt = jax.nn.softplus(sdt.reshape(B, T, 2, H_S).astype(f32) + dt_bias.astype(f32)).reshape(B, T, 2, G_S, R_S)
    A = -jnp.exp(a_log.astype(f32)).reshape(2, G_S, R_S)
    h0 = h0.astype(f32).reshape(B, 2, G_S, R_S, P_S, N_S)
    yf, hf = ssd_chunked(x, dt[:, :, 0], A[0], Bm, Cm, h0[:, 0])
    fl = lambda a: jnp.flip(a, axis=1)
    yb, hb = ssd_chunked(fl(x), fl(dt[:, :, 1]), A[1], fl(Bm), fl(Cm), h0[:, 1])
    y = yf + fl(yb) + d_skip.astype(f32).reshape(G_S, R_S)[:, :, None] * x
    y = y.reshape(B, T, D_S) * jax.nn.silu(sz.astype(f32))
    y = rmsnorm(y, norm_g)
    hT = jnp.stack([hf, hb], axis=1).reshape(B, 2, H_S, P_S, N_S)
    return y.astype(sz.dtype), hT


def ctx_self_attention(q, k, v):
    B, L, H, d = q.shape
    nb = L // Q_BLOCK
    qb = jnp.moveaxis(q.reshape(B, nb, Q_BLOCK, H, d), 1, 0)
    scale = d ** -0.5

    def block(qi):
        s = jnp.einsum('bqhd,bkhd->bhqk', qi, k).astype(jnp.float32) * scale
        p = jax.nn.softmax(s, axis=-1).astype(v.dtype)
        return jnp.einsum('bhqk,bkhd->bqhd', p, v)

    o = lax.map(block, qb)
    return jnp.moveaxis(o, 0, 1).reshape(B, L, H * d)


def latent_neighbourhood_attention(q, k, v, ck, cv, rpb):
    B, T, H, d = q.shape
    rows = T // GRID_W
    kh = min(WIN_H, rows)
    ncb = GRID_W // Q_COLS
    j = np.arange(ncb)
    qcol = j[:, None] * Q_COLS + np.arange(Q_COLS)[None, :]
    kstart = np.clip(j * Q_COLS - WIN_W // 2, 0, GRID_W - K_COLS)
    kcol = kstart[:, None] + np.arange(K_COLS)[None, :]
    cstart = np.clip(qcol - WIN_W // 2, 0, GRID_W - WIN_W)
    col_ok = (kcol[:, None, :] >= cstart[:, :, None]) & (kcol[:, None, :] < cstart[:, :, None] + WIN_W)
    col_off = np.clip(kcol[:, None, :] - qcol[:, :, None] + WIN_W - 1, 0, 2 * WIN_W - 2)
    rpb_cols = rpb[:, :, col_off]
    col_ok = jnp.asarray(col_ok)[:, :, None, :]
    kg = k.reshape(B, rows, GRID_W, H, d)
    vg = v.reshape(B, rows, GRID_W, H, d)
    qg = jnp.moveaxis(q.reshape(B, rows, ncb, Q_COLS, H, d), 1, 0)
    scale = d ** -0.5
    n_loc = kh * K_COLS

    def row_block(args):
        r, qr = args
        rs = jnp.clip(r - kh // 2, 0, rows - kh)
        kr = lax.dynamic_slice_in_dim(kg, rs, kh, axis=1)[:, :, kcol]
        vr = lax.dynamic_slice_in_dim(vg, rs, kh, axis=1)[:, :, kcol]
        s_loc = jnp.einsum('bjqhd,bkjchd->bhjqkc', qr, kr).astype(jnp.float32) * scale
        row_off = rs + jnp.arange(kh) - r + WIN_H - 1
        bias = jnp.transpose(jnp.take(rpb_cols, row_off, axis=1), (0, 2, 3, 1, 4)).astype(jnp.float32)
        s_loc = jnp.where(col_ok, s_loc + bias, -jnp.inf).reshape(B, H, ncb, Q_COLS, n_loc)
        s_ctx = jnp.einsum('bjqhd,blhd->bhjql', qr, ck).astype(jnp.float32) * scale
        p = jax.nn.softmax(jnp.concatenate([s_loc, s_ctx], axis=-1), axis=-1).astype(v.dtype)
        p_loc = p[..., :n_loc].reshape(B, H, ncb, Q_COLS, kh, K_COLS)
        p_ctx = p[..., n_loc:]
        o = jnp.einsum('bhjqkc,bkjchd->bjqhd', p_loc, vr) + jnp.einsum('bhjql,blhd->bjqhd', p_ctx, cv)
        return o.reshape(B, GRID_W, H, d)

    o = lax.map(row_block, (jnp.arange(rows), qg))
    return jnp.moveaxis(o, 0, 1).reshape(B, T, H * d)


def token_mixer_context(h, w_in, f_bias, m_norm_g, conv_w, conv_b, dt_bias, a_log, d_skip, s_norm_g, w_out):
    B, L, _ = h.shape
    mq, mk, mv, mo, mi, mf, sz, sxbc, sdt, nq, nk, nv = project(h, w_in)
    C0 = jnp.zeros((B, 2, H_M, DK, DV), jnp.float32)
    n0 = jnp.zeros((B, 2, H_M, DK), jnp.float32)
    m0 = jnp.full((B, 2, H_M), NEG_INIT, jnp.float32)
    m_out, C, n, m = mlstm_mixer(mq, mk, mv, mo, mi, mf, f_bias, m_norm_g, C0, n0, m0)
    h0 = jnp.zeros((B, 2, H_S, P_S, N_S), jnp.float32)
    s_out, hs = ssm_mixer(sz, sxbc, sdt, conv_w, conv_b, dt_bias, a_log, d_skip, s_norm_g, h0)
    q = nq.reshape(B, L, H_NA, HEAD_DIM)
    k = nk.reshape(B, L, H_NA, HEAD_DIM)
    v = nv.reshape(B, L, H_NA, HEAD_DIM)
    a_out = ctx_self_attention(q, k, v)
    y = jnp.concatenate([m_out, s_out, a_out], axis=-1) @ w_out
    return y, k, v, C, n, m, hs


def token_mixer_latent(h, ck, cv, C0, n0, m0, h0, w_in, f_bias, m_norm_g, conv_w, conv_b, dt_bias, a_log,
                       d_skip, s_norm_g, rpb, w_out):
    B, T, _ = h.shape
    mq, mk, mv, mo, mi, mf, sz, sxbc, sdt, nq, nk, nv = project(h, w_in)
    m_out, _, _, _ = mlstm_mixer(mq, mk, mv, mo, mi, mf, f_bias, m_norm_g, C0, n0, m0)
    s_out, _ = ssm_mixer(sz, sxbc, sdt, conv_w, conv_b, dt_bias, a_log, d_skip, s_norm_g, h0)
    q = nq.reshape(B, T, H_NA, HEAD_DIM)
    k = nk.reshape(B, T, H_NA, HEAD_DIM)
    v = nv.reshape(B, T, H_NA, HEAD_DIM)
    a_out = latent_neighbourhood_attention(q, k, v, ck.astype(q.dtype), cv.astype(v.dtype), rpb)
    return jnp.concatenate([m_out, s_out, a_out], axis=-1) @ w_out


def ec_moe(h, router_w, w1, w3, w2):
    B, T, D = h.shape
    xt = h.reshape(B * T, D)
    cap = EC_FACTOR * (B * T) // N_EXPERTS
    aff = jax.nn.softmax((xt @ router_w).astype(jnp.float32), axis=-1)
    gate_vals, idx = lax.top_k(aff.T, cap)
    xe = xt[idx]
    hid = jax.nn.silu(jnp.einsum('ecd,edf->ecf', xe, w1)) * jnp.einsum('ecd,edf->ecf', xe, w3)
    ye = jnp.einsum('ecf,efd->ecd', hid, w2) * gate_vals[..., None]
    y = jnp.zeros_like(xt).at[idx.reshape(-1)].add(ye.reshape(-1, D).astype(xt.dtype))
    return y.reshape(B, T, D)


def setup_inputs(seed: int = 0) -> dict:
    key = jax.random.key(seed)
    ks = jax.random.split(key, 32)
    f32 = jnp.float32
    nrm = lambda kk, shp: jax.random.normal(kk, shp, f32)
    D = D_MODEL
    return {
        'x_prompt': nrm(ks[0], (BATCH, SEQ, D)),
        'x_sample': nrm(ks[1], (DEC_BATCH, DEC_SEQ, D)),
        'cache_na_k': nrm(ks[2], (DEC_BATCH, DEPTH, PAST_LEN, H_NA, HEAD_DIM)),
        'cache_na_v': nrm(ks[3], (DEC_BATCH, DEPTH, PAST_LEN, H_NA, HEAD_DIM)),
        'state_mlstm_c': 0.5 * nrm(ks[4], (DEC_BATCH, DEPTH, 2, H_M, DK, DV)),
        'state_mlstm_n': nrm(ks[5], (DEC_BATCH, DEPTH, 2, H_M, DK)),
        'state_mlstm_m': nrm(ks[6], (DEC_BATCH, DEPTH, 2, H_M)),
        'state_ssm': 0.5 * nrm(ks[7], (DEC_BATCH, DEPTH, 2, H_S, P_S, N_S)),
        'c': nrm(ks[8], (DEC_BATCH, D)),
        'c_ctx': nrm(ks[9], (D,)),
        'ada_w': 0.5 * D ** -0.5 * nrm(ks[10], (DEPTH, D, 6 * D)),
        'ada_b': 0.02 * nrm(ks[11], (DEPTH, 6 * D)),
        'norm1_g': 1.0 + 0.05 * nrm(ks[12], (DEPTH, D)),
        'norm2_g': 1.0 + 0.05 * nrm(ks[13], (DEPTH, D)),
        'w_in': D ** -0.5 * nrm(ks[14], (DEPTH, D, sum(SECTIONS))),
        'mlstm_f_bias': 3.0 + 0.5 * nrm(ks[15], (DEPTH, 2, H_M)),
        'mlstm_norm_g': 1.0 + 0.05 * nrm(ks[16], (DEPTH, D_M)),
        'conv_w': CONV_K ** -0.5 * nrm(ks[17], (DEPTH, CONV_K, D_XBC)),
        'conv_b': 0.02 * nrm(ks[18], (DEPTH, D_XBC)),
        'ssm_dt_bias': -3.0 + 0.5 * nrm(ks[19], (DEPTH, 2, H_S)),
        'ssm_a_log': jnp.log(jax.random.uniform(ks[20], (DEPTH, 2, H_S), f32, 1.0, 16.0)),
        'ssm_d': 1.0 + 0.1 * nrm(ks[21], (DEPTH, H_S)),
        'ssm_norm_g': 1.0 + 0.05 * nrm(ks[22], (DEPTH, D_S)),
        'na_rpb': 0.1 * nrm(ks[23], (DEPTH, H_NA, 2 * WIN_H - 1, 2 * WIN_W - 1)),
        'w_out': D_MIX ** -0.5 * nrm(ks[24], (DEPTH, D_MIX, D)),
        'router_w': D ** -0.5 * nrm(ks[25], (DEPTH, D, N_EXPERTS)),
        'exp_w1': D ** -0.5 * nrm(ks[26], (DEPTH, N_EXPERTS, D, D_FF)),
        'exp_w3': D ** -0.5 * nrm(ks[27], (DEPTH, N_EXPERTS, D, D_FF)),
        'exp_w2': D_FF ** -0.5 * nrm(ks[28], (DEPTH, N_EXPERTS, D_FF, D)),
        'final_g': 1.0 + 0.05 * nrm(ks[29], (D,)),
    }


def reference(x_prompt, x_sample, cache_na_k, cache_na_v, state_mlstm_c, state_mlstm_n, state_mlstm_m,
              state_ssm, c, c_ctx, ada_w, ada_b, norm1_g, norm2_g, w_in, mlstm_f_bias, mlstm_norm_g,
              conv_w, conv_b, ssm_dt_bias, ssm_a_log, ssm_d, ssm_norm_g, na_rpb, w_out, router_w,
              exp_w1, exp_w3, exp_w2, final_g):
    xp = x_prompt
    xs = x_sample
    ks, vs, cs, ns, ms, hs = [], [], [], [], [], []
    for l in range(DEPTH):
        mod = jax.nn.silu(c_ctx) @ ada_w[l] + ada_b[l]
        sh1, sc1, g1, sh2, sc2, g2 = jnp.split(mod, 6, axis=-1)
        y, k_l, v_l, c_l, n_l, m_l, h_l = token_mixer_context(
            modulate(xp, norm1_g[l], sh1, sc1), w_in[l], mlstm_f_bias[l], mlstm_norm_g[l], conv_w[l],
            conv_b[l], ssm_dt_bias[l], ssm_a_log[l], ssm_d[l], ssm_norm_g[l], w_out[l])
        xp = xp + g1 * y
        xp = xp + g2 * ec_moe(modulate(xp, norm2_g[l], sh2, sc2), router_w[l], exp_w1[l], exp_w3[l], exp_w2[l])
        ks.append(k_l)
        vs.append(v_l)
        cs.append(c_l)
        ns.append(n_l)
        ms.append(m_l)
        hs.append(h_l)
        mod = (jax.nn.silu(c) @ ada_w[l] + ada_b[l])[:, None, :]
        sh1, sc1, g1, sh2, sc2, g2 = jnp.split(mod, 6, axis=-1)
        y = token_mixer_latent(
            modulate(xs, norm1_g[l], sh1, sc1), cache_na_k[:, l], cache_na_v[:, l], state_mlstm_c[:, l],
            state_mlstm_n[:, l], state_mlstm_m[:, l], state_ssm[:, l], w_in[l], mlstm_f_bias[l],
            mlstm_norm_g[l], conv_w[l], conv_b[l], ssm_dt_bias[l], ssm_a_log[l], ssm_d[l], ssm_norm_g[l],
            na_rpb[l], w_out[l])
        xs = xs + g1 * y
        xs = xs + g2 * ec_moe(modulate(xs, norm2_g[l], sh2, sc2), router_w[l], exp_w1[l], exp_w3[l], exp_w2[l])
    y_prompt = rmsnorm(xp, final_g)
    y_sample = rmsnorm(xs, final_g)
    new_na_k = jnp.stack(ks, axis=1)
    new_na_v = jnp.stack(vs, axis=1)
    new_mlstm_c = jnp.stack(cs, axis=1)
    new_mlstm_n = jnp.stack(ns, axis=1)
    new_mlstm_m = jnp.stack(ms, axis=1)
    new_ssm = jnp.stack(hs, axis=1)
    return (y_prompt, y_sample, new_na_k, new_na_v, new_mlstm_c, new_mlstm_n, new_mlstm_m, new_ssm)
```

```python
import functools

import numpy as np
import jax
import jax.numpy as jnp
from jax import lax
from jax.experimental import pallas as pl
from jax.experimental.pallas import tpu as pltpu

F32 = jnp.float32
BF16 = jnp.bfloat16
HI = lax.Precision.HIGHEST

D = 1024
DEPTH = 4
HD = 64
H_M = 4
D_M = H_M * HD
H_S = 6
D_S = H_S * HD
G_S = 2
R_S = H_S // G_S
N_S = 64
D_XBC = D_S + 2 * G_S * N_S
H_NA = 6
D_NA = H_NA * HD
GRID_W = 64
WIN_H = 8
WIN_W = 16
N_EXPERTS = 16
EC_FACTOR = 2
EPS = 1e-6
NEG_INIT = -1e30
GATE_W = 128
MI0, MF0, DT0 = 0, 2 * H_M, 4 * H_M
W_PACKED = 3 * D_M + D_M + D_S + D_XBC + 3 * D_NA + GATE_W

VMEM_LIMIT = 56 * 1024 * 1024


def _cparams(sem):
    return pltpu.CompilerParams(dimension_semantics=sem, vmem_limit_bytes=VMEM_LIMIT)


def _sigmoid(x):
    return 1.0 / (1.0 + jnp.exp(-x))


def _softplus(x):
    return jnp.maximum(x, 0.0) + jnp.log1p(jnp.exp(-jnp.abs(x)))


def _dot(a, b):
    return jnp.dot(a, b, preferred_element_type=F32)


def _dot_nt(a, b):
    return lax.dot_general(a, b, (((1,), (1,)), ((), ())), preferred_element_type=F32)


def _dot_tn(a, b):
    return lax.dot_general(a, b, (((0,), (0,)), ((), ())), preferred_element_type=F32)


def _mod_kernel(c_ref, w_ref, b_ref, o_ref):
    cv = c_ref[...]
    s = cv * _sigmoid(cv)
    o_ref[...] = jnp.dot(s, w_ref[...], precision=HI, preferred_element_type=F32) + b_ref[...]


def _modulation(cvec, ada_w, ada_b):
    tn = 1536
    return pl.pallas_call(
        _mod_kernel,
        grid=(DEPTH, 6 * D // tn),
        in_specs=[pl.BlockSpec((8, D), lambda l, j: (0, 0)),
                  pl.BlockSpec((None, D, tn), lambda l, j: (l, 0, j)),
                  pl.BlockSpec((None, 1, tn), lambda l, j: (l, 0, j))],
        out_specs=pl.BlockSpec((None, 8, tn), lambda l, j: (l, 0, j)),
        out_shape=jax.ShapeDtypeStruct((DEPTH, 8, 6 * D), F32),
        compiler_params=_cparams(("arbitrary", "arbitrary")),
    )(cvec, ada_w, ada_b.reshape(DEPTH, 1, 6 * D))


_IN_SEGS = (("qkv", 0, 768), ("mo", 768, 1024), ("sz", 1024, 1408), ("xbc", 1408, 2048),
            ("nq", 2048, 2432), ("nk", 2432, 2816), ("nv", 2816, 3200), ("gate", 3200, 3328))


def _inproj_kernel(x_ref, g_ref, sh_ref, sc_ref, w_ref, *outs):
    x = x_ref[...]
    ms = jnp.mean(x * x, axis=-1, keepdims=True)
    h = x * lax.rsqrt(ms + EPS) * g_ref[...]
    h = h * (1.0 + sc_ref[...]) + sh_ref[...]
    hb = h.astype(BF16)
    for o_ref, (_, lo, hi) in zip(outs, _IN_SEGS):
        o_ref[...] = _dot(hb, w_ref[:, lo:hi])


def _in_projection(x, norm_g, shift, scale, w_packed, seq_len):
    n = x.shape[0]
    tm = 512
    groups = shift.shape[0]
    if groups == 1:
        gidx = lambda i: (0, 0, 0)
    else:
        gidx = lambda i: ((i * tm) // seq_len, 0, 0)
    out_shape = tuple(jax.ShapeDtypeStruct((n, hi - lo), F32) for _, lo, hi in _IN_SEGS)
    out_specs = tuple(pl.BlockSpec((tm, hi - lo), lambda i: (i, 0)) for _, lo, hi in _IN_SEGS)
    return pl.pallas_call(
        _inproj_kernel,
        grid=(n // tm,),
        in_specs=[pl.BlockSpec((tm, D), lambda i: (i, 0)),
                  pl.BlockSpec((1, D), lambda i: (0, 0)),
                  pl.BlockSpec((None, 1, D), gidx),
                  pl.BlockSpec((None, 1, D), gidx),
                  pl.BlockSpec((D, W_PACKED), lambda i: (0, 0))],
        out_specs=out_specs,
        out_shape=out_shape,
        compiler_params=_cparams(("arbitrary",)),
    )(x, norm_g.reshape(1, D), shift, scale, w_packed)


def _conv_kernel(cur_ref, prev_ref, next_ref, w_ref, b_ref, o_ref, *, tiles_per_seq):
    i = pl.program_id(0)
    tt = cur_ref.shape[0]
    cur = cur_ref[...]
    first = (i % tiles_per_seq) == 0
    last = (i % tiles_per_seq) == tiles_per_seq - 1
    prow = jnp.where(first, 0.0, prev_ref[7:8, :])
    nrow = jnp.where(last, 0.0, next_ref[0:1, :])
    rows = lax.broadcasted_iota(jnp.int32, cur.shape, 0)
    prev = jnp.where(rows == 0, prow, pltpu.roll(cur, 1, 0))
    nxt = jnp.where(rows == tt - 1, nrow, pltpu.roll(cur, tt - 1, 0))
    y = prev * w_ref[0:1, :] + cur * w_ref[1:2, :] + nxt * w_ref[2:3, :] + b_ref[...]
    o_ref[...] = y * _sigmoid(y)


def _conv_silu(sxbc, conv_w, conv_b, seq_len):
    n = sxbc.shape[0]
    tt = min(seq_len, 512)
    tps = seq_len // tt
    nb8 = n // 8
    return pl.pallas_call(
        functools.partial(_conv_kernel, tiles_per_seq=tps),
        grid=(n // tt,),
        in_specs=[pl.BlockSpec((tt, D_XBC), lambda i: (i, 0)),
                  pl.BlockSpec((8, D_XBC), lambda i: (jnp.maximum(i * (tt // 8) - 1, 0), 0)),
                  pl.BlockSpec((8, D_XBC), lambda i: (jnp.minimum((i + 1) * (tt // 8), nb8 - 1), 0)),
                  pl.BlockSpec((3, D_XBC), lambda i: (0, 0)),
                  pl.BlockSpec((1, D_XBC), lambda i: (0, 0))],
        out_specs=pl.BlockSpec((tt, D_XBC), lambda i: (i, 0)),
        out_shape=jax.ShapeDtypeStruct((n, D_XBC), F32),
        compiler_params=_cparams(("arbitrary",)),
    )(sxbc, sxbc, sxbc, conv_w, conv_b.reshape(1, D_XBC))


def _tri_masks(ch):
    r = lax.broadcasted_iota(jnp.int32, (ch, ch), 0)
    c = lax.broadcasted_iota(jnp.int32, (ch, ch), 1)
    return r >= c, r <= c


def _mlstm_kernel(qkv_f, qkv_b, gt_f, gt_b, fb_ref, c0_ref, m0_ref,
                  hf_ref, hb_ref, cfin_ref, mfin_ref, c_scr, m_scr, *, n_tiles, ch):
    j = pl.program_id(1)

    @pl.when(j == 0)
    def _():
        c_scr[...] = c0_ref[...]
        m_scr[...] = m0_ref[...]

    tt = qkv_f.shape[0]
    nch = tt // ch
    fb = fb_ref[...]
    low, upp = _tri_masks(ch)
    ones_lane = (lax.broadcasted_iota(jnp.int32, (ch, HD), 1) == 0).astype(F32)

    def chunk(ci, carry):
        for d in range(2):
            qkv_ref = (qkv_f, qkv_b)[d]
            g_ref = (gt_f, gt_b)[d]
            out_ref = (hf_ref, hb_ref)[d]
            mask = (low, upp)[d]
            c0 = pl.multiple_of((ci if d == 0 else nch - 1 - ci) * ch, ch)
            g = g_ref[pl.ds(c0, ch), :]
            z = g + fb
            logf = jnp.minimum(z, 0.0) - jnp.log1p(jnp.exp(-jnp.abs(z)))
            bt = jnp.dot(mask.astype(F32), logf, precision=HI, preferred_element_type=F32)
            bt_t = bt.T
            g_t = g.T
            edge = ch - 1 if d == 0 else 0
            for h in range(H_M):
                idx = d * H_M + h
                q = qkv_ref[pl.ds(c0, ch), h * HD:(h + 1) * HD].astype(BF16)
                k = (qkv_ref[pl.ds(c0, ch), D_M + h * HD:D_M + (h + 1) * HD] * (HD ** -0.5))
                v = qkv_ref[pl.ds(c0, ch), 2 * D_M + h * HD:2 * D_M + (h + 1) * HD]
                vaug = jnp.concatenate([v, ones_lane], axis=-1).astype(BF16)
                b_col = bt[:, MF0 + idx:MF0 + idx + 1]
                b_row = bt_t[MF0 + idx:MF0 + idx + 1, :]
                i_row = g_t[MI0 + idx:MI0 + idx + 1, :]
                i_col = g[:, MI0 + idx:MI0 + idx + 1]
                m_prev = m_scr[idx][0:1, 0:1]
                caug = c_scr[idx]
                logd = jnp.where(mask, b_col - b_row + i_row, -jnp.inf)
                m_inter = b_col + m_prev
                m_t = jnp.maximum(jnp.max(logd, axis=-1, keepdims=True), m_inter)
                dmat = jnp.exp(logd - m_t)
                w_inter = jnp.exp(m_inter - m_t)
                s = (_dot_nt(q, k.astype(BF16)) * dmat).astype(BF16)
                tot = _dot(s, vaug) + w_inter * _dot(q, caug.astype(BF16))
                den = jnp.maximum(jnp.abs(tot[:, HD:HD + 1]), jnp.exp(-m_t))
                out_ref[pl.ds(c0, ch), h * HD:(h + 1) * HD] = tot[:, 0:HD] / den
                b_l = bt[edge:edge + 1, MF0 + idx:MF0 + idx + 1]
                lw = b_l - b_col + i_col
                m_new = jnp.maximum(b_l + m_prev, jnp.max(lw, axis=0, keepdims=True))
                wk = jnp.exp(lw - m_new)
                decay = jnp.exp(b_l + m_prev - m_new)
                c_scr[idx] = decay * caug + _dot_tn((k * wk).astype(BF16), vaug)
                m_scr[idx] = jnp.broadcast_to(m_new, (8, 128))
        return carry

    lax.fori_loop(0, nch, chunk, 0)

    @pl.when(j == n_tiles - 1)
    def _():
        cfin_ref[...] = c_scr[...]
        mfin_ref[...] = m_scr[...]


def _mlstm_scan(qkv, gate, f_bias, caug0, m0, batch, seq_len):
    n = qkv.shape[0]
    tt = min(seq_len, 512)
    nt = seq_len // tt
    fb = jnp.zeros((1, GATE_W), F32).at[0, MF0:MF0 + 2 * H_M].set(f_bias.reshape(-1))
    fwd = lambda b, j: (b * nt + j, 0)
    bwd = lambda b, j: (b * nt + nt - 1 - j, 0)
    st4 = lambda b, j: (b, 0, 0, 0)
    return pl.pallas_call(
        functools.partial(_mlstm_kernel, n_tiles=nt, ch=128),
        grid=(batch, nt),
        in_specs=[pl.BlockSpec((tt, 3 * D_M), fwd), pl.BlockSpec((tt, 3 * D_M), bwd),
                  pl.BlockSpec((tt, GATE_W), fwd), pl.BlockSpec((tt, GATE_W), bwd),
                  pl.BlockSpec((1, GATE_W), lambda b, j: (0, 0)),
                  pl.BlockSpec((None, 2 * H_M, HD, 128), st4),
                  pl.BlockSpec((None, 2 * H_M, 8, 128), st4)],
        out_specs=(pl.BlockSpec((tt, D_M), fwd), pl.BlockSpec((tt, D_M), bwd),
                   pl.BlockSpec((None, 2 * H_M, HD, 128), st4),
                   pl.BlockSpec((None, 2 * H_M, 8, 128), st4)),
        out_shape=(jax.ShapeDtypeStruct((n, D_M), F32), jax.ShapeDtypeStruct((n, D_M), F32),
                   jax.ShapeDtypeStruct((batch, 2 * H_M, HD, 128), F32),
                   jax.ShapeDtypeStruct((batch, 2 * H_M, 8, 128), F32)),
        scratch_shapes=[pltpu.VMEM((2 * H_M, HD, 128), F32), pltpu.VMEM((2 * H_M, 8, 128), F32)],
        compiler_params=_cparams(("arbitrary", "arbitrary")),
    )(qkv, qkv, gate, gate, fb, caug0, m0)


def _ssd_kernel(xbc_f, xbc_b, gt_f, gt_b, dtb_ref, alog_ref, h0_ref,
                yf_ref, yb_ref, hfin_ref, h_scr, *, n_tiles, ch):
    j = pl.program_id(1)

    @pl.when(j == 0)
    def _():
        h_scr[...] = h0_ref[...]

    tt = xbc_f.shape[0]
    nch = tt // ch
    dtb = dtb_ref[...]
    a_neg = -jnp.exp(alog_ref[...])
    low, upp = _tri_masks(ch)

    def chunk(ci, carry):
        for d in range(2):
            x_ref = (xbc_f, xbc_b)[d]
            g_ref = (gt_f, gt_b)[d]
            out_ref = (yf_ref, yb_ref)[d]
            mask = (low, upp)[d]
            c0 = pl.multiple_of((ci if d == 0 else nch - 1 - ci) * ch, ch)
            dt = _softplus(g_ref[pl.ds(c0, ch), :] + dtb)
            a = jnp.dot(mask.astype(F32), dt * a_neg, precision=HI, preferred_element_type=F32)
            a_t = a.T
            dt_t = dt.T
            edge = ch - 1 if d == 0 else 0
            for grp in range(G_S):
                bm = x_ref[pl.ds(c0, ch), D_S + grp * N_S:D_S + (grp + 1) * N_S]
                cm = x_ref[pl.ds(c0, ch), D_S + G_S * N_S + grp * N_S:D_S + G_S * N_S + (grp + 1) * N_S].astype(BF16)
                cb = _dot_nt(cm, bm.astype(BF16))
                for r in range(R_S):
                    hh = grp * R_S + r
                    idx = d * H_S + hh
                    ln = DT0 + idx
                    a_col = a[:, ln:ln + 1]
                    a_row = a_t[ln:ln + 1, :]
                    seg = jnp.exp(jnp.where(mask, a_col - a_row, -jnp.inf))
                    mm = (cb * seg * dt_t[ln:ln + 1, :]).astype(BF16)
                    xh = x_ref[pl.ds(c0, ch), hh * HD:(hh + 1) * HD].astype(BF16)
                    h_t = h_scr[idx]
                    y = _dot(mm, xh) + _dot(cm, h_t.astype(BF16)) * jnp.exp(a_col)
                    out_ref[pl.ds(c0, ch), hh * HD:(hh + 1) * HD] = y
                    a_l = a[edge:edge + 1, ln:ln + 1]
                    wk = jnp.exp(a_l - a_col) * dt[:, ln:ln + 1]
                    h_scr[idx] = jnp.exp(a_l) * h_t + _dot_tn((bm * wk).astype(BF16), xh)
        return carry

    lax.fori_loop(0, nch, chunk, 0)

    @pl.when(j == n_tiles - 1)
    def _():
        hfin_ref[...] = h_scr[...]


def _ssd_scan(xbc, gate, dt_bias, a_log, h0_t, batch, seq_len):
    n = xbc.shape[0]
    tt = min(seq_len, 512)
    nt = seq_len // tt
    dtb = jnp.zeros((1, GATE_W), F32).at[0, DT0:DT0 + 2 * H_S].set(dt_bias.reshape(-1))
    alog = jnp.zeros((1, GATE_W), F32).at[0, DT0:DT0 + 2 * H_S].set(a_log.reshape(-1))
    fwd = lambda b, j: (b * nt + j, 0)
    bwd = lambda b, j: (b * nt + nt - 1 - j, 0)
    st4 = lambda b, j: (b, 0, 0, 0)
    return pl.pallas_call(
        functools.partial(_ssd_kernel, n_tiles=nt, ch=128),
        grid=(batch, nt),
        in_specs=[pl.BlockSpec((tt, D_XBC), fwd), pl.BlockSpec((tt, D_XBC), bwd),
                  pl.BlockSpec((tt, GATE_W), fwd), pl.BlockSpec((tt, GATE_W), bwd),
                  pl.BlockSpec((1, GATE_W), lambda b, j: (0, 0)),
                  pl.BlockSpec((1, GATE_W), lambda b, j: (0, 0)),
                  pl.BlockSpec((None, 2 * H_S, N_S, HD), st4)],
        out_specs=(pl.BlockSpec((tt, D_S), fwd), pl.BlockSpec((tt, D_S), bwd),
                   pl.BlockSpec((None, 2 * H_S, N_S, HD), st4)),
        out_shape=(jax.ShapeDtypeStruct((n, D_S), F32), jax.ShapeDtypeStruct((n, D_S), F32),
                   jax.ShapeDtypeStruct((batch, 2 * H_S, N_S, HD), F32)),
        scratch_shapes=[pltpu.VMEM((2 * H_S, N_S, HD), F32)],
        compiler_params=_cparams(("arbitrary", "arbitrary")),
    )(xbc, xbc, gate, gate, dtb, alog, h0_t)


def _ctx_attn_kernel(q_ref, k_ref, v_ref, o_ref):
    for h in range(H_NA):
        sl = slice(h * HD, (h + 1) * HD)
        q = (q_ref[:, sl] * (HD ** -0.5)).astype(BF16)
        s = _dot_nt(q, k_ref[:, sl].astype(BF16))
        m = jnp.max(s, axis=-1, keepdims=True)
        p = jnp.exp(s - m)
        l = jnp.sum(p, axis=-1, keepdims=True)
        o_ref[:, sl] = _dot(p.astype(BF16), v_ref[:, sl].astype(BF16)) / l


def _ctx_attention(q, k, v, batch, seq_len):
    spec = pl.BlockSpec((seq_len, D_NA), lambda b: (b, 0))
    return pl.pallas_call(
        _ctx_attn_kernel,
        grid=(batch,),
        in_specs=[spec, spec, spec],
        out_specs=spec,
        out_shape=jax.ShapeDtypeStruct(q.shape, F32),
        compiler_params=_cparams(("arbitrary",)),
    )(q, k, v)


def _row_start(r, rows):
    return jnp.clip(r - WIN_H // 2, 0, rows - WIN_H)


def _natten_kernel(q_ref, k_ref, v_ref, ck_ref, cv_ref, bias_ref, o_ref, *, rows):
    r = pl.program_id(1)
    k0 = pl.multiple_of(_row_start(r, rows) * GRID_W, GRID_W)
    nloc = WIN_H * GRID_W
    for h in range(H_NA):
        sl = slice(h * HD, (h + 1) * HD)
        q = (q_ref[:, sl] * (HD ** -0.5)).astype(BF16)
        kw = k_ref[pl.ds(k0, nloc), sl].astype(BF16)
        vw = v_ref[pl.ds(k0, nloc), sl].astype(BF16)
        s_loc = _dot_nt(q, kw) + bias_ref[h]
        s_ctx = _dot_nt(q, ck_ref[:, sl].astype(BF16))
        m = jnp.maximum(jnp.max(s_loc, axis=-1, keepdims=True), jnp.max(s_ctx, axis=-1, keepdims=True))
        p_loc = jnp.exp(s_loc - m)
        p_ctx = jnp.exp(s_ctx - m)
        l = jnp.sum(p_loc, axis=-1, keepdims=True) + jnp.sum(p_ctx, axis=-1, keepdims=True)
        o = _dot(p_loc.astype(BF16), vw) + _dot(p_ctx.astype(BF16), cv_ref[:, sl].astype(BF16))
        o_ref[:, sl] = o / l


def _natten_bias(rpb, rows):
    qc = np.arange(GRID_W)[:, None]
    kc = np.arange(GRID_W)[None, :]
    cstart = np.clip(qc - WIN_W // 2, 0, GRID_W - WIN_W)
    ok = (kc >= cstart) & (kc < cstart + WIN_W)
    col_off = np.clip(kc - qc + WIN_W - 1, 0, 2 * WIN_W - 2)
    t = jnp.where(jnp.asarray(ok)[None, None], rpb[:, :, col_off], -jnp.inf)
    var = np.arange(WIN_H)[:, None] + np.arange(WIN_H)[None, :]
    b = t[:, var]
    return b.transpose(0, 1, 3, 2, 4).reshape(H_NA, WIN_H, GRID_W, WIN_H * GRID_W).astype(F32)


def _neighbourhood_attention(q, k, v, ck, cv, bias, batch, seq_len):
    rows = seq_len // GRID_W
    past = ck.shape[0] // batch
    bvar = lambda b, r: (0, _row_start(r, rows) - r + WIN_H - 1, 0, 0)
    return pl.pallas_call(
        functools.partial(_natten_kernel, rows=rows),
        grid=(batch, rows),
        in_specs=[pl.BlockSpec((GRID_W, D_NA), lambda b, r: (b * rows + r, 0)),
                  pl.BlockSpec((seq_len, D_NA), lambda b, r: (b, 0)),
                  pl.BlockSpec((seq_len, D_NA), lambda b, r: (b, 0)),
                  pl.BlockSpec((past, D_NA), lambda b, r: (b, 0)),
                  pl.BlockSpec((past, D_NA), lambda b, r: (b, 0)),
                  pl.BlockSpec((H_NA, None, GRID_W, WIN_H * GRID_W), bvar)],
        out_specs=pl.BlockSpec((GRID_W, D_NA), lambda b, r: (b * rows + r, 0)),
        out_shape=jax.ShapeDtypeStruct(q.shape, F32),
        compiler_params=_cparams(("arbitrary", "arbitrary")),
    )(q, k, v, ck, cv, bias)


def _outproj_kernel(x_ref, hf_ref, hb_ref, mo_ref, yf_ref, yb_ref, xbc_ref, sz_ref, a_ref,
                    mg_ref, dsk_ref, sg_ref, bd_ref, w_ref, g1_ref, n2_ref, sh2_ref, sc2_ref, rw_ref,
                    xo_ref, h2_ref, aff_ref):
    hm = hf_ref[...] + hb_ref[...]
    ssq = jnp.dot(hm * hm, bd_ref[...], precision=HI, preferred_element_type=F32)
    m_out = hm * lax.rsqrt(ssq * (1.0 / HD) + EPS) * mg_ref[...] * _sigmoid(mo_ref[...])
    sz = sz_ref[...]
    ys = (yf_ref[...] + yb_ref[...] + dsk_ref[...] * xbc_ref[:, 0:D_S]) * (sz * _sigmoid(sz))
    s_out = ys * lax.rsqrt(jnp.mean(ys * ys, axis=-1, keepdims=True) + EPS) * sg_ref[...]
    y = (_dot(m_out.astype(BF16), w_ref[0:D_M, :])
         + _dot(s_out.astype(BF16), w_ref[D_M:D_M + D_S, :])
         + _dot(a_ref[...].astype(BF16), w_ref[D_M + D_S:D, :]))
    xn = x_ref[...] + g1_ref[...] * y
    xo_ref[...] = xn
    h2 = xn * lax.rsqrt(jnp.mean(xn * xn, axis=-1, keepdims=True) + EPS) * n2_ref[...]
    h2 = h2 * (1.0 + sc2_ref[...]) + sh2_ref[...]
    h2_ref[...] = h2.astype(BF16)
    logits = lax.dot_general(rw_ref[...], h2, (((1,), (1,)), ((), ())), precision=HI,
                             preferred_element_type=F32)
    mx = jnp.max(logits, axis=0, keepdims=True)
    ex = jnp.exp(logits - mx)
    aff_ref[...] = ex / jnp.sum(ex, axis=0, keepdims=True)


def _out_projection(x, hf, hb, mo, yf, yb, xbc, sz, a_out, mnorm_g, d_skip, snorm_g, w_out_bf,
                    g1, norm2_g, sh2, sc2, router_wt, seq_len):
    n = x.shape[0]
    tm = 512
    groups = g1.shape[0]
    if groups == 1:
        gidx = lambda i: (0, 0, 0)
    else:
        gidx = lambda i: ((i * tm) // seq_len, 0, 0)
    row = lambda w: pl.BlockSpec((tm, w), lambda i: (i, 0))
    const = lambda s: pl.BlockSpec(s, lambda i: (0,) * len(s))
    hid = np.arange(D_M) // HD
    blockdiag = jnp.asarray((hid[:, None] == hid[None, :]).astype(np.float32))
    return pl.pallas_call(
        _outproj_kernel,
        grid=(n // tm,),
        in_specs=[row(D), row(D_M), row(D_M), row(D_M), row(D_S), row(D_S), row(D_XBC), row(D_S), row(D_NA),
                  const((1, D_M)), const((1, D_S)), const((1, D_S)), const((D_M, D_M)), const((D, D)),
                  pl.BlockSpec((None, 1, D), gidx), const((1, D)),
                  pl.BlockSpec((None, 1, D), gidx), pl.BlockSpec((None, 1, D), gidx),
                  const((N_EXPERTS, D))],
        out_specs=(row(D), row(D), pl.BlockSpec((N_EXPERTS, tm), lambda i: (0, i))),
        out_shape=(jax.ShapeDtypeStruct((n, D), F32), jax.ShapeDtypeStruct((n, D), BF16),
                   jax.ShapeDtypeStruct((N_EXPERTS, n), F32)),
        compiler_params=_cparams(("arbitrary",)),
    )(x, hf, hb, mo, yf, yb, xbc, sz, a_out,
      mnorm_g.reshape(1, D_M), jnp.repeat(d_skip, HD).reshape(1, D_S), snorm_g.reshape(1, D_S), blockdiag,
      w_out_bf, g1, norm2_g.reshape(1, D), sh2, sc2, router_wt)


def _select_kernel(aff_ref, o_ref, *, cap):
    aff = aff_ref[...]
    bits = pltpu.bitcast(aff, jnp.int32)
    n_tok = aff.shape[1]

    def step(i, prefix):
        cand = prefix | (jnp.int32(1) << (30 - i))
        cnt = jnp.sum((bits >= cand).astype(F32), axis=1, keepdims=True)
        return jnp.where(cnt >= cap, cand, prefix)

    thr = lax.fori_loop(0, 31, step, jnp.zeros((aff.shape[0], 1), jnp.int32))
    gt = bits > thr
    eq = bits == thr
    need = cap - jnp.sum(gt.astype(F32), axis=1, keepdims=True)
    r_i = lax.broadcasted_iota(jnp.int32, (128, 128), 0)
    c_i = lax.broadcasted_iota(jnp.int32, (128, 128), 1)
    strict = (r_i < c_i).astype(BF16)
    run = jnp.zeros((aff.shape[0], 1), F32)
    for blk in range(n_tok // 128):
        sl = slice(blk * 128, (blk + 1) * 128)
        e = eq[:, sl]
        rank = _dot(e.astype(BF16), strict) + run
        keep = gt[:, sl] | (e & (rank < need))
        o_ref[:, sl] = jnp.where(keep, aff[:, sl], 0.0)
        run = run + jnp.sum(e.astype(F32), axis=1, keepdims=True)


def _select(aff_t):
    n = aff_t.shape[1]
    cap = EC_FACTOR * n // N_EXPERTS
    return pl.pallas_call(
        functools.partial(_select_kernel, cap=float(cap)),
        out_shape=jax.ShapeDtypeStruct(aff_t.shape, F32),
        compiler_params=pltpu.CompilerParams(vmem_limit_bytes=VMEM_LIMIT),
    )(aff_t)


def _moe_kernel(h2_ref, gs_ref, x_ref, g2_ref, w1_ref, w3_ref, w2_ref, o_ref, acc_ref):
    e = pl.program_id(1)

    @pl.when(e == 0)
    def _():
        acc_ref[...] = jnp.zeros_like(acc_ref)

    h = h2_ref[...]
    a = _dot(h, w1_ref[...])
    b = _dot(h, w3_ref[...])
    hid = (a * _sigmoid(a) * b).astype(BF16)
    y = _dot(hid, w2_ref[...])
    gs = gs_ref[...]
    lane = lax.broadcasted_iota(jnp.int32, gs.shape, 1)
    gcol = jnp.sum(jnp.where(lane == e, gs, 0.0), axis=-1, keepdims=True)
    acc_ref[...] += y * gcol

    @pl.when(e == N_EXPERTS - 1)
    def _():
        o_ref[...] = x_ref[...] + g2_ref[...] * acc_ref[...]


def _moe(h2, gsel, x, g2, w1, w3, w2, seq_len):
    n = x.shape[0]
    tm = 1024
    groups = g2.shape[0]
    if groups == 1:
        gidx = lambda i, e: (0, 0, 0)
    else:
        gidx = lambda i, e: ((i * tm) // seq_len, 0, 0)
    wspec = pl.BlockSpec((None, D, D), lambda i, e: (e, 0, 0))
    return pl.pallas_call(
        _moe_kernel,
        grid=(n // tm, N_EXPERTS),
        in_specs=[pl.BlockSpec((tm, D), lambda i, e: (i, 0)),
                  pl.BlockSpec((tm, N_EXPERTS), lambda i, e: (i, 0)),
                  pl.BlockSpec((tm, D), lambda i, e: (i, 0)),
                  pl.BlockSpec((None, 1, D), gidx),
                  wspec, wspec, wspec],
        out_specs=pl.BlockSpec((tm, D), lambda i, e: (i, 0)),
        out_shape=jax.ShapeDtypeStruct((n, D), F32),
        scratch_shapes=[pltpu.VMEM((tm, D), F32)],
        compiler_params=_cparams(("arbitrary", "arbitrary")),
    )(h2, gsel, x, g2, w1, w3, w2)


def _final_kernel(x_ref, g_ref, o_ref):
    x = x_ref[...]
    o_ref[...] = x * lax.rsqrt(jnp.mean(x * x, axis=-1, keepdims=True) + EPS) * g_ref[...]


def _final_norm(x, g):
    n = x.shape[0]
    tm = 1024
    return pl.pallas_call(
        _final_kernel,
        grid=(n // tm,),
        in_specs=[pl.BlockSpec((tm, D), lambda i: (i, 0)), pl.BlockSpec((1, D), lambda i: (0, 0))],
        out_specs=pl.BlockSpec((tm, D), lambda i: (i, 0)),
        out_shape=jax.ShapeDtypeStruct((n, D), F32),
        compiler_params=_cparams(("arbitrary",)),
    )(x, g.reshape(1, D))


def _layer(x, mod, prm, batch, seq_len, caug0, m0, h0_t, ck=None, cv=None, bias=None):
    sh1, sc1, g1, sh2, sc2, g2 = [m[:, None, :] for m in jnp.split(mod, 6, axis=-1)]
    qkv, mo, sz, sxbc, nq, nk, nv, gate = _in_projection(x, prm["norm1_g"], sh1, sc1, prm["w_in"], seq_len)
    xbc = _conv_silu(sxbc, prm["conv_w"], prm["conv_b"], seq_len)
    hf, hb, cfin, mfin = _mlstm_scan(qkv, gate, prm["f_bias"], caug0, m0, batch, seq_len)
    yf, yb, hfin = _ssd_scan(xbc, gate, prm["dt_bias"], prm["a_log"], h0_t, batch, seq_len)
    if ck is None:
        a_out = _ctx_attention(nq, nk, nv, batch, seq_len)
    else:
        a_out = _neighbourhood_attention(nq, nk, nv, ck, cv, bias, batch, seq_len)
    xn, h2, aff_t = _out_projection(x, hf, hb, mo, yf, yb, xbc, sz, a_out, prm["mnorm_g"], prm["d_skip"],
                                    prm["snorm_g"], prm["w_out"], g1, prm["norm2_g"], sh2, sc2,
                                    prm["router_wt"], seq_len)
    gsel = _select(aff_t)
    xo = _moe(h2, gsel.T, xn, g2, prm["w1"], prm["w3"], prm["w2"], seq_len)
    return xo, nk, nv, cfin, mfin, hfin


def _pack_w_in(w_in):
    o = np.cumsum((0, D_M, D_M, D_M, D_M, 2 * H_M, 2 * H_M, D_S, D_XBC, 2 * H_S, D_NA, D_NA, D_NA))
    pad = jnp.zeros(w_in.shape[:-1] + (GATE_W - 4 * H_M - 2 * H_S,), w_in.dtype)
    parts = [w_in[..., o[0]:o[4]], w_in[..., o[6]:o[8]], w_in[..., o[9]:o[12]],
             w_in[..., o[4]:o[6]], w_in[..., o[8]:o[9]], pad]
    return jnp.concatenate(parts, axis=-1).astype(BF16)


def kernel(x_prompt, x_sample, cache_na_k, cache_na_v, state_mlstm_c, state_mlstm_n, state_mlstm_m, state_ssm, c, c_ctx, ada_w, ada_b, norm1_g, norm2_g, w_in, mlstm_f_bias, mlstm_norm_g, conv_w, conv_b, ssm_dt_bias, ssm_a_log, ssm_d, ssm_norm_g, na_rpb, w_out, router_w, exp_w1, exp_w3, exp_w2, final_g):
    bp, sp, _ = x_prompt.shape
    bs, ss, _ = x_sample.shape
    past = cache_na_k.shape[2]

    cvec = jnp.zeros((8, D), F32).at[0].set(c_ctx).at[1:1 + bs].set(c)
    mod = _modulation(cvec, ada_w, ada_b)

    w_in_p = _pack_w_in(w_in)
    w_out_bf = w_out.astype(BF16)
    w1_bf, w3_bf, w2_bf = exp_w1.astype(BF16), exp_w3.astype(BF16), exp_w2.astype(BF16)
    router_wt = jnp.swapaxes(router_w, 1, 2)

    def aug(cs, ns):
        b = cs.shape[0]
        pad = jnp.zeros(cs.shape[:-1] + (128 - HD - 1,), F32)
        return jnp.concatenate([cs, ns[..., None], pad], axis=-1).reshape(b, 2 * H_M, HD, 128)

    def rep_m(ms):
        b = ms.shape[0]
        return jnp.broadcast_to(ms.reshape(b, 2 * H_M, 1, 1), (b, 2 * H_M, 8, 128)).astype(F32)

    ctx_caug0 = jnp.zeros((bp, 2 * H_M, HD, 128), F32)
    ctx_m0 = jnp.full((bp, 2 * H_M, 8, 128), NEG_INIT, F32)
    ctx_h0 = jnp.zeros((bp, 2 * H_S, N_S, HD), F32)

    xp = x_prompt.reshape(bp * sp, D)
    xs = x_sample.reshape(bs * ss, D)
    ks, vs, cs, ns, ms, hs = [], [], [], [], [], []
    for l in range(DEPTH):
        prm = dict(norm1_g=norm1_g[l], norm2_g=norm2_g[l], w_in=w_in_p[l], f_bias=mlstm_f_bias[l],
                   mnorm_g=mlstm_norm_g[l], conv_w=conv_w[l], conv_b=conv_b[l], dt_bias=ssm_dt_bias[l],
                   a_log=ssm_a_log[l], d_skip=ssm_d[l], snorm_g=ssm_norm_g[l], w_out=w_out_bf[l],
                   router_wt=router_wt[l], w1=w1_bf[l], w3=w3_bf[l], w2=w2_bf[l])
        xp, nk, nv, cfin, mfin, hfin = _layer(xp, mod[l, 0:1], prm, bp, sp, ctx_caug0, ctx_m0, ctx_h0)
        ks.append(nk.reshape(bp, sp, H_NA, HD))
        vs.append(nv.reshape(bp, sp, H_NA, HD))
        cs.append(cfin[..., 0:HD].reshape(bp, 2, H_M, HD, HD))
        ns.append(cfin[..., HD].reshape(bp, 2, H_M, HD))
        ms.append(mfin[..., 0, 0].reshape(bp, 2, H_M))
        hs.append(jnp.swapaxes(hfin, -1, -2).reshape(bp, 2, H_S, HD, N_S))

        lat_caug0 = aug(state_mlstm_c[:, l], state_mlstm_n[:, l])
        lat_m0 = rep_m(state_mlstm_m[:, l])
        lat_h0 = jnp.swapaxes(state_ssm[:, l], -1, -2).reshape(bs, 2 * H_S, N_S, HD)
        ck = cache_na_k[:, l].reshape(bs * past, D_NA)
        cv = cache_na_v[:, l].reshape(bs * past, D_NA)
        bias = _natten_bias(na_rpb[l], ss // GRID_W)
        xs, *_ = _layer(xs, mod[l, 1:1 + bs], prm, bs, ss, lat_caug0, lat_m0, lat_h0, ck, cv, bias)

    y_prompt = _final_norm(xp, final_g).reshape(bp, sp, D)
    y_sample = _final_norm(xs, final_g).reshape(bs, ss, D)
    return (y_prompt, y_sample, jnp.stack(ks, axis=1), jnp.stack(vs, axis=1), jnp.stack(cs, axis=1),
            jnp.stack(ns, axis=1), jnp.stack(ms, axis=1), jnp.stack(hs, axis=1))
```

```python
import functools

import numpy as np
import jax
import jax.numpy as jnp
from jax import lax
from jax.experimental import pallas as pl
from jax.experimental.pallas import tpu as pltpu

F32 = jnp.float32
BF16 = jnp.bfloat16
HI = lax.Precision.HIGHEST

D = 1024
DEPTH = 4
HD = 64
H_M = 4
D_M = H_M * HD
H_S = 6
D_S = H_S * HD
G_S = 2
R_S = H_S // G_S
N_S = 64
D_XBC = D_S + 2 * G_S * N_S
H_NA = 6
D_NA = H_NA * HD
GRID_W = 64
WIN_H = 8
WIN_W = 16
N_EXPERTS = 16
EC_FACTOR = 2
EPS = 1e-6
NEG_INIT = -1e30
TOK_BLK = 128
SLOT_BLK = 128
GATE_W = 128
MI0, MF0, DT0 = 0, 2 * H_M, 4 * H_M
W_PACKED = 3 * D_M + D_M + D_S + D_XBC + 3 * D_NA + GATE_W

VMEM_LIMIT = 56 * 1024 * 1024


def _cparams(sem):
    return pltpu.CompilerParams(dimension_semantics=sem, vmem_limit_bytes=VMEM_LIMIT)


def _sigmoid(x):
    return 1.0 / (1.0 + jnp.exp(-x))


def _softplus(x):
    return jnp.maximum(x, 0.0) + jnp.log1p(jnp.exp(-jnp.abs(x)))


def _dot(a, b):
    return jnp.dot(a, b, preferred_element_type=F32)


def _dot_nt(a, b):
    return lax.dot_general(a, b, (((1,), (1,)), ((), ())), preferred_element_type=F32)


def _dot_tn(a, b):
    return lax.dot_general(a, b, (((0,), (0,)), ((), ())), preferred_element_type=F32)


def _mod_kernel(c_ref, w_ref, b_ref, o_ref):
    cv = c_ref[...]
    s = cv * _sigmoid(cv)
    o_ref[...] = jnp.dot(s, w_ref[...], precision=HI, preferred_element_type=F32) + b_ref[...]


def _modulation(cvec, ada_w, ada_b):
    tn = 1536
    return pl.pallas_call(
        _mod_kernel,
        grid=(DEPTH, 6 * D // tn),
        in_specs=[pl.BlockSpec((8, D), lambda l, j: (0, 0)),
                  pl.BlockSpec((None, D, tn), lambda l, j: (l, 0, j)),
                  pl.BlockSpec((None, 1, tn), lambda l, j: (l, 0, j))],
        out_specs=pl.BlockSpec((None, 8, tn), lambda l, j: (l, 0, j)),
        out_shape=jax.ShapeDtypeStruct((DEPTH, 8, 6 * D), F32),
        compiler_params=_cparams(("arbitrary", "arbitrary")),
    )(cvec, ada_w, ada_b.reshape(DEPTH, 1, 6 * D))


_IN_SEGS = (("qkv", 0, 768), ("mo", 768, 1024), ("sz", 1024, 1408), ("xbc", 1408, 2048),
            ("nq", 2048, 2432), ("nk", 2432, 2816), ("nv", 2816, 3200), ("gate", 3200, 3328))


def _inproj_kernel(x_ref, g_ref, sh_ref, sc_ref, w_ref, *outs):
    x = x_ref[...]
    ms = jnp.mean(x * x, axis=-1, keepdims=True)
    h = x * lax.rsqrt(ms + EPS) * g_ref[...]
    h = h * (1.0 + sc_ref[...]) + sh_ref[...]
    hb = h.astype(BF16)
    for o_ref, (_, lo, hi) in zip(outs, _IN_SEGS):
        o_ref[...] = _dot(hb, w_ref[:, lo:hi])


def _in_projection(x, norm_g, shift, scale, w_packed, seq_len):
    n = x.shape[0]
    tm = 512
    groups = shift.shape[0]
    if groups == 1:
        gidx = lambda i: (0, 0, 0)
    else:
        gidx = lambda i: ((i * tm) // seq_len, 0, 0)
    out_shape = tuple(jax.ShapeDtypeStruct((n, hi - lo), F32) for _, lo, hi in _IN_SEGS)
    out_specs = tuple(pl.BlockSpec((tm, hi - lo), lambda i: (i, 0)) for _, lo, hi in _IN_SEGS)
    return pl.pallas_call(
        _inproj_kernel,
        grid=(n // tm,),
        in_specs=[pl.BlockSpec((tm, D), lambda i: (i, 0)),
                  pl.BlockSpec((1, D), lambda i: (0, 0)),
                  pl.BlockSpec((None, 1, D), gidx),
                  pl.BlockSpec((None, 1, D), gidx),
                  pl.BlockSpec((D, W_PACKED), lambda i: (0, 0))],
        out_specs=out_specs,
        out_shape=out_shape,
        compiler_params=_cparams(("arbitrary",)),
    )(x, norm_g.reshape(1, D), shift, scale, w_packed)


def _conv_kernel(cur_ref, prev_ref, next_ref, w_ref, b_ref, o_ref, *, tiles_per_seq):
    i = pl.program_id(0)
    tt = cur_ref.shape[0]
    cur = cur_ref[...]
    first = (i % tiles_per_seq) == 0
    last = (i % tiles_per_seq) == tiles_per_seq - 1
    prow = jnp.where(first, 0.0, prev_ref[7:8, :])
    nrow = jnp.where(last, 0.0, next_ref[0:1, :])
    rows = lax.broadcasted_iota(jnp.int32, cur.shape, 0)
    prev = jnp.where(rows == 0, prow, pltpu.roll(cur, 1, 0))
    nxt = jnp.where(rows == tt - 1, nrow, pltpu.roll(cur, tt - 1, 0))
    y = prev * w_ref[0:1, :] + cur * w_ref[1:2, :] + nxt * w_ref[2:3, :] + b_ref[...]
    o_ref[...] = y * _sigmoid(y)


def _conv_silu(sxbc, conv_w, conv_b, seq_len):
    n = sxbc.shape[0]
    tt = min(seq_len, 512)
    tps = seq_len // tt
    nb8 = n // 8
    return pl.pallas_call(
        functools.partial(_conv_kernel, tiles_per_seq=tps),
        grid=(n // tt,),
        in_specs=[pl.BlockSpec((tt, D_XBC), lambda i: (i, 0)),
                  pl.BlockSpec((8, D_XBC), lambda i: (jnp.maximum(i * (tt // 8) - 1, 0), 0)),
                  pl.BlockSpec((8, D_XBC), lambda i: (jnp.minimum((i + 1) * (tt // 8), nb8 - 1), 0)),
                  pl.BlockSpec((3, D_XBC), lambda i: (0, 0)),
                  pl.BlockSpec((1, D_XBC), lambda i: (0, 0))],
        out_specs=pl.BlockSpec((tt, D_XBC), lambda i: (i, 0)),
        out_shape=jax.ShapeDtypeStruct((n, D_XBC), F32),
        compiler_params=_cparams(("arbitrary",)),
    )(sxbc, sxbc, sxbc, conv_w, conv_b.reshape(1, D_XBC))


def _tri_masks(ch):
    r = lax.broadcasted_iota(jnp.int32, (ch, ch), 0)
    c = lax.broadcasted_iota(jnp.int32, (ch, ch), 1)
    return r >= c, r <= c


def _mlstm_kernel(qkv_f, qkv_b, gt_f, gt_b, fb_ref, c0_ref, m0_ref,
                  hf_ref, hb_ref, cfin_ref, mfin_ref, c_scr, m_scr, *, n_tiles, ch):
    j = pl.program_id(1)

    @pl.when(j == 0)
    def _():
        c_scr[...] = c0_ref[...]
        m_scr[...] = m0_ref[...]

    tt = qkv_f.shape[0]
    nch = tt // ch
    fb = fb_ref[...]
    low, upp = _tri_masks(ch)
    ones_lane = (lax.broadcasted_iota(jnp.int32, (ch, HD), 1) == 0).astype(F32)

    def chunk(ci, carry):
        for d in range(2):
            qkv_ref = (qkv_f, qkv_b)[d]
            g_ref = (gt_f, gt_b)[d]
            out_ref = (hf_ref, hb_ref)[d]
            mask = (low, upp)[d]
            c0 = pl.multiple_of((ci if d == 0 else nch - 1 - ci) * ch, ch)
            g = g_ref[pl.ds(c0, ch), :]
            z = g + fb
            logf = jnp.minimum(z, 0.0) - jnp.log1p(jnp.exp(-jnp.abs(z)))
            bt = jnp.dot(mask.astype(F32), logf, precision=HI, preferred_element_type=F32)
            bt_t = bt.T
            g_t = g.T
            edge = ch - 1 if d == 0 else 0
            for h in range(H_M):
                idx = d * H_M + h
                q = qkv_ref[pl.ds(c0, ch), h * HD:(h + 1) * HD].astype(BF16)
                k = (qkv_ref[pl.ds(c0, ch), D_M + h * HD:D_M + (h + 1) * HD] * (HD ** -0.5))
                v = qkv_ref[pl.ds(c0, ch), 2 * D_M + h * HD:2 * D_M + (h + 1) * HD]
                vaug = jnp.concatenate([v, ones_lane], axis=-1).astype(BF16)
                b_col = bt[:, MF0 + idx:MF0 + idx + 1]
                b_row = bt_t[MF0 + idx:MF0 + idx + 1, :]
                i_row = g_t[MI0 + idx:MI0 + idx + 1, :]
                i_col = g[:, MI0 + idx:MI0 + idx + 1]
                m_prev = m_scr[idx][0:1, 0:1]
                caug = c_scr[idx]
                logd = jnp.where(mask, b_col - b_row + i_row, -jnp.inf)
                m_inter = b_col + m_prev
                m_t = jnp.maximum(jnp.max(logd, axis=-1, keepdims=True), m_inter)
                dmat = jnp.exp(logd - m_t)
                w_inter = jnp.exp(m_inter - m_t)
                s = (_dot_nt(q, k.astype(BF16)) * dmat).astype(BF16)
                tot = _dot(s, vaug) + w_inter * _dot(q, caug.astype(BF16))
                den = jnp.maximum(jnp.abs(tot[:, HD:HD + 1]), jnp.exp(-m_t))
                out_ref[pl.ds(c0, ch), h * HD:(h + 1) * HD] = tot[:, 0:HD] / den
                b_l = bt[edge:edge + 1, MF0 + idx:MF0 + idx + 1]
                lw = b_l - b_col + i_col
                m_new = jnp.maximum(b_l + m_prev, jnp.max(lw, axis=0, keepdims=True))
                wk = jnp.exp(lw - m_new)
                decay = jnp.exp(b_l + m_prev - m_new)
                c_scr[idx] = decay * caug + _dot_tn((k * wk).astype(BF16), vaug)
                m_scr[idx] = jnp.broadcast_to(m_new, (8, 128))
        return carry

    lax.fori_loop(0, nch, chunk, 0)

    @pl.when(j == n_tiles - 1)
    def _():
        cfin_ref[...] = c_scr[...]
        mfin_ref[...] = m_scr[...]


def _mlstm_scan(qkv, gate, f_bias, caug0, m0, batch, seq_len):
    n = qkv.shape[0]
    tt = min(seq_len, 512)
    nt = seq_len // tt
    fb = jnp.zeros((1, GATE_W), F32).at[0, MF0:MF0 + 2 * H_M].set(f_bias.reshape(-1))
    fwd = lambda b, j: (b * nt + j, 0)
    bwd = lambda b, j: (b * nt + nt - 1 - j, 0)
    st4 = lambda b, j: (b, 0, 0, 0)
    return pl.pallas_call(
        functools.partial(_mlstm_kernel, n_tiles=nt, ch=128),
        grid=(batch, nt),
        in_specs=[pl.BlockSpec((tt, 3 * D_M), fwd), pl.BlockSpec((tt, 3 * D_M), bwd),
                  pl.BlockSpec((tt, GATE_W), fwd), pl.BlockSpec((tt, GATE_W), bwd),
                  pl.BlockSpec((1, GATE_W), lambda b, j: (0, 0)),
                  pl.BlockSpec((None, 2 * H_M, HD, 128), st4),
                  pl.BlockSpec((None, 2 * H_M, 8, 128), st4)],
        out_specs=(pl.BlockSpec((tt, D_M), fwd), pl.BlockSpec((tt, D_M), bwd),
                   pl.BlockSpec((None, 2 * H_M, HD, 128), st4),
                   pl.BlockSpec((None, 2 * H_M, 8, 128), st4)),
        out_shape=(jax.ShapeDtypeStruct((n, D_M), F32), jax.ShapeDtypeStruct((n, D_M), F32),
                   jax.ShapeDtypeStruct((batch, 2 * H_M, HD, 128), F32),
                   jax.ShapeDtypeStruct((batch, 2 * H_M, 8, 128), F32)),
        scratch_shapes=[pltpu.VMEM((2 * H_M, HD, 128), F32), pltpu.VMEM((2 * H_M, 8, 128), F32)],
        compiler_params=_cparams(("arbitrary", "arbitrary")),
    )(qkv, qkv, gate, gate, fb, caug0, m0)


def _ssd_kernel(xbc_f, xbc_b, gt_f, gt_b, dtb_ref, alog_ref, h0_ref,
                yf_ref, yb_ref, hfin_ref, h_scr, *, n_tiles, ch):
    j = pl.program_id(1)

    @pl.when(j == 0)
    def _():
        h_scr[...] = h0_ref[...]

    tt = xbc_f.shape[0]
    nch = tt // ch
    dtb = dtb_ref[...]
    a_neg = -jnp.exp(alog_ref[...])
    low, upp = _tri_masks(ch)

    def chunk(ci, carry):
        for d in range(2):
            x_ref = (xbc_f, xbc_b)[d]
            g_ref = (gt_f, gt_b)[d]
            out_ref = (yf_ref, yb_ref)[d]
            mask = (low, upp)[d]
            c0 = pl.multiple_of((ci if d == 0 else nch - 1 - ci) * ch, ch)
            dt = _softplus(g_ref[pl.ds(c0, ch), :] + dtb)
            a = jnp.dot(mask.astype(F32), dt * a_neg, precision=HI, preferred_element_type=F32)
            a_t = a.T
            dt_t = dt.T
            edge = ch - 1 if d == 0 else 0
            for grp in range(G_S):
                bm = x_ref[pl.ds(c0, ch), D_S + grp * N_S:D_S + (grp + 1) * N_S]
                cm = x_ref[pl.ds(c0, ch), D_S + G_S * N_S + grp * N_S:D_S + G_S * N_S + (grp + 1) * N_S].astype(BF16)
                cb = _dot_nt(cm, bm.astype(BF16))
                for r in range(R_S):
                    hh = grp * R_S + r
                    idx = d * H_S + hh
                    ln = DT0 + idx
                    a_col = a[:, ln:ln + 1]
                    a_row = a_t[ln:ln + 1, :]
                    seg = jnp.exp(jnp.where(mask, a_col - a_row, -jnp.inf))
                    mm = (cb * seg * dt_t[ln:ln + 1, :]).astype(BF16)
                    xh = x_ref[pl.ds(c0, ch), hh * HD:(hh + 1) * HD].astype(BF16)
                    h_st = h_scr[idx]
                    y = _dot(mm, xh) + _dot_nt(cm, h_st.astype(BF16)) * jnp.exp(a_col)
                    out_ref[pl.ds(c0, ch), hh * HD:(hh + 1) * HD] = y
                    a_l = a[edge:edge + 1, ln:ln + 1]
                    wk = jnp.exp(a_l - a_col) * dt[:, ln:ln + 1]
                    h_scr[idx] = jnp.exp(a_l) * h_st + _dot_tn(xh, (bm * wk).astype(BF16))
        return carry

    lax.fori_loop(0, nch, chunk, 0)

    @pl.when(j == n_tiles - 1)
    def _():
        hfin_ref[...] = h_scr[...]


def _ssd_scan(xbc, gate, dt_bias, a_log, h0_t, batch, seq_len):
    n = xbc.shape[0]
    tt = min(seq_len, 512)
    nt = seq_len // tt
    dtb = jnp.zeros((1, GATE_W), F32).at[0, DT0:DT0 + 2 * H_S].set(dt_bias.reshape(-1))
    alog = jnp.zeros((1, GATE_W), F32).at[0, DT0:DT0 + 2 * H_S].set(a_log.reshape(-1))
    fwd = lambda b, j: (b * nt + j, 0)
    bwd = lambda b, j: (b * nt + nt - 1 - j, 0)
    st4 = lambda b, j: (b, 0, 0, 0)
    return pl.pallas_call(
        functools.partial(_ssd_kernel, n_tiles=nt, ch=128),
        grid=(batch, nt),
        in_specs=[pl.BlockSpec((tt, D_XBC), fwd), pl.BlockSpec((tt, D_XBC), bwd),
                  pl.BlockSpec((tt, GATE_W), fwd), pl.BlockSpec((tt, GATE_W), bwd),
                  pl.BlockSpec((1, GATE_W), lambda b, j: (0, 0)),
                  pl.BlockSpec((1, GATE_W), lambda b, j: (0, 0)),
                  pl.BlockSpec((None, 2 * H_S, N_S, HD), st4)],
        out_specs=(pl.BlockSpec((tt, D_S), fwd), pl.BlockSpec((tt, D_S), bwd),
                   pl.BlockSpec((None, 2 * H_S, N_S, HD), st4)),
        out_shape=(jax.ShapeDtypeStruct((n, D_S), F32), jax.ShapeDtypeStruct((n, D_S), F32),
                   jax.ShapeDtypeStruct((batch, 2 * H_S, N_S, HD), F32)),
        scratch_shapes=[pltpu.VMEM((2 * H_S, N_S, HD), F32)],
        compiler_params=_cparams(("arbitrary", "arbitrary")),
    )(xbc, xbc, gate, gate, dtb, alog, h0_t)


def _ctx_attn_kernel(q_ref, k_ref, v_ref, o_ref):
    for h in range(H_NA):
        sl = slice(h * HD, (h + 1) * HD)
        q = (q_ref[:, sl] * (HD ** -0.5)).astype(BF16)
        s = _dot_nt(q, k_ref[:, sl].astype(BF16))
        m = jnp.max(s, axis=-1, keepdims=True)
        p = jnp.exp(s - m)
        l = jnp.sum(p, axis=-1, keepdims=True)
        o_ref[:, sl] = _dot(p.astype(BF16), v_ref[:, sl].astype(BF16)) / l


def _ctx_attention(q, k, v, batch, seq_len):
    spec = pl.BlockSpec((seq_len, D_NA), lambda b: (b, 0))
    return pl.pallas_call(
        _ctx_attn_kernel,
        grid=(batch,),
        in_specs=[spec, spec, spec],
        out_specs=spec,
        out_shape=jax.ShapeDtypeStruct(q.shape, F32),
        compiler_params=_cparams(("arbitrary",)),
    )(q, k, v)


def _row_start(r, rows):
    return jnp.clip(r - WIN_H // 2, 0, rows - WIN_H)


def _natten_kernel(q_ref, k_ref, v_ref, ck_ref, cv_ref, bias_ref, o_ref, *, rows):
    r = pl.program_id(1)
    k0 = pl.multiple_of(_row_start(r, rows) * GRID_W, GRID_W)
    nloc = WIN_H * GRID_W
    for h in range(H_NA):
        sl = slice(h * HD, (h + 1) * HD)
        q = (q_ref[:, sl] * (HD ** -0.5)).astype(BF16)
        kw = k_ref[pl.ds(k0, nloc), sl].astype(BF16)
        vw = v_ref[pl.ds(k0, nloc), sl].astype(BF16)
        s_loc = _dot_nt(q, kw) + bias_ref[h]
        s_ctx = _dot_nt(q, ck_ref[:, sl].astype(BF16))
        m = jnp.maximum(jnp.max(s_loc, axis=-1, keepdims=True), jnp.max(s_ctx, axis=-1, keepdims=True))
        p_loc = jnp.exp(s_loc - m)
        p_ctx = jnp.exp(s_ctx - m)
        l = jnp.sum(p_loc, axis=-1, keepdims=True) + jnp.sum(p_ctx, axis=-1, keepdims=True)
        o = _dot(p_loc.astype(BF16), vw) + _dot(p_ctx.astype(BF16), cv_ref[:, sl].astype(BF16))
        o_ref[:, sl] = o / l


def _natten_bias(rpb, rows):
    qc = np.arange(GRID_W)[:, None]
    kc = np.arange(GRID_W)[None, :]
    cstart = np.clip(qc - WIN_W // 2, 0, GRID_W - WIN_W)
    ok = (kc >= cstart) & (kc < cstart + WIN_W)
    col_off = np.clip(kc - qc + WIN_W - 1, 0, 2 * WIN_W - 2)
    t = jnp.where(jnp.asarray(ok)[None, None], rpb[:, :, col_off], -jnp.inf)
    var = np.arange(WIN_H)[:, None] + np.arange(WIN_H)[None, :]
    b = t[:, var]
    return b.transpose(0, 1, 3, 2, 4).reshape(H_NA, WIN_H, GRID_W, WIN_H * GRID_W).astype(F32)


def _neighbourhood_attention(q, k, v, ck, cv, bias, batch, seq_len):
    rows = seq_len // GRID_W
    past = ck.shape[0] // batch
    bvar = lambda b, r: (0, _row_start(r, rows) - r + WIN_H - 1, 0, 0)
    return pl.pallas_call(
        functools.partial(_natten_kernel, rows=rows),
        grid=(batch, rows),
        in_specs=[pl.BlockSpec((GRID_W, D_NA), lambda b, r: (b * rows + r, 0)),
                  pl.BlockSpec((seq_len, D_NA), lambda b, r: (b, 0)),
                  pl.BlockSpec((seq_len, D_NA), lambda b, r: (b, 0)),
                  pl.BlockSpec((past, D_NA), lambda b, r: (b, 0)),
                  pl.BlockSpec((past, D_NA), lambda b, r: (b, 0)),
                  pl.BlockSpec((H_NA, None, GRID_W, WIN_H * GRID_W), bvar)],
        out_specs=pl.BlockSpec((GRID_W, D_NA), lambda b, r: (b * rows + r, 0)),
        out_shape=jax.ShapeDtypeStruct(q.shape, F32),
        compiler_params=_cparams(("arbitrary", "arbitrary")),
    )(q, k, v, ck, cv, bias)


def _outproj_kernel(x_ref, hf_ref, hb_ref, mo_ref, yf_ref, yb_ref, xbc_ref, sz_ref, a_ref,
                    mg_ref, dsk_ref, sg_ref, bd_ref, w_ref, g1_ref, n2_ref, sh2_ref, sc2_ref, rw_ref,
                    xo_ref, h2_ref, aff_ref):
    hm = hf_ref[...] + hb_ref[...]
    ssq = jnp.dot(hm * hm, bd_ref[...], precision=HI, preferred_element_type=F32)
    m_out = hm * lax.rsqrt(ssq * (1.0 / HD) + EPS) * mg_ref[...] * _sigmoid(mo_ref[...])
    sz = sz_ref[...]
    ys = (yf_ref[...] + yb_ref[...] + dsk_ref[...] * xbc_ref[:, 0:D_S]) * (sz * _sigmoid(sz))
    s_out = ys * lax.rsqrt(jnp.mean(ys * ys, axis=-1, keepdims=True) + EPS) * sg_ref[...]
    y = (_dot(m_out.astype(BF16), w_ref[0:D_M, :])
         + _dot(s_out.astype(BF16), w_ref[D_M:D_M + D_S, :])
         + _dot(a_ref[...].astype(BF16), w_ref[D_M + D_S:D, :]))
    xn = x_ref[...] + g1_ref[...] * y
    xo_ref[...] = xn
    h2 = xn * lax.rsqrt(jnp.mean(xn * xn, axis=-1, keepdims=True) + EPS) * n2_ref[...]
    h2 = h2 * (1.0 + sc2_ref[...]) + sh2_ref[...]
    h2_ref[...] = h2.astype(BF16)
    logits = lax.dot_general(rw_ref[...], h2, (((1,), (1,)), ((), ())), precision=HI,
                             preferred_element_type=F32)
    mx = jnp.max(logits, axis=0, keepdims=True)
    ex = jnp.exp(logits - mx)
    aff_ref[...] = ex / jnp.sum(ex, axis=0, keepdims=True)


def _out_projection(x, hf, hb, mo, yf, yb, xbc, sz, a_out, mnorm_g, d_skip, snorm_g, w_out_bf,
                    g1, norm2_g, sh2, sc2, router_wt, seq_len):
    n = x.shape[0]
    tm = 512
    groups = g1.shape[0]
    if groups == 1:
        gidx = lambda i: (0, 0, 0)
    else:
        gidx = lambda i: ((i * tm) // seq_len, 0, 0)
    row = lambda w: pl.BlockSpec((tm, w), lambda i: (i, 0))
    const = lambda s: pl.BlockSpec(s, lambda i: (0,) * len(s))
    hid = np.arange(D_M) // HD
    blockdiag = jnp.asarray((hid[:, None] == hid[None, :]).astype(np.float32))
    return pl.pallas_call(
        _outproj_kernel,
        grid=(n // tm,),
        in_specs=[row(D), row(D_M), row(D_M), row(D_M), row(D_S), row(D_S), row(D_XBC), row(D_S), row(D_NA),
                  const((1, D_M)), const((1, D_S)), const((1, D_S)), const((D_M, D_M)), const((D, D)),
                  pl.BlockSpec((None, 1, D), gidx), const((1, D)),
                  pl.BlockSpec((None, 1, D), gidx), pl.BlockSpec((None, 1, D), gidx),
                  const((N_EXPERTS, D))],
        out_specs=(row(D), row(D), pl.BlockSpec((N_EXPERTS, tm), lambda i: (0, i))),
        out_shape=(jax.ShapeDtypeStruct((n, D), F32), jax.ShapeDtypeStruct((n, D), BF16),
                   jax.ShapeDtypeStruct((N_EXPERTS, n), F32)),
        compiler_params=_cparams(("arbitrary",)),
    )(x, hf, hb, mo, yf, yb, xbc, sz, a_out,
      mnorm_g.reshape(1, D_M), jnp.repeat(d_skip, HD).reshape(1, D_S), snorm_g.reshape(1, D_S), blockdiag,
      w_out_bf, g1, norm2_g.reshape(1, D), sh2, sc2, router_wt)


def _select_kernel(aff_ref, gate_ref, slot_ref, start_ref, *, cap):
    aff = aff_ref[...]
    bits = pltpu.bitcast(aff, jnp.int32)
    n_tok = aff.shape[1]

    def step(i, prefix):
        cand = prefix | (jnp.int32(1) << (30 - i))
        cnt = jnp.sum((bits >= cand).astype(F32), axis=1, keepdims=True)
        return jnp.where(cnt >= cap, cand, prefix)

    thr = lax.fori_loop(0, 31, step, jnp.zeros((aff.shape[0], 1), jnp.int32))
    gt = bits > thr
    eq = bits == thr
    need = cap - jnp.sum(gt.astype(F32), axis=1, keepdims=True)
    r_i = lax.broadcasted_iota(jnp.int32, (128, 128), 0)
    c_i = lax.broadcasted_iota(jnp.int32, (128, 128), 1)
    strict = (r_i < c_i).astype(BF16)
    run_eq = jnp.zeros((aff.shape[0], 1), F32)
    run_sel = jnp.zeros((aff.shape[0], 1), F32)
    lane = lax.broadcasted_iota(jnp.int32, (aff.shape[0], 128), 1)
    starts = jnp.zeros((aff.shape[0], 128), F32)
    for blk in range(n_tok // TOK_BLK):
        sl = slice(blk * TOK_BLK, (blk + 1) * TOK_BLK)
        e = eq[:, sl]
        rank = _dot(e.astype(BF16), strict) + run_eq
        keep = gt[:, sl] | (e & (rank < need))
        kf = jnp.where(keep, 1.0, 0.0)
        slot = _dot(kf.astype(BF16), strict) + run_sel
        gate_ref[:, sl] = jnp.where(keep, aff[:, sl], 0.0)
        slot_ref[:, sl] = jnp.where(keep, slot, -1.0).astype(jnp.int32)
        starts = jnp.where(lane == blk, run_sel, starts)
        run_eq = run_eq + jnp.sum(e.astype(F32), axis=1, keepdims=True)
        run_sel = run_sel + jnp.sum(kf, axis=1, keepdims=True)
    start_ref[...] = starts.astype(jnp.int32)


def _select(aff_t):
    n = aff_t.shape[1]
    cap = EC_FACTOR * n // N_EXPERTS
    assert n // TOK_BLK <= 128
    return pl.pallas_call(
        functools.partial(_select_kernel, cap=float(cap)),
        out_shape=(jax.ShapeDtypeStruct(aff_t.shape, F32), jax.ShapeDtypeStruct(aff_t.shape, jnp.int32),
                   jax.ShapeDtypeStruct((N_EXPERTS, 128), jnp.int32)),
        compiler_params=pltpu.CompilerParams(vmem_limit_bytes=VMEM_LIMIT),
    )(aff_t)


def _expert_kernel(lo_ref, hi_ref, slot_ref, h2_ref, w1_ref, w3_ref, w2_ref, ye_ref, xe_scr, acc_scr, *, cap):
    e = pl.program_id(0)
    nsb = cap // SLOT_BLK
    slot_iota = lax.broadcasted_iota(jnp.int32, (SLOT_BLK, TOK_BLK), 0)
    for sb in range(nsb):
        acc_scr[...] = jnp.zeros_like(acc_scr)
        want = slot_iota + sb * SLOT_BLK

        def body(tb, carry):
            t0 = pl.multiple_of(tb * TOK_BLK, TOK_BLK)
            onehot = jnp.where(slot_ref[:, pl.ds(t0, TOK_BLK)] == want, 1.0, 0.0).astype(BF16)
            acc_scr[...] += _dot(onehot, h2_ref[pl.ds(t0, TOK_BLK), :])
            return carry

        lax.fori_loop(lo_ref[e * nsb + sb], hi_ref[e * nsb + sb] + 1, body, 0)
        xe_scr[sb * SLOT_BLK:(sb + 1) * SLOT_BLK, :] = acc_scr[...].astype(BF16)
    half = cap // 2
    for r in range(2):
        x = xe_scr[r * half:(r + 1) * half, :]
        a = _dot(x, w1_ref[...])
        b = _dot(x, w3_ref[...])
        hid = (a * _sigmoid(a) * b).astype(BF16)
        ye_ref[r * half:(r + 1) * half, :] = _dot(hid, w2_ref[...]).astype(BF16)


def _combine_kernel(win_ref, slot_ref, gate_ref, x_ref, g2_ref, ye_ref, o_ref, acc_scr):
    tb = pl.program_id(0)
    acc_scr[...] = jnp.zeros_like(acc_scr)
    lane = lax.broadcasted_iota(jnp.int32, (TOK_BLK, 2 * SLOT_BLK), 1)
    fill = jnp.zeros((TOK_BLK - N_EXPERTS, TOK_BLK), F32)
    slot_t = jnp.concatenate([slot_ref[...].astype(F32), fill], axis=0).T
    gate_t = jnp.concatenate([gate_ref[...], fill], axis=0).T
    for e in range(N_EXPERTS):
        s0 = pl.multiple_of(win_ref[tb * N_EXPERTS + e] * SLOT_BLK, SLOT_BLK)
        want = (lane + s0).astype(F32)
        onehot = jnp.where(slot_t[:, e:e + 1] == want, 1.0, 0.0).astype(BF16)
        acc_scr[...] += _dot(onehot, ye_ref[e, pl.ds(s0, 2 * SLOT_BLK), :]) * gate_t[:, e:e + 1]
    o_ref[...] = x_ref[...] + g2_ref[...] * acc_scr[...]


def _moe(h2, aff_t, x, g2, w1, w3, w2, layer, seq_len):
    n = x.shape[0]
    cap = EC_FACTOR * n // N_EXPERTS
    nsb = cap // SLOT_BLK
    ntb = n // TOK_BLK
    gate, slot, starts = _select(aff_t)

    cs = starts[:, :ntb]
    cend = jnp.concatenate([cs[:, 1:], jnp.full((N_EXPERTS, 1), cap, jnp.int32)], axis=1)
    edges = jnp.arange(nsb, dtype=jnp.int32) * SLOT_BLK
    lo = jnp.sum(cend[:, None, :] <= edges[None, :, None], axis=-1).astype(jnp.int32)
    hi = jnp.sum(cs[:, None, :] < (edges + SLOT_BLK)[None, :, None], axis=-1).astype(jnp.int32) - 1
    win = jnp.minimum(cs // SLOT_BLK, nsb - 2).T.astype(jnp.int32)

    wspec = pl.BlockSpec((None, None, D, D), lambda e, *_: (layer, e, 0, 0))
    ye = pl.pallas_call(
        functools.partial(_expert_kernel, cap=cap),
        grid_spec=pltpu.PrefetchScalarGridSpec(
            num_scalar_prefetch=2,
            grid=(N_EXPERTS,),
            in_specs=[pl.BlockSpec((None, 1, n), lambda e, *_: (e, 0, 0)),
                      pl.BlockSpec((n, D), lambda e, *_: (0, 0), pipeline_mode=pl.Buffered(1)),
                      wspec, wspec, wspec],
            out_specs=pl.BlockSpec((None, cap, D), lambda e, *_: (e, 0, 0)),
            scratch_shapes=[pltpu.VMEM((cap, D), BF16), pltpu.VMEM((SLOT_BLK, D), F32)]),
        out_shape=jax.ShapeDtypeStruct((N_EXPERTS, cap, D), BF16),
        compiler_params=_cparams(("arbitrary",)),
    )(lo.reshape(-1), hi.reshape(-1), slot.reshape(N_EXPERTS, 1, n), h2, w1, w3, w2)

    groups = g2.shape[0]
    if groups == 1:
        gidx = lambda i, *_: (0, 0, 0)
    else:
        gidx = lambda i, *_: ((i * TOK_BLK) // seq_len, 0, 0)
    return pl.pallas_call(
        _combine_kernel,
        grid_spec=pltpu.PrefetchScalarGridSpec(
            num_scalar_prefetch=1,
            grid=(ntb,),
            in_specs=[pl.BlockSpec((N_EXPERTS, TOK_BLK), lambda i, *_: (0, i)),
                      pl.BlockSpec((N_EXPERTS, TOK_BLK), lambda i, *_: (0, i)),
                      pl.BlockSpec((TOK_BLK, D), lambda i, *_: (i, 0)),
                      pl.BlockSpec((None, 1, D), gidx),
                      pl.BlockSpec((N_EXPERTS, cap, D), lambda i, *_: (0, 0, 0), pipeline_mode=pl.Buffered(1))],
            out_specs=pl.BlockSpec((TOK_BLK, D), lambda i, *_: (i, 0)),
            scratch_shapes=[pltpu.VMEM((TOK_BLK, D), F32)]),
        out_shape=jax.ShapeDtypeStruct((n, D), F32),
        compiler_params=_cparams(("arbitrary",)),
    )(win.reshape(-1), slot, gate, x, g2, ye)


def _final_kernel(x_ref, g_ref, o_ref):
    x = x_ref[...]
    o_ref[...] = x * lax.rsqrt(jnp.mean(x * x, axis=-1, keepdims=True) + EPS) * g_ref[...]


def _final_norm(x, g):
    n = x.shape[0]
    tm = 1024
    return pl.pallas_call(
        _final_kernel,
        grid=(n // tm,),
        in_specs=[pl.BlockSpec((tm, D), lambda i: (i, 0)), pl.BlockSpec((1, D), lambda i: (0, 0))],
        out_specs=pl.BlockSpec((tm, D), lambda i: (i, 0)),
        out_shape=jax.ShapeDtypeStruct((n, D), F32),
        compiler_params=_cparams(("arbitrary",)),
    )(x, g.reshape(1, D))


def _layer(x, mod, prm, batch, seq_len, caug0, m0, h0_t, ck=None, cv=None, bias=None):
    sh1, sc1, g1, sh2, sc2, g2 = [m[:, None, :] for m in jnp.split(mod, 6, axis=-1)]
    qkv, mo, sz, sxbc, nq, nk, nv, gate = _in_projection(x, prm["norm1_g"], sh1, sc1, prm["w_in"], seq_len)
    xbc = _conv_silu(sxbc, prm["conv_w"], prm["conv_b"], seq_len)
    hf, hb, cfin, mfin = _mlstm_scan(qkv, gate, prm["f_bias"], caug0, m0, batch, seq_len)
    yf, yb, hfin = _ssd_scan(xbc, gate, prm["dt_bias"], prm["a_log"], h0_t, batch, seq_len)
    if ck is None:
        a_out = _ctx_attention(nq, nk, nv, batch, seq_len)
    else:
        a_out = _neighbourhood_attention(nq, nk, nv, ck, cv, bias, batch, seq_len)
    xn, h2, aff_t = _out_projection(x, hf, hb, mo, yf, yb, xbc, sz, a_out, prm["mnorm_g"], prm["d_skip"],
                                    prm["snorm_g"], prm["w_out"], g1, prm["norm2_g"], sh2, sc2,
                                    prm["router_wt"], seq_len)
    xo = _moe(h2, aff_t, xn, g2, prm["w1"], prm["w3"], prm["w2"], prm["layer"], seq_len)
    return xo, nk, nv, cfin, mfin, hfin


def _pack_w_in(w_in):
    o = np.cumsum((0, D_M, D_M, D_M, D_M, 2 * H_M, 2 * H_M, D_S, D_XBC, 2 * H_S, D_NA, D_NA, D_NA))
    pad = jnp.zeros(w_in.shape[:-1] + (GATE_W - 4 * H_M - 2 * H_S,), w_in.dtype)
    parts = [w_in[..., o[0]:o[4]], w_in[..., o[6]:o[8]], w_in[..., o[9]:o[12]],
             w_in[..., o[4]:o[6]], w_in[..., o[8]:o[9]], pad]
    return jnp.concatenate(parts, axis=-1).astype(BF16)


def kernel(x_prompt, x_sample, cache_na_k, cache_na_v, state_mlstm_c, state_mlstm_n, state_mlstm_m, state_ssm, c, c_ctx, ada_w, ada_b, norm1_g, norm2_g, w_in, mlstm_f_bias, mlstm_norm_g, conv_w, conv_b, ssm_dt_bias, ssm_a_log, ssm_d, ssm_norm_g, na_rpb, w_out, router_w, exp_w1, exp_w3, exp_w2, final_g):
    bp, sp, _ = x_prompt.shape
    bs, ss, _ = x_sample.shape
    past = cache_na_k.shape[2]

    cvec = jnp.zeros((8, D), F32).at[0].set(c_ctx).at[1:1 + bs].set(c)
    mod = _modulation(cvec, ada_w, ada_b)

    w_in_p = _pack_w_in(w_in)
    w_out_bf = w_out.astype(BF16)
    w1_bf, w3_bf, w2_bf = exp_w1.astype(BF16), exp_w3.astype(BF16), exp_w2.astype(BF16)
    router_wt = jnp.swapaxes(router_w, 1, 2)

    def aug(cs, ns):
        b = cs.shape[0]
        pad = jnp.zeros(cs.shape[:-1] + (128 - HD - 1,), F32)
        return jnp.concatenate([cs, ns[..., None], pad], axis=-1).reshape(b, 2 * H_M, HD, 128)

    def rep_m(ms):
        b = ms.shape[0]
        return jnp.broadcast_to(ms.reshape(b, 2 * H_M, 1, 1), (b, 2 * H_M, 8, 128)).astype(F32)

    ctx_caug0 = jnp.zeros((bp, 2 * H_M, HD, 128), F32)
    ctx_m0 = jnp.full((bp, 2 * H_M, 8, 128), NEG_INIT, F32)
    ctx_h0 = jnp.zeros((bp, 2 * H_S, N_S, HD), F32)

    xp = x_prompt.reshape(bp * sp, D)
    xs = x_sample.reshape(bs * ss, D)
    ks, vs, cs, ns, ms, hs = [], [], [], [], [], []
    for l in range(DEPTH):
        prm = dict(norm1_g=norm1_g[l], norm2_g=norm2_g[l], w_in=w_in_p[l], f_bias=mlstm_f_bias[l],
                   mnorm_g=mlstm_norm_g[l], conv_w=conv_w[l], conv_b=conv_b[l], dt_bias=ssm_dt_bias[l],
                   a_log=ssm_a_log[l], d_skip=ssm_d[l], snorm_g=ssm_norm_g[l], w_out=w_out_bf[l],
                   router_wt=router_wt[l], w1=w1_bf, w3=w3_bf, w2=w2_bf, layer=l)
        xp, nk, nv, cfin, mfin, hfin = _layer(xp, mod[l, 0:1], prm, bp, sp, ctx_caug0, ctx_m0, ctx_h0)
        ks.append(nk.reshape(bp, sp, H_NA, HD))
        vs.append(nv.reshape(bp, sp, H_NA, HD))
        cs.append(cfin[..., 0:HD].reshape(bp, 2, H_M, HD, HD))
        ns.append(cfin[..., HD].reshape(bp, 2, H_M, HD))
        ms.append(mfin[..., 0, 0].reshape(bp, 2, H_M))
        hs.append(hfin.reshape(bp, 2, H_S, HD, N_S))

        lat_caug0 = aug(state_mlstm_c[:, l], state_mlstm_n[:, l])
        lat_m0 = rep_m(state_mlstm_m[:, l])
        lat_h0 = state_ssm[:, l].reshape(bs, 2 * H_S, HD, N_S)
        ck = cache_na_k[:, l].reshape(bs * past, D_NA)
        cv = cache_na_v[:, l].reshape(bs * past, D_NA)
        bias = _natten_bias(na_rpb[l], ss // GRID_W)
        xs, *_ = _layer(xs, mod[l, 1:1 + bs], prm, bs, ss, lat_caug0, lat_m0, lat_h0, ck, cv, bias)

    y_prompt = _final_norm(xp, final_g).reshape(bp, sp, D)
    y_sample = _final_norm(xs, final_g).reshape(bs, ss, D)
    return (y_prompt, y_sample, jnp.stack(ks, axis=1), jnp.stack(vs, axis=1), jnp.stack(cs, axis=1),
            jnp.stack(ns, axis=1), jnp.stack(ms, axis=1), jnp.stack(hs, axis=1))
```

```python
import functools

import numpy as np
import jax
import jax.numpy as jnp
from jax import lax
from jax.experimental import pallas as pl
from jax.experimental.pallas import tpu as pltpu

F32 = jnp.float32
BF16 = jnp.bfloat16
HI = lax.Precision.HIGHEST

D = 1024
DEPTH = 4
HD = 64
H_M = 4
D_M = H_M * HD
H_S = 6
D_S = H_S * HD
G_S = 2
R_S = H_S // G_S
N_S = 64
D_XBC = D_S + 2 * G_S * N_S
H_NA = 6
D_NA = H_NA * HD
GRID_W = 64
WIN_H = 8
WIN_W = 16
N_EXPERTS = 16
EC_FACTOR = 2
EPS = 1e-6
NEG_INIT = -1e30
TOK_BLK = 128
SLOT_BLK = 128
GATHER_TOK = 512
GATE_W = 128
MI0, MF0, DT0 = 0, 2 * H_M, 4 * H_M
W_PACKED = 3 * D_M + D_M + D_S + D_XBC + 3 * D_NA + GATE_W

VMEM_LIMIT = 56 * 1024 * 1024


def _cparams(sem):
    return pltpu.CompilerParams(dimension_semantics=sem, vmem_limit_bytes=VMEM_LIMIT)


def _sigmoid(x):
    return 1.0 / (1.0 + jnp.exp(-x))


def _softplus(x):
    return jnp.maximum(x, 0.0) + jnp.log1p(jnp.exp(-jnp.abs(x)))


def _dot(a, b):
    return jnp.dot(a, b, preferred_element_type=F32)


def _dot_nt(a, b):
    return lax.dot_general(a, b, (((1,), (1,)), ((), ())), preferred_element_type=F32)


def _dot_tn(a, b):
    return lax.dot_general(a, b, (((0,), (0,)), ((), ())), preferred_element_type=F32)


def _mod_kernel(c_ref, w_ref, b_ref, o_ref):
    cv = c_ref[...]
    s = cv * _sigmoid(cv)
    o_ref[...] = jnp.dot(s, w_ref[...], precision=HI, preferred_element_type=F32) + b_ref[...]


def _modulation(cvec, ada_w, ada_b):
    tn = 1536
    return pl.pallas_call(
        _mod_kernel,
        grid=(DEPTH, 6 * D // tn),
        in_specs=[pl.BlockSpec((8, D), lambda l, j: (0, 0)),
                  pl.BlockSpec((None, D, tn), lambda l, j: (l, 0, j)),
                  pl.BlockSpec((None, 1, tn), lambda l, j: (l, 0, j))],
        out_specs=pl.BlockSpec((None, 8, tn), lambda l, j: (l, 0, j)),
        out_shape=jax.ShapeDtypeStruct((DEPTH, 8, 6 * D), F32),
        compiler_params=_cparams(("arbitrary", "arbitrary")),
    )(cvec, ada_w, ada_b.reshape(DEPTH, 1, 6 * D))


_IN_SEGS = (("qkv", 0, 768), ("mo", 768, 1024), ("sz", 1024, 1408), ("xbc", 1408, 2048),
            ("nq", 2048, 2432), ("nk", 2432, 2816), ("nv", 2816, 3200), ("gate", 3200, 3328))


def _inproj_kernel(x_ref, g_ref, sh_ref, sc_ref, w_ref, *outs):
    x = x_ref[...]
    ms = jnp.mean(x * x, axis=-1, keepdims=True)
    h = x * lax.rsqrt(ms + EPS) * g_ref[...]
    h = h * (1.0 + sc_ref[...]) + sh_ref[...]
    hb = h.astype(BF16)
    for o_ref, (_, lo, hi) in zip(outs, _IN_SEGS):
        o_ref[...] = _dot(hb, w_ref[:, lo:hi]).astype(o_ref.dtype)


def _in_projection(x, norm_g, shift, scale, w_packed, seq_len, attn_dtype):
    n = x.shape[0]
    tm = 512
    groups = shift.shape[0]
    if groups == 1:
        gidx = lambda i: (0, 0, 0)
    else:
        gidx = lambda i: ((i * tm) // seq_len, 0, 0)
    out_shape = tuple(jax.ShapeDtypeStruct((n, hi - lo), attn_dtype if name in ("nq", "nk", "nv") else F32)
                      for name, lo, hi in _IN_SEGS)
    out_specs = tuple(pl.BlockSpec((tm, hi - lo), lambda i: (i, 0)) for _, lo, hi in _IN_SEGS)
    return pl.pallas_call(
        _inproj_kernel,
        grid=(n // tm,),
        in_specs=[pl.BlockSpec((tm, D), lambda i: (i, 0)),
                  pl.BlockSpec((1, D), lambda i: (0, 0)),
                  pl.BlockSpec((None, 1, D), gidx),
                  pl.BlockSpec((None, 1, D), gidx),
                  pl.BlockSpec((D, W_PACKED), lambda i: (0, 0))],
        out_specs=out_specs,
        out_shape=out_shape,
        compiler_params=_cparams(("arbitrary",)),
    )(x, norm_g.reshape(1, D), shift, scale, w_packed)


def _conv_kernel(cur_ref, prev_ref, next_ref, w_ref, b_ref, o_ref, *, tiles_per_seq):
    i = pl.program_id(0)
    tt = cur_ref.shape[0]
    cur = cur_ref[...]
    first = (i % tiles_per_seq) == 0
    last = (i % tiles_per_seq) == tiles_per_seq - 1
    prow = jnp.where(first, 0.0, prev_ref[7:8, :])
    nrow = jnp.where(last, 0.0, next_ref[0:1, :])
    rows = lax.broadcasted_iota(jnp.int32, cur.shape, 0)
    prev = jnp.where(rows == 0, prow, pltpu.roll(cur, 1, 0))
    nxt = jnp.where(rows == tt - 1, nrow, pltpu.roll(cur, tt - 1, 0))
    y = prev * w_ref[0:1, :] + cur * w_ref[1:2, :] + nxt * w_ref[2:3, :] + b_ref[...]
    o_ref[...] = y * _sigmoid(y)


def _conv_silu(sxbc, conv_w, conv_b, seq_len):
    n = sxbc.shape[0]
    tt = min(seq_len, 512)
    tps = seq_len // tt
    nb8 = n // 8
    return pl.pallas_call(
        functools.partial(_conv_kernel, tiles_per_seq=tps),
        grid=(n // tt,),
        in_specs=[pl.BlockSpec((tt, D_XBC), lambda i: (i, 0)),
                  pl.BlockSpec((8, D_XBC), lambda i: (jnp.maximum(i * (tt // 8) - 1, 0), 0)),
                  pl.BlockSpec((8, D_XBC), lambda i: (jnp.minimum((i + 1) * (tt // 8), nb8 - 1), 0)),
                  pl.BlockSpec((3, D_XBC), lambda i: (0, 0)),
                  pl.BlockSpec((1, D_XBC), lambda i: (0, 0))],
        out_specs=pl.BlockSpec((tt, D_XBC), lambda i: (i, 0)),
        out_shape=jax.ShapeDtypeStruct((n, D_XBC), F32),
        compiler_params=_cparams(("arbitrary",)),
    )(sxbc, sxbc, sxbc, conv_w, conv_b.reshape(1, D_XBC))


def _tri_masks(ch):
    r = lax.broadcasted_iota(jnp.int32, (ch, ch), 0)
    c = lax.broadcasted_iota(jnp.int32, (ch, ch), 1)
    return r >= c, r <= c


def _mlstm_kernel(qkv_f, qkv_b, gt_f, gt_b, fb_ref, c0_ref, m0_ref,
                  hf_ref, hb_ref, cfin_ref, mfin_ref, c_scr, m_scr, it_scr, bt_scr, rc_scr, vt_scr, *, n_tiles, ch):
    j = pl.program_id(1)

    @pl.when(j == 0)
    def _():
        c_scr[...] = c0_ref[...]
        m_scr[...] = m0_ref[...]

    tt = qkv_f.shape[0]
    nch = tt // ch
    fb = fb_ref[...]
    low, upp = _tri_masks(ch)
    ones_row = (lax.broadcasted_iota(jnp.int32, (HD, ch), 0) == 0).astype(F32)
    pad_rows = jnp.zeros((ch - 2 * H_M, ch), F32)

    for d in range(2):
        qkv_ref = (qkv_f, qkv_b)[d]
        g_ref = (gt_f, gt_b)[d]
        mask = (upp, low)[d]
        for c in range(nch):
            rows = slice(c * ch, (c + 1) * ch)
            z_t = (g_ref[rows, :] + fb).T
            i_t = z_t[MI0:MI0 + 2 * H_M, :]
            zf = z_t[MF0:MF0 + 2 * H_M, :]
            logf = jnp.minimum(zf, 0.0) - jnp.log1p(jnp.exp(-jnp.abs(zf)))
            b_t = jnp.dot(logf, mask.astype(F32), precision=HI, preferred_element_type=F32)
            it_scr[d, c] = i_t
            bt_scr[d, c] = b_t
            rc_scr[d, c] = jnp.concatenate([i_t - b_t, pad_rows], axis=0).T
            for p in range(H_M // 2):
                vt_scr[d, c, p * 2 * HD:(p + 1) * 2 * HD, :] = qkv_ref[rows, 2 * D_M + p * 2 * HD:2 * D_M + (p + 1) * 2 * HD].T

    def chunk(ci, carry):
        for d in range(2):
            qkv_ref = (qkv_f, qkv_b)[d]
            out_ref = (hf_ref, hb_ref)[d]
            mask = (upp, low)[d]
            cidx = ci if d == 0 else nch - 1 - ci
            c0 = pl.multiple_of(cidx * ch, ch)
            i_t = it_scr[d, cidx]
            b_t = bt_scr[d, cidx]
            r_cols = rc_scr[d, cidx]
            edge = ch - 1 if d == 0 else 0
            h_t = []
            for h in range(H_M):
                idx = d * H_M + h
                vaug = jnp.concatenate([vt_scr[d, cidx, h * HD:(h + 1) * HD, :], ones_row], axis=0)
                q = qkv_ref[pl.ds(c0, ch), h * HD:(h + 1) * HD].astype(BF16)
                k = (qkv_ref[pl.ds(c0, ch), D_M + h * HD:D_M + (h + 1) * HD] * (HD ** -0.5)).astype(BF16)
                b_row = b_t[idx:idx + 1, :]
                i_row = i_t[idx:idx + 1, :]
                m_prev = m_scr[idx][0:1, 0:1]
                caug = c_scr[idx]
                logd = jnp.where(mask, r_cols[:, idx:idx + 1] + b_row, -jnp.inf)
                m_inter = b_row + m_prev
                m_t = jnp.maximum(jnp.max(logd, axis=0, keepdims=True), m_inter)
                dmat = jnp.exp(logd - m_t)
                w_inter = jnp.exp(m_inter - m_t)
                s = (_dot_nt(k, q) * dmat).astype(BF16)
                tot = _dot(vaug.astype(BF16), s) + w_inter * _dot_nt(caug.astype(BF16), q)
                den = jnp.maximum(jnp.abs(tot[HD:HD + 1, :]), jnp.exp(-m_t))
                h_t.append(tot[0:HD, :] / den)
                if h % 2 == 1:
                    pair = jnp.concatenate(h_t[-2:], axis=0).T
                    out_ref[pl.ds(c0, ch), (h - 1) * HD:(h + 1) * HD] = pair
                b_l = b_row[:, edge:edge + 1]
                lw = b_l - b_row + i_row
                m_new = jnp.maximum(b_l + m_prev, jnp.max(lw, axis=1, keepdims=True))
                wk = jnp.exp(lw - m_new)
                decay = jnp.exp(b_l + m_prev - m_new)
                c_scr[idx] = decay * caug + _dot((vaug * wk).astype(BF16), k)
                m_scr[idx] = jnp.broadcast_to(m_new, (8, 128))
        return carry

    lax.fori_loop(0, nch, chunk, 0)

    @pl.when(j == n_tiles - 1)
    def _():
        cfin_ref[...] = c_scr[...]
        mfin_ref[...] = m_scr[...]


def _mlstm_scan(qkv, gate, f_bias, caug0, m0, batch, seq_len):
    n = qkv.shape[0]
    tt = min(seq_len, 512)
    nt = seq_len // tt
    ch = 128
    fb = jnp.zeros((1, GATE_W), F32).at[0, MF0:MF0 + 2 * H_M].set(f_bias.reshape(-1))
    fwd = lambda b, j: (b * nt + j, 0)
    bwd = lambda b, j: (b * nt + nt - 1 - j, 0)
    st4 = lambda b, j: (b, 0, 0, 0)
    return pl.pallas_call(
        functools.partial(_mlstm_kernel, n_tiles=nt, ch=ch),
        grid=(batch, nt),
        in_specs=[pl.BlockSpec((tt, 3 * D_M), fwd), pl.BlockSpec((tt, 3 * D_M), bwd),
                  pl.BlockSpec((tt, GATE_W), fwd), pl.BlockSpec((tt, GATE_W), bwd),
                  pl.BlockSpec((1, GATE_W), lambda b, j: (0, 0)),
                  pl.BlockSpec((None, 2 * H_M, 128, HD), st4),
                  pl.BlockSpec((None, 2 * H_M, 8, 128), st4)],
        out_specs=(pl.BlockSpec((tt, D_M), fwd), pl.BlockSpec((tt, D_M), bwd),
                   pl.BlockSpec((None, 2 * H_M, 128, HD), st4),
                   pl.BlockSpec((None, 2 * H_M, 8, 128), st4)),
        out_shape=(jax.ShapeDtypeStruct((n, D_M), F32), jax.ShapeDtypeStruct((n, D_M), F32),
                   jax.ShapeDtypeStruct((batch, 2 * H_M, 128, HD), F32),
                   jax.ShapeDtypeStruct((batch, 2 * H_M, 8, 128), F32)),
        scratch_shapes=[pltpu.VMEM((2 * H_M, 128, HD), F32), pltpu.VMEM((2 * H_M, 8, 128), F32),
                        pltpu.VMEM((2, tt // ch, 2 * H_M, ch), F32), pltpu.VMEM((2, tt // ch, 2 * H_M, ch), F32),
                        pltpu.VMEM((2, tt // ch, ch, 128), F32), pltpu.VMEM((2, tt // ch, D_M, ch), F32)],
        compiler_params=_cparams(("arbitrary", "arbitrary")),
    )(qkv, qkv, gate, gate, fb, caug0, m0)


def _ssd_kernel(xbc_f, xbc_b, gt_f, gt_b, dtb_ref, alog_ref, h0_ref,
                yf_ref, yb_ref, hfin_ref, h_scr, dtt_scr, at_scr, ac_scr, xt_scr, *, n_tiles, ch):
    j = pl.program_id(1)

    @pl.when(j == 0)
    def _():
        h_scr[...] = h0_ref[...]

    tt = xbc_f.shape[0]
    nch = tt // ch
    dtb = dtb_ref[...]
    a_neg = -jnp.exp(alog_ref[...])
    low, upp = _tri_masks(ch)
    nrow = 16
    pad_rows = jnp.zeros((ch - nrow, ch), F32)

    for d in range(2):
        x_ref = (xbc_f, xbc_b)[d]
        g_ref = (gt_f, gt_b)[d]
        mask = (upp, low)[d]
        for c in range(nch):
            rows = slice(c * ch, (c + 1) * ch)
            dt = _softplus(g_ref[rows, :] + dtb)
            dt_t = dt.T[DT0:DT0 + nrow, :]
            da_t = (dt * a_neg).T[DT0:DT0 + nrow, :]
            a_t = jnp.dot(da_t, mask.astype(F32), precision=HI, preferred_element_type=F32)
            dtt_scr[d, c] = dt_t
            at_scr[d, c] = a_t
            ac_scr[d, c] = jnp.concatenate([a_t, pad_rows], axis=0).T
            for p in range(H_S // 2):
                xt_scr[d, c, p * 2 * HD:(p + 1) * 2 * HD, :] = x_ref[rows, p * 2 * HD:(p + 1) * 2 * HD].T

    def chunk(ci, carry):
        for d in range(2):
            x_ref = (xbc_f, xbc_b)[d]
            out_ref = (yf_ref, yb_ref)[d]
            mask = (upp, low)[d]
            cidx = ci if d == 0 else nch - 1 - ci
            c0 = pl.multiple_of(cidx * ch, ch)
            dt_t = dtt_scr[d, cidx]
            a_t = at_scr[d, cidx]
            a_cols = ac_scr[d, cidx]
            edge = ch - 1 if d == 0 else 0
            y_t = []
            for grp in range(G_S):
                bm = x_ref[pl.ds(c0, ch), D_S + grp * N_S:D_S + (grp + 1) * N_S].astype(BF16)
                cm = x_ref[pl.ds(c0, ch), D_S + G_S * N_S + grp * N_S:D_S + G_S * N_S + (grp + 1) * N_S].astype(BF16)
                cb = _dot_nt(bm, cm)
                for r in range(R_S):
                    hh = grp * R_S + r
                    idx = d * H_S + hh
                    a_row = a_t[idx:idx + 1, :]
                    dt_row = dt_t[idx:idx + 1, :]
                    seg = jnp.exp(jnp.where(mask, a_row - a_cols[:, idx:idx + 1], -jnp.inf))
                    mm = (cb * seg).astype(BF16)
                    xh = xt_scr[d, cidx, hh * HD:(hh + 1) * HD, :]
                    h_st = h_scr[idx]
                    y_t.append(_dot((xh * dt_row).astype(BF16), mm) + _dot_nt(h_st.astype(BF16), cm) * jnp.exp(a_row))
                    if hh % 2 == 1:
                        pair = jnp.concatenate(y_t[-2:], axis=0).T
                        out_ref[pl.ds(c0, ch), (hh - 1) * HD:(hh + 1) * HD] = pair
                    a_l = a_row[:, edge:edge + 1]
                    wk = jnp.exp(a_l - a_row) * dt_row
                    h_scr[idx] = jnp.exp(a_l) * h_st + _dot((xh * wk).astype(BF16), bm)
        return carry

    lax.fori_loop(0, nch, chunk, 0)

    @pl.when(j == n_tiles - 1)
    def _():
        hfin_ref[...] = h_scr[...]


def _ssd_scan(xbc, gate, dt_bias, a_log, h0_t, batch, seq_len):
    n = xbc.shape[0]
    tt = min(seq_len, 512)
    nt = seq_len // tt
    ch = 128
    dtb = jnp.zeros((1, GATE_W), F32).at[0, DT0:DT0 + 2 * H_S].set(dt_bias.reshape(-1))
    alog = jnp.zeros((1, GATE_W), F32).at[0, DT0:DT0 + 2 * H_S].set(a_log.reshape(-1))
    fwd = lambda b, j: (b * nt + j, 0)
    bwd = lambda b, j: (b * nt + nt - 1 - j, 0)
    st4 = lambda b, j: (b, 0, 0, 0)
    return pl.pallas_call(
        functools.partial(_ssd_kernel, n_tiles=nt, ch=ch),
        grid=(batch, nt),
        in_specs=[pl.BlockSpec((tt, D_XBC), fwd), pl.BlockSpec((tt, D_XBC), bwd),
                  pl.BlockSpec((tt, GATE_W), fwd), pl.BlockSpec((tt, GATE_W), bwd),
                  pl.BlockSpec((1, GATE_W), lambda b, j: (0, 0)),
                  pl.BlockSpec((1, GATE_W), lambda b, j: (0, 0)),
                  pl.BlockSpec((None, 2 * H_S, N_S, HD), st4)],
        out_specs=(pl.BlockSpec((tt, D_S), fwd), pl.BlockSpec((tt, D_S), bwd),
                   pl.BlockSpec((None, 2 * H_S, N_S, HD), st4)),
        out_shape=(jax.ShapeDtypeStruct((n, D_S), F32), jax.ShapeDtypeStruct((n, D_S), F32),
                   jax.ShapeDtypeStruct((batch, 2 * H_S, N_S, HD), F32)),
        scratch_shapes=[pltpu.VMEM((2 * H_S, N_S, HD), F32),
                        pltpu.VMEM((2, tt // ch, 16, ch), F32), pltpu.VMEM((2, tt // ch, 16, ch), F32),
                        pltpu.VMEM((2, tt // ch, ch, 128), F32), pltpu.VMEM((2, tt // ch, D_S, ch), F32)],
        compiler_params=_cparams(("arbitrary", "arbitrary")),
    )(xbc, xbc, gate, gate, dtb, alog, h0_t)


def _ctx_attn_kernel(q_ref, k_ref, v_ref, o_ref):
    for h in range(H_NA):
        sl = slice(h * HD, (h + 1) * HD)
        q = (q_ref[:, sl] * (HD ** -0.5)).astype(BF16)
        s = _dot_nt(q, k_ref[:, sl].astype(BF16))
        m = jnp.max(s, axis=-1, keepdims=True)
        p = jnp.exp(s - m)
        l = jnp.sum(p, axis=-1, keepdims=True)
        o_ref[:, sl] = _dot(p.astype(BF16), v_ref[:, sl].astype(BF16)) / l


def _ctx_attention(q, k, v, batch, seq_len):
    spec = pl.BlockSpec((seq_len, D_NA), lambda b: (b, 0))
    return pl.pallas_call(
        _ctx_attn_kernel,
        grid=(batch,),
        in_specs=[spec, spec, spec],
        out_specs=spec,
        out_shape=jax.ShapeDtypeStruct(q.shape, F32),
        compiler_params=_cparams(("arbitrary",)),
    )(q, k, v)


def _row_start(r, rows):
    return jnp.clip(r - WIN_H // 2, 0, rows - WIN_H)


def _natten_kernel(q_ref, k_ref, v_ref, ck_ref, cv_ref, bias_ref, o_ref, *, rows):
    r = pl.program_id(1)
    k0 = pl.multiple_of(_row_start(r, rows) * GRID_W, GRID_W)
    nloc = WIN_H * GRID_W
    for h in range(H_NA):
        sl = slice(h * HD, (h + 1) * HD)
        q = (q_ref[:, sl].astype(F32) * (HD ** -0.5)).astype(BF16)
        kw = k_ref[pl.ds(k0, nloc), sl].astype(BF16)
        vw = v_ref[pl.ds(k0, nloc), sl].astype(BF16)
        s_loc = _dot_nt(q, kw) + bias_ref[h]
        s_ctx = _dot_nt(q, ck_ref[:, sl].astype(BF16))
        m = jnp.maximum(jnp.max(s_loc, axis=-1, keepdims=True), jnp.max(s_ctx, axis=-1, keepdims=True))
        p_loc = jnp.exp(s_loc - m)
        p_ctx = jnp.exp(s_ctx - m)
        l = jnp.sum(p_loc, axis=-1, keepdims=True) + jnp.sum(p_ctx, axis=-1, keepdims=True)
        o = _dot(p_loc.astype(BF16), vw) + _dot(p_ctx.astype(BF16), cv_ref[:, sl].astype(BF16))
        o_ref[:, sl] = o / l


def _natten_bias(rpb, rows):
    qc = np.arange(GRID_W)[:, None]
    kc = np.arange(GRID_W)[None, :]
    cstart = np.clip(qc - WIN_W // 2, 0, GRID_W - WIN_W)
    ok = (kc >= cstart) & (kc < cstart + WIN_W)
    col_off = np.clip(kc - qc + WIN_W - 1, 0, 2 * WIN_W - 2)
    t = jnp.where(jnp.asarray(ok)[None, None], rpb[:, :, col_off], -jnp.inf)
    var = np.arange(WIN_H)[:, None] + np.arange(WIN_H)[None, :]
    b = t[:, var]
    return b.transpose(0, 1, 3, 2, 4).reshape(H_NA, WIN_H, GRID_W, WIN_H * GRID_W).astype(F32)


def _neighbourhood_attention(q, k, v, ck, cv, bias, batch, seq_len):
    rows = seq_len // GRID_W
    past = ck.shape[0] // batch
    bvar = lambda b, r: (0, _row_start(r, rows) - r + WIN_H - 1, 0, 0)
    return pl.pallas_call(
        functools.partial(_natten_kernel, rows=rows),
        grid=(batch, rows),
        in_specs=[pl.BlockSpec((GRID_W, D_NA), lambda b, r: (b * rows + r, 0)),
                  pl.BlockSpec((seq_len, D_NA), lambda b, r: (b, 0)),
                  pl.BlockSpec((seq_len, D_NA), lambda b, r: (b, 0)),
                  pl.BlockSpec((past, D_NA), lambda b, r: (b, 0)),
                  pl.BlockSpec((past, D_NA), lambda b, r: (b, 0)),
                  pl.BlockSpec((H_NA, None, GRID_W, WIN_H * GRID_W), bvar)],
        out_specs=pl.BlockSpec((GRID_W, D_NA), lambda b, r: (b * rows + r, 0)),
        out_shape=jax.ShapeDtypeStruct(q.shape, F32),
        compiler_params=_cparams(("arbitrary", "arbitrary")),
    )(q, k, v, ck, cv, bias)


def _outproj_kernel(x_ref, hf_ref, hb_ref, mo_ref, yf_ref, yb_ref, xbc_ref, sz_ref, a_ref,
                    mg_ref, dsk_ref, sg_ref, bd_ref, w_ref, g1_ref, n2_ref, sh2_ref, sc2_ref, rw_ref,
                    xo_ref, h2_ref, aff_ref):
    hm = hf_ref[...] + hb_ref[...]
    ssq = jnp.dot(hm * hm, bd_ref[...], precision=HI, preferred_element_type=F32)
    m_out = hm * lax.rsqrt(ssq * (1.0 / HD) + EPS) * mg_ref[...] * _sigmoid(mo_ref[...])
    sz = sz_ref[...]
    ys = (yf_ref[...] + yb_ref[...] + dsk_ref[...] * xbc_ref[:, 0:D_S]) * (sz * _sigmoid(sz))
    s_out = ys * lax.rsqrt(jnp.mean(ys * ys, axis=-1, keepdims=True) + EPS) * sg_ref[...]
    y = (_dot(m_out.astype(BF16), w_ref[0:D_M, :])
         + _dot(s_out.astype(BF16), w_ref[D_M:D_M + D_S, :])
         + _dot(a_ref[...].astype(BF16), w_ref[D_M + D_S:D, :]))
    xn = x_ref[...] + g1_ref[...] * y
    xo_ref[...] = xn
    h2 = xn * lax.rsqrt(jnp.mean(xn * xn, axis=-1, keepdims=True) + EPS) * n2_ref[...]
    h2 = h2 * (1.0 + sc2_ref[...]) + sh2_ref[...]
    h2_ref[...] = h2.astype(BF16)
    logits = lax.dot_general(rw_ref[...], h2, (((1,), (1,)), ((), ())), precision=HI,
                             preferred_element_type=F32)
    mx = jnp.max(logits, axis=0, keepdims=True)
    ex = jnp.exp(logits - mx)
    aff_ref[...] = ex / jnp.sum(ex, axis=0, keepdims=True)


def _out_projection(x, hf, hb, mo, yf, yb, xbc, sz, a_out, mnorm_g, d_skip, snorm_g, w_out_bf,
                    g1, norm2_g, sh2, sc2, router_wt, seq_len):
    n = x.shape[0]
    tm = 512
    groups = g1.shape[0]
    if groups == 1:
        gidx = lambda i: (0, 0, 0)
    else:
        gidx = lambda i: ((i * tm) // seq_len, 0, 0)
    row = lambda w: pl.BlockSpec((tm, w), lambda i: (i, 0))
    const = lambda s: pl.BlockSpec(s, lambda i: (0,) * len(s))
    hid = np.arange(D_M) // HD
    blockdiag = jnp.asarray((hid[:, None] == hid[None, :]).astype(np.float32))
    return pl.pallas_call(
        _outproj_kernel,
        grid=(n // tm,),
        in_specs=[row(D), row(D_M), row(D_M), row(D_M), row(D_S), row(D_S), row(D_XBC), row(D_S), row(D_NA),
                  const((1, D_M)), const((1, D_S)), const((1, D_S)), const((D_M, D_M)), const((D, D)),
                  pl.BlockSpec((None, 1, D), gidx), const((1, D)),
                  pl.BlockSpec((None, 1, D), gidx), pl.BlockSpec((None, 1, D), gidx),
                  const((N_EXPERTS, D))],
        out_specs=(row(D), row(D), pl.BlockSpec((N_EXPERTS, tm), lambda i: (0, i))),
        out_shape=(jax.ShapeDtypeStruct((n, D), F32), jax.ShapeDtypeStruct((n, D), BF16),
                   jax.ShapeDtypeStruct((N_EXPERTS, n), F32)),
        compiler_params=_cparams(("arbitrary",)),
    )(x, hf, hb, mo, yf, yb, xbc, sz, a_out,
      mnorm_g.reshape(1, D_M), jnp.repeat(d_skip, HD).reshape(1, D_S), snorm_g.reshape(1, D_S), blockdiag,
      w_out_bf, g1, norm2_g.reshape(1, D), sh2, sc2, router_wt)


def _select_kernel(aff_ref, gate_ref, slot_ref, start_ref, *, cap):
    aff = aff_ref[...]
    bits = pltpu.bitcast(aff, jnp.int32)
    n_tok = aff.shape[1]

    def step(i, prefix):
        cand = prefix | (jnp.int32(1) << (30 - i))
        cnt = jnp.sum((bits >= cand).astype(F32), axis=1, keepdims=True)
        return jnp.where(cnt >= cap, cand, prefix)

    thr = lax.fori_loop(0, 31, step, jnp.zeros((aff.shape[0], 1), jnp.int32))
    gt = bits > thr
    eq = bits == thr
    need = cap - jnp.sum(gt.astype(F32), axis=1, keepdims=True)
    r_i = lax.broadcasted_iota(jnp.int32, (128, 128), 0)
    c_i = lax.broadcasted_iota(jnp.int32, (128, 128), 1)
    strict = (r_i < c_i).astype(BF16)
    run_eq = jnp.zeros((aff.shape[0], 1), F32)
    run_sel = jnp.zeros((aff.shape[0], 1), F32)
    lane = lax.broadcasted_iota(jnp.int32, (aff.shape[0], 128), 1)
    starts = jnp.zeros((aff.shape[0], 128), F32)
    for blk in range(n_tok // TOK_BLK):
        sl = slice(blk * TOK_BLK, (blk + 1) * TOK_BLK)
        e = eq[:, sl]
        rank = _dot(e.astype(BF16), strict) + run_eq
        keep = gt[:, sl] | (e & (rank < need))
        kf = jnp.where(keep, 1.0, 0.0)
        slot = _dot(kf.astype(BF16), strict) + run_sel
        gate_ref[:, sl] = jnp.where(keep, aff[:, sl], 0.0)
        slot_ref[:, sl] = jnp.where(keep, slot, -1.0).astype(jnp.int32)
        starts = jnp.where(lane == blk, run_sel, starts)
        run_eq = run_eq + jnp.sum(e.astype(F32), axis=1, keepdims=True)
        run_sel = run_sel + jnp.sum(kf, axis=1, keepdims=True)
    start_ref[...] = starts.astype(jnp.int32)


def _select(aff_t):
    n = aff_t.shape[1]
    cap = EC_FACTOR * n // N_EXPERTS
    assert n // TOK_BLK <= 128
    return pl.pallas_call(
        functools.partial(_select_kernel, cap=float(cap)),
        out_shape=(jax.ShapeDtypeStruct(aff_t.shape, F32), jax.ShapeDtypeStruct(aff_t.shape, jnp.int32),
                   jax.ShapeDtypeStruct((N_EXPERTS, 128), jnp.int32)),
        compiler_params=pltpu.CompilerParams(vmem_limit_bytes=VMEM_LIMIT),
    )(aff_t)


def _expert_kernel(lo_ref, hi_ref, slot_ref, h2_ref, w1_ref, w3_ref, w2_ref, ye_ref, xe_scr, acc_scr, *, cap):
    e = pl.program_id(0)
    nsb = cap // SLOT_BLK
    slot_iota = lax.broadcasted_iota(jnp.int32, (SLOT_BLK, GATHER_TOK), 0)
    for sb in range(nsb):
        acc_scr[...] = jnp.zeros_like(acc_scr)
        want = slot_iota + sb * SLOT_BLK

        def body(tb, carry):
            t0 = pl.multiple_of(tb * GATHER_TOK, GATHER_TOK)
            onehot = jnp.where(slot_ref[:, pl.ds(t0, GATHER_TOK)] == want, 1.0, 0.0).astype(BF16)
            acc_scr[...] += _dot(onehot, h2_ref[pl.ds(t0, GATHER_TOK), :])
            return carry

        lax.fori_loop(lo_ref[e * nsb + sb], hi_ref[e * nsb + sb] + 1, body, 0)
        xe_scr[sb * SLOT_BLK:(sb + 1) * SLOT_BLK, :] = acc_scr[...].astype(BF16)
    half = cap // 2
    for r in range(2):
        x = xe_scr[r * half:(r + 1) * half, :]
        a = _dot(x, w1_ref[...])
        b = _dot(x, w3_ref[...])
        hid = (a * _sigmoid(a) * b).astype(BF16)
        ye_ref[r * half:(r + 1) * half, :] = _dot(hid, w2_ref[...]).astype(BF16)


def _combine_kernel(win_ref, slot_ref, gate_ref, x_ref, g2_ref, ye_ref, o_ref, acc_scr):
    tb = pl.program_id(0)
    lane = lax.broadcasted_iota(jnp.int32, (TOK_BLK, 2 * SLOT_BLK), 1)
    fill = jnp.zeros((TOK_BLK - N_EXPERTS, TOK_BLK), F32)
    slot_t = jnp.concatenate([slot_ref[...].astype(F32), fill], axis=0).T
    gate_t = jnp.concatenate([gate_ref[...], fill], axis=0).T
    acc_scr[...] = jnp.zeros_like(acc_scr)
    for e in range(N_EXPERTS):
        s0 = pl.multiple_of(win_ref[tb * N_EXPERTS + e] * SLOT_BLK, SLOT_BLK)
        want = (lane + s0).astype(F32)
        onehot = jnp.where(slot_t[:, e:e + 1] == want, 1.0, 0.0).astype(BF16)
        acc_scr[...] += _dot(onehot, ye_ref[e, pl.ds(s0, 2 * SLOT_BLK), :]) * gate_t[:, e:e + 1]
    o_ref[...] = x_ref[...] + g2_ref[...] * acc_scr[...]


def _moe(h2, aff_t, x, g2, w1, w3, w2, layer, seq_len):
    n = x.shape[0]
    cap = EC_FACTOR * n // N_EXPERTS
    nsb = cap // SLOT_BLK
    ntb = n // TOK_BLK
    gate, slot, starts = _select(aff_t)

    cs = starts[:, :ntb]
    win = jnp.minimum(cs // SLOT_BLK, nsb - 2).T.astype(jnp.int32)
    gs = cs[:, ::GATHER_TOK // TOK_BLK]
    gend = jnp.concatenate([gs[:, 1:], jnp.full((N_EXPERTS, 1), cap, jnp.int32)], axis=1)
    edges = jnp.arange(nsb, dtype=jnp.int32) * SLOT_BLK
    lo = jnp.sum(gend[:, None, :] <= edges[None, :, None], axis=-1).astype(jnp.int32)
    hi = jnp.sum(gs[:, None, :] < (edges + SLOT_BLK)[None, :, None], axis=-1).astype(jnp.int32) - 1

    wspec = pl.BlockSpec((None, None, D, D), lambda e, *_: (layer, e, 0, 0))
    ye = pl.pallas_call(
        functools.partial(_expert_kernel, cap=cap),
        grid_spec=pltpu.PrefetchScalarGridSpec(
            num_scalar_prefetch=2,
            grid=(N_EXPERTS,),
            in_specs=[pl.BlockSpec((None, 1, n), lambda e, *_: (e, 0, 0)),
                      pl.BlockSpec((n, D), lambda e, *_: (0, 0), pipeline_mode=pl.Buffered(1)),
                      wspec, wspec, wspec],
            out_specs=pl.BlockSpec((None, cap, D), lambda e, *_: (e, 0, 0)),
            scratch_shapes=[pltpu.VMEM((cap, D), BF16), pltpu.VMEM((SLOT_BLK, D), F32)]),
        out_shape=jax.ShapeDtypeStruct((N_EXPERTS, cap, D), BF16),
        compiler_params=_cparams(("arbitrary",)),
    )(lo.reshape(-1), hi.reshape(-1), slot.reshape(N_EXPERTS, 1, n), h2, w1, w3, w2)

    groups = g2.shape[0]
    if groups == 1:
        gidx = lambda i, *_: (0, 0, 0)
    else:
        gidx = lambda i, *_: ((i * TOK_BLK) // seq_len, 0, 0)
    return pl.pallas_call(
        _combine_kernel,
        grid_spec=pltpu.PrefetchScalarGridSpec(
            num_scalar_prefetch=1,
            grid=(ntb,),
            in_specs=[pl.BlockSpec((N_EXPERTS, TOK_BLK), lambda i, *_: (0, i)),
                      pl.BlockSpec((N_EXPERTS, TOK_BLK), lambda i, *_: (0, i)),
                      pl.BlockSpec((TOK_BLK, D), lambda i, *_: (i, 0)),
                      pl.BlockSpec((None, 1, D), gidx),
                      pl.BlockSpec((N_EXPERTS, cap, D), lambda i, *_: (0, 0, 0), pipeline_mode=pl.Buffered(1))],
            out_specs=pl.BlockSpec((TOK_BLK, D), lambda i, *_: (i, 0)),
            scratch_shapes=[pltpu.VMEM((TOK_BLK, D), F32)]),
        out_shape=jax.ShapeDtypeStruct((n, D), F32),
        compiler_params=_cparams(("arbitrary",)),
    )(win.reshape(-1), slot, gate, x, g2, ye)


def _final_kernel(x_ref, g_ref, o_ref):
    x = x_ref[...]
    o_ref[...] = x * lax.rsqrt(jnp.mean(x * x, axis=-1, keepdims=True) + EPS) * g_ref[...]


def _final_norm(x, g):
    n = x.shape[0]
    tm = 1024
    return pl.pallas_call(
        _final_kernel,
        grid=(n // tm,),
        in_specs=[pl.BlockSpec((tm, D), lambda i: (i, 0)), pl.BlockSpec((1, D), lambda i: (0, 0))],
        out_specs=pl.BlockSpec((tm, D), lambda i: (i, 0)),
        out_shape=jax.ShapeDtypeStruct((n, D), F32),
        compiler_params=_cparams(("arbitrary",)),
    )(x, g.reshape(1, D))


def _layer(x, mod, prm, batch, seq_len, caug0, m0, h0_t, ck=None, cv=None, bias=None):
    sh1, sc1, g1, sh2, sc2, g2 = [m[:, None, :] for m in jnp.split(mod, 6, axis=-1)]
    attn_dtype = F32 if ck is None else BF16
    qkv, mo, sz, sxbc, nq, nk, nv, gate = _in_projection(x, prm["norm1_g"], sh1, sc1, prm["w_in"], seq_len, attn_dtype)
    xbc = _conv_silu(sxbc, prm["conv_w"], prm["conv_b"], seq_len)
    hf, hb, cfin, mfin = _mlstm_scan(qkv, gate, prm["f_bias"], caug0, m0, batch, seq_len)
    yf, yb, hfin = _ssd_scan(xbc, gate, prm["dt_bias"], prm["a_log"], h0_t, batch, seq_len)
    if ck is None:
        a_out = _ctx_attention(nq, nk, nv, batch, seq_len)
    else:
        a_out = _neighbourhood_attention(nq, nk, nv, ck, cv, bias, batch, seq_len)
    xn, h2, aff_t = _out_projection(x, hf, hb, mo, yf, yb, xbc, sz, a_out, prm["mnorm_g"], prm["d_skip"],
                                    prm["snorm_g"], prm["w_out"], g1, prm["norm2_g"], sh2, sc2,
                                    prm["router_wt"], seq_len)
    xo = _moe(h2, aff_t, xn, g2, prm["w1"], prm["w3"], prm["w2"], prm["layer"], seq_len)
    return xo, nk, nv, cfin, mfin, hfin


def _pack_w_in(w_in):
    o = np.cumsum((0, D_M, D_M, D_M, D_M, 2 * H_M, 2 * H_M, D_S, D_XBC, 2 * H_S, D_NA, D_NA, D_NA))
    pad = jnp.zeros(w_in.shape[:-1] + (GATE_W - 4 * H_M - 2 * H_S,), w_in.dtype)
    parts = [w_in[..., o[0]:o[4]], w_in[..., o[6]:o[8]], w_in[..., o[9]:o[12]],
             w_in[..., o[4]:o[6]], w_in[..., o[8]:o[9]], pad]
    return jnp.concatenate(parts, axis=-1).astype(BF16)


def kernel(x_prompt, x_sample, cache_na_k, cache_na_v, state_mlstm_c, state_mlstm_n, state_mlstm_m, state_ssm, c, c_ctx, ada_w, ada_b, norm1_g, norm2_g, w_in, mlstm_f_bias, mlstm_norm_g, conv_w, conv_b, ssm_dt_bias, ssm_a_log, ssm_d, ssm_norm_g, na_rpb, w_out, router_w, exp_w1, exp_w3, exp_w2, final_g):
    bp, sp, _ = x_prompt.shape
    bs, ss, _ = x_sample.shape
    past = cache_na_k.shape[2]

    cvec = jnp.zeros((8, D), F32).at[0].set(c_ctx).at[1:1 + bs].set(c)
    mod = _modulation(cvec, ada_w, ada_b)

    w_in_p = _pack_w_in(w_in)
    w_out_bf = w_out.astype(BF16)
    w1_bf, w3_bf, w2_bf = exp_w1.astype(BF16), exp_w3.astype(BF16), exp_w2.astype(BF16)
    router_wt = jnp.swapaxes(router_w, 1, 2)

    def aug(cs, ns):
        b = cs.shape[0]
        pad = jnp.zeros(cs.shape[:-2] + (128 - HD - 1, HD), F32)
        return jnp.concatenate([jnp.swapaxes(cs, -1, -2), ns[..., None, :], pad], axis=-2).reshape(b, 2 * H_M, 128, HD)

    def rep_m(ms):
        b = ms.shape[0]
        return jnp.broadcast_to(ms.reshape(b, 2 * H_M, 1, 1), (b, 2 * H_M, 8, 128)).astype(F32)

    ctx_caug0 = jnp.zeros((bp, 2 * H_M, 128, HD), F32)
    ctx_m0 = jnp.full((bp, 2 * H_M, 8, 128), NEG_INIT, F32)
    ctx_h0 = jnp.zeros((bp, 2 * H_S, N_S, HD), F32)

    xp = x_prompt.reshape(bp * sp, D)
    xs = x_sample.reshape(bs * ss, D)
    ks, vs, cs, ns, ms, hs = [], [], [], [], [], []
    for l in range(DEPTH):
        prm = dict(norm1_g=norm1_g[l], norm2_g=norm2_g[l], w_in=w_in_p[l], f_bias=mlstm_f_bias[l],
                   mnorm_g=mlstm_norm_g[l], conv_w=conv_w[l], conv_b=conv_b[l], dt_bias=ssm_dt_bias[l],
                   a_log=ssm_a_log[l], d_skip=ssm_d[l], snorm_g=ssm_norm_g[l], w_out=w_out_bf[l],
                   router_wt=router_wt[l], w1=w1_bf, w3=w3_bf, w2=w2_bf, layer=l)
        xp, nk, nv, cfin, mfin, hfin = _layer(xp, mod[l, 0:1], prm, bp, sp, ctx_caug0, ctx_m0, ctx_h0)
        ks.append(nk.reshape(bp, sp, H_NA, HD))
        vs.append(nv.reshape(bp, sp, H_NA, HD))
        cs.append(jnp.swapaxes(cfin[..., 0:HD, :], -1, -2).reshape(bp, 2, H_M, HD, HD))
        ns.append(cfin[..., HD, :].reshape(bp, 2, H_M, HD))
        ms.append(mfin[..., 0, 0].reshape(bp, 2, H_M))
        hs.append(hfin.reshape(bp, 2, H_S, HD, N_S))

        lat_caug0 = aug(state_mlstm_c[:, l], state_mlstm_n[:, l])
        lat_m0 = rep_m(state_mlstm_m[:, l])
        lat_h0 = state_ssm[:, l].reshape(bs, 2 * H_S, HD, N_S)
        ck = cache_na_k[:, l].reshape(bs * past, D_NA).astype(BF16)
        cv = cache_na_v[:, l].reshape(bs * past, D_NA).astype(BF16)
        bias = _natten_bias(na_rpb[l], ss // GRID_W)
        xs, *_ = _layer(xs, mod[l, 1:1 + bs], prm, bs, ss, lat_caug0, lat_m0, lat_h0, ck, cv, bias)

    y_prompt = _final_norm(xp, final_g).reshape(bp, sp, D)
    y_sample = _final_norm(xs, final_g).reshape(bs, ss, D)
    return (y_prompt, y_sample, jnp.stack(ks, axis=1), jnp.stack(vs, axis=1), jnp.stack(cs, axis=1),
            jnp.stack(ns, axis=1), jnp.stack(ms, axis=1), jnp.stack(hs, axis=1))
```

```python
import functools

import numpy as np
import jax
import jax.numpy as jnp
from jax import lax
from jax.experimental import pallas as pl
from jax.experimental.pallas import tpu as pltpu

F32 = jnp.float32
BF16 = jnp.bfloat16
HI = lax.Precision.HIGHEST

D = 1024
DEPTH = 4
HD = 64
H_M = 4
D_M = H_M * HD
H_S = 6
D_S = H_S * HD
G_S = 2
R_S = H_S // G_S
N_S = 64
D_XBC = D_S + 2 * G_S * N_S
H_NA = 6
D_NA = H_NA * HD
GRID_W = 64
WIN_H = 8
WIN_W = 16
N_EXPERTS = 16
EC_FACTOR = 2
EPS = 1e-6
NEG_INIT = -1e30
TOK_BLK = 128
SLOT_BLK = 128
GATHER_TOK = 512
GATHER_SLOT = 256
GATE_W = 128
MI0, MF0, DT0 = 0, 2 * H_M, 4 * H_M
W_PACKED = 3 * D_M + D_M + D_S + D_XBC + 3 * D_NA + GATE_W

VMEM_LIMIT = 56 * 1024 * 1024


def _cparams(sem):
    return pltpu.CompilerParams(dimension_semantics=sem, vmem_limit_bytes=VMEM_LIMIT)


def _sigmoid(x):
    return 1.0 / (1.0 + jnp.exp(-x))


def _softplus(x):
    return jnp.maximum(x, 0.0) + jnp.log1p(jnp.exp(-jnp.abs(x)))


def _dot(a, b):
    return jnp.dot(a, b, preferred_element_type=F32)


def _dot_nt(a, b):
    return lax.dot_general(a, b, (((1,), (1,)), ((), ())), preferred_element_type=F32)


def _dot_tn(a, b):
    return lax.dot_general(a, b, (((0,), (0,)), ((), ())), preferred_element_type=F32)


def _mod_kernel(c_ref, w_ref, b_ref, o_ref):
    cv = c_ref[...]
    s = cv * _sigmoid(cv)
    o_ref[...] = jnp.dot(s, w_ref[...], precision=HI, preferred_element_type=F32) + b_ref[...]


def _modulation(cvec, ada_w, ada_b):
    tn = 1536
    return pl.pallas_call(
        _mod_kernel,
        grid=(DEPTH, 6 * D // tn),
        in_specs=[pl.BlockSpec((8, D), lambda l, j: (0, 0)),
                  pl.BlockSpec((None, D, tn), lambda l, j: (l, 0, j)),
                  pl.BlockSpec((None, 1, tn), lambda l, j: (l, 0, j))],
        out_specs=pl.BlockSpec((None, 8, tn), lambda l, j: (l, 0, j)),
        out_shape=jax.ShapeDtypeStruct((DEPTH, 8, 6 * D), F32),
        compiler_params=_cparams(("arbitrary", "arbitrary")),
    )(cvec, ada_w, ada_b.reshape(DEPTH, 1, 6 * D))


_IN_SEGS = (("qkv", 0, 768), ("mo", 768, 1024), ("sz", 1024, 1408), ("xbc", 1408, 2048),
            ("nq", 2048, 2432), ("nk", 2432, 2816), ("nv", 2816, 3200), ("gate", 3200, 3328))


def _inproj_kernel(x_ref, g_ref, sh_ref, sc_ref, w_ref, *outs):
    x = x_ref[...]
    ms = jnp.mean(x * x, axis=-1, keepdims=True)
    h = x * lax.rsqrt(ms + EPS) * g_ref[...]
    h = h * (1.0 + sc_ref[...]) + sh_ref[...]
    hb = h.astype(BF16)
    for o_ref, (_, lo, hi) in zip(outs, _IN_SEGS):
        o_ref[...] = _dot(hb, w_ref[:, lo:hi]).astype(o_ref.dtype)


def _in_projection(x, norm_g, shift, scale, w_packed, seq_len, attn_dtype):
    n = x.shape[0]
    tm = 512
    groups = shift.shape[0]
    if groups == 1:
        gidx = lambda i: (0, 0, 0)
    else:
        gidx = lambda i: ((i * tm) // seq_len, 0, 0)
    out_shape = tuple(jax.ShapeDtypeStruct((n, hi - lo), attn_dtype if name in ("nq", "nk", "nv") else F32)
                      for name, lo, hi in _IN_SEGS)
    out_specs = tuple(pl.BlockSpec((tm, hi - lo), lambda i: (i, 0)) for _, lo, hi in _IN_SEGS)
    return pl.pallas_call(
        _inproj_kernel,
        grid=(n // tm,),
        in_specs=[pl.BlockSpec((tm, D), lambda i: (i, 0)),
                  pl.BlockSpec((1, D), lambda i: (0, 0)),
                  pl.BlockSpec((None, 1, D), gidx),
                  pl.BlockSpec((None, 1, D), gidx),
                  pl.BlockSpec((D, W_PACKED), lambda i: (0, 0))],
        out_specs=out_specs,
        out_shape=out_shape,
        compiler_params=_cparams(("arbitrary",)),
    )(x, norm_g.reshape(1, D), shift, scale, w_packed)


def _conv_kernel(cur_ref, prev_ref, next_ref, w_ref, b_ref, o_ref, *, tiles_per_seq):
    i = pl.program_id(0)
    tt = cur_ref.shape[0]
    cur = cur_ref[...]
    first = (i % tiles_per_seq) == 0
    last = (i % tiles_per_seq) == tiles_per_seq - 1
    prow = jnp.where(first, 0.0, prev_ref[7:8, :])
    nrow = jnp.where(last, 0.0, next_ref[0:1, :])
    rows = lax.broadcasted_iota(jnp.int32, cur.shape, 0)
    prev = jnp.where(rows == 0, prow, pltpu.roll(cur, 1, 0))
    nxt = jnp.where(rows == tt - 1, nrow, pltpu.roll(cur, tt - 1, 0))
    y = prev * w_ref[0:1, :] + cur * w_ref[1:2, :] + nxt * w_ref[2:3, :] + b_ref[...]
    o_ref[...] = y * _sigmoid(y)


def _conv_silu(sxbc, conv_w, conv_b, seq_len):
    n = sxbc.shape[0]
    tt = min(seq_len, 512)
    tps = seq_len // tt
    nb8 = n // 8
    return pl.pallas_call(
        functools.partial(_conv_kernel, tiles_per_seq=tps),
        grid=(n // tt,),
        in_specs=[pl.BlockSpec((tt, D_XBC), lambda i: (i, 0)),
                  pl.BlockSpec((8, D_XBC), lambda i: (jnp.maximum(i * (tt // 8) - 1, 0), 0)),
                  pl.BlockSpec((8, D_XBC), lambda i: (jnp.minimum((i + 1) * (tt // 8), nb8 - 1), 0)),
                  pl.BlockSpec((3, D_XBC), lambda i: (0, 0)),
                  pl.BlockSpec((1, D_XBC), lambda i: (0, 0))],
        out_specs=pl.BlockSpec((tt, D_XBC), lambda i: (i, 0)),
        out_shape=jax.ShapeDtypeStruct((n, D_XBC), F32),
        compiler_params=_cparams(("arbitrary",)),
    )(sxbc, sxbc, sxbc, conv_w, conv_b.reshape(1, D_XBC))


def _tri_masks(ch):
    r = lax.broadcasted_iota(jnp.int32, (ch, ch), 0)
    c = lax.broadcasted_iota(jnp.int32, (ch, ch), 1)
    return r >= c, r <= c


def _mlstm_kernel(qkv_f, qkv_b, gt_f, gt_b, fb_ref, c0_ref, m0_ref,
                  hf_ref, hb_ref, cfin_ref, mfin_ref, c_scr, m_scr, it_scr, bt_scr, rc_scr, vt_scr,
                  kq_scr, in_scr, *, n_tiles, ch):
    j = pl.program_id(1)

    @pl.when(j == 0)
    def _():
        c_scr[...] = c0_ref[...]
        m_scr[...] = m0_ref[...]

    tt = qkv_f.shape[0]
    nch = tt // ch
    fb = fb_ref[...]
    low, upp = _tri_masks(ch)
    ones_row = (lax.broadcasted_iota(jnp.int32, (HD, ch), 0) == 0).astype(F32)
    pad_rows = jnp.zeros((ch - 2 * H_M, ch), F32)

    dc = [(d, c) for d in range(2) for c in range(nch)]
    i_rows, logf_rows, b_rows = {}, {}, {}
    for d, c in dc:
        z_t = ((gt_f, gt_b)[d][c * ch:(c + 1) * ch, :] + fb).T
        i_rows[d, c] = z_t[MI0:MI0 + 2 * H_M, :]
        zf = z_t[MF0:MF0 + 2 * H_M, :]
        logf_rows[d, c] = jnp.minimum(zf, 0.0) - jnp.log1p(jnp.exp(-jnp.abs(zf)))
    for d, c in dc:
        mask = (upp, low)[d]
        b_rows[d, c] = jnp.dot(logf_rows[d, c], mask.astype(F32), precision=HI, preferred_element_type=F32)
        it_scr[d, c] = i_rows[d, c]
        bt_scr[d, c] = b_rows[d, c]
    for d, c in dc:
        rc_scr[d, c] = jnp.concatenate([i_rows[d, c] - b_rows[d, c], pad_rows], axis=0).T
    for d, c in dc:
        for p in range(H_M // 2):
            vt_scr[d, c, p * 2 * HD:(p + 1) * 2 * HD, :] = (
                (qkv_f, qkv_b)[d][c * ch:(c + 1) * ch, 2 * D_M + p * 2 * HD:2 * D_M + (p + 1) * 2 * HD].T)

    def chunk(ci, carry):
        for d in range(2):
            qkv_ref = (qkv_f, qkv_b)[d]
            c0 = pl.multiple_of((ci if d == 0 else nch - 1 - ci) * ch, ch)
            for h in range(H_M):
                idx = d * H_M + h
                q = qkv_ref[pl.ds(c0, ch), h * HD:(h + 1) * HD].astype(BF16)
                k = (qkv_ref[pl.ds(c0, ch), D_M + h * HD:D_M + (h + 1) * HD] * (HD ** -0.5)).astype(BF16)
                kq_scr[idx] = _dot_nt(k, q)
                in_scr[idx] = _dot_nt(c_scr[idx].astype(BF16), q)
        for d in range(2):
            qkv_ref = (qkv_f, qkv_b)[d]
            out_ref = (hf_ref, hb_ref)[d]
            mask = (upp, low)[d]
            cidx = ci if d == 0 else nch - 1 - ci
            c0 = pl.multiple_of(cidx * ch, ch)
            i_t = it_scr[d, cidx]
            b_t = bt_scr[d, cidx]
            r_cols = rc_scr[d, cidx]
            edge = ch - 1 if d == 0 else 0
            h_t = []
            for h in range(H_M):
                idx = d * H_M + h
                vaug = jnp.concatenate([vt_scr[d, cidx, h * HD:(h + 1) * HD, :], ones_row], axis=0)
                k = (qkv_ref[pl.ds(c0, ch), D_M + h * HD:D_M + (h + 1) * HD] * (HD ** -0.5)).astype(BF16)
                b_row = b_t[idx:idx + 1, :]
                i_row = i_t[idx:idx + 1, :]
                m_prev = m_scr[idx][0:1, 0:1]
                caug = c_scr[idx]
                logd = jnp.where(mask, r_cols[:, idx:idx + 1] + b_row, -jnp.inf)
                m_inter = b_row + m_prev
                m_t = jnp.maximum(jnp.max(logd, axis=0, keepdims=True), m_inter)
                dmat = jnp.exp(logd - m_t)
                w_inter = jnp.exp(m_inter - m_t)
                s = (kq_scr[idx] * dmat).astype(BF16)
                tot = _dot(vaug.astype(BF16), s) + w_inter * in_scr[idx]
                den = jnp.maximum(jnp.abs(tot[HD:HD + 1, :]), jnp.exp(-m_t))
                h_t.append(tot[0:HD, :] / den)
                if h % 2 == 1:
                    pair = jnp.concatenate(h_t[-2:], axis=0).T
                    out_ref[pl.ds(c0, ch), (h - 1) * HD:(h + 1) * HD] = pair
                b_l = b_row[:, edge:edge + 1]
                lw = b_l - b_row + i_row
                m_new = jnp.maximum(b_l + m_prev, jnp.max(lw, axis=1, keepdims=True))
                wk = jnp.exp(lw - m_new)
                decay = jnp.exp(b_l + m_prev - m_new)
                c_scr[idx] = decay * caug + _dot((vaug * wk).astype(BF16), k)
                m_scr[idx] = jnp.broadcast_to(m_new, (8, 128))
        return carry

    lax.fori_loop(0, nch, chunk, 0)

    @pl.when(j == n_tiles - 1)
    def _():
        cfin_ref[...] = c_scr[...]
        mfin_ref[...] = m_scr[...]


def _mlstm_scan(qkv, gate, f_bias, caug0, m0, batch, seq_len):
    n = qkv.shape[0]
    tt = min(seq_len, 512)
    nt = seq_len // tt
    ch = 128
    fb = jnp.zeros((1, GATE_W), F32).at[0, MF0:MF0 + 2 * H_M].set(f_bias.reshape(-1))
    fwd = lambda b, j: (b * nt + j, 0)
    bwd = lambda b, j: (b * nt + nt - 1 - j, 0)
    st4 = lambda b, j: (b, 0, 0, 0)
    return pl.pallas_call(
        functools.partial(_mlstm_kernel, n_tiles=nt, ch=ch),
        grid=(batch, nt),
        in_specs=[pl.BlockSpec((tt, 3 * D_M), fwd), pl.BlockSpec((tt, 3 * D_M), bwd),
                  pl.BlockSpec((tt, GATE_W), fwd), pl.BlockSpec((tt, GATE_W), bwd),
                  pl.BlockSpec((1, GATE_W), lambda b, j: (0, 0)),
                  pl.BlockSpec((None, 2 * H_M, 128, HD), st4),
                  pl.BlockSpec((None, 2 * H_M, 8, 128), st4)],
        out_specs=(pl.BlockSpec((tt, D_M), fwd), pl.BlockSpec((tt, D_M), bwd),
                   pl.BlockSpec((None, 2 * H_M, 128, HD), st4),
                   pl.BlockSpec((None, 2 * H_M, 8, 128), st4)),
        out_shape=(jax.ShapeDtypeStruct((n, D_M), F32), jax.ShapeDtypeStruct((n, D_M), F32),
                   jax.ShapeDtypeStruct((batch, 2 * H_M, 128, HD), F32),
                   jax.ShapeDtypeStruct((batch, 2 * H_M, 8, 128), F32)),
        scratch_shapes=[pltpu.VMEM((2 * H_M, 128, HD), F32), pltpu.VMEM((2 * H_M, 8, 128), F32),
                        pltpu.VMEM((2, tt // ch, 2 * H_M, ch), F32), pltpu.VMEM((2, tt // ch, 2 * H_M, ch), F32),
                        pltpu.VMEM((2, tt // ch, ch, 128), F32), pltpu.VMEM((2, tt // ch, D_M, ch), F32),
                        pltpu.VMEM((2 * H_M, ch, ch), F32), pltpu.VMEM((2 * H_M, 128, ch), F32)],
        compiler_params=_cparams(("arbitrary", "arbitrary")),
    )(qkv, qkv, gate, gate, fb, caug0, m0)


def _ssd_kernel(xbc_f, xbc_b, gt_f, gt_b, dtb_ref, alog_ref, h0_ref,
                yf_ref, yb_ref, hfin_ref, h_scr, dtt_scr, at_scr, ac_scr, xt_scr, cb_scr, in_scr, *, n_tiles, ch):
    j = pl.program_id(1)

    @pl.when(j == 0)
    def _():
        h_scr[...] = h0_ref[...]

    tt = xbc_f.shape[0]
    nch = tt // ch
    dtb = dtb_ref[...]
    a_neg = -jnp.exp(alog_ref[...])
    low, upp = _tri_masks(ch)
    nrow = 16
    pad_rows = jnp.zeros((ch - nrow, ch), F32)

    dc = [(d, c) for d in range(2) for c in range(nch)]
    da_rows, a_rows = {}, {}
    for d, c in dc:
        dt = _softplus((gt_f, gt_b)[d][c * ch:(c + 1) * ch, :] + dtb)
        dtt_scr[d, c] = dt.T[DT0:DT0 + nrow, :]
        da_rows[d, c] = (dt * a_neg).T[DT0:DT0 + nrow, :]
    for d, c in dc:
        mask = (upp, low)[d]
        a_rows[d, c] = jnp.dot(da_rows[d, c], mask.astype(F32), precision=HI, preferred_element_type=F32)
        at_scr[d, c] = a_rows[d, c]
    for d, c in dc:
        ac_scr[d, c] = jnp.concatenate([a_rows[d, c], pad_rows], axis=0).T
    for d, c in dc:
        for p in range(H_S // 2):
            xt_scr[d, c, p * 2 * HD:(p + 1) * 2 * HD, :] = (
                (xbc_f, xbc_b)[d][c * ch:(c + 1) * ch, p * 2 * HD:(p + 1) * 2 * HD].T)

    def chunk(ci, carry):
        for d in range(2):
            x_ref = (xbc_f, xbc_b)[d]
            c0 = pl.multiple_of((ci if d == 0 else nch - 1 - ci) * ch, ch)
            for grp in range(G_S):
                bm = x_ref[pl.ds(c0, ch), D_S + grp * N_S:D_S + (grp + 1) * N_S].astype(BF16)
                cm = x_ref[pl.ds(c0, ch), D_S + G_S * N_S + grp * N_S:D_S + G_S * N_S + (grp + 1) * N_S].astype(BF16)
                cb_scr[d * G_S + grp] = _dot_nt(bm, cm)
                for r in range(R_S):
                    idx = d * H_S + grp * R_S + r
                    in_scr[idx] = _dot_nt(h_scr[idx].astype(BF16), cm)
        for d in range(2):
            x_ref = (xbc_f, xbc_b)[d]
            out_ref = (yf_ref, yb_ref)[d]
            mask = (upp, low)[d]
            cidx = ci if d == 0 else nch - 1 - ci
            c0 = pl.multiple_of(cidx * ch, ch)
            dt_t = dtt_scr[d, cidx]
            a_t = at_scr[d, cidx]
            a_cols = ac_scr[d, cidx]
            edge = ch - 1 if d == 0 else 0
            y_t = []
            for grp in range(G_S):
                bm = x_ref[pl.ds(c0, ch), D_S + grp * N_S:D_S + (grp + 1) * N_S].astype(BF16)
                for r in range(R_S):
                    hh = grp * R_S + r
                    idx = d * H_S + hh
                    a_row = a_t[idx:idx + 1, :]
                    dt_row = dt_t[idx:idx + 1, :]
                    seg = jnp.exp(jnp.where(mask, a_row - a_cols[:, idx:idx + 1], -jnp.inf))
                    mm = (cb_scr[d * G_S + grp] * seg).astype(BF16)
                    xh = xt_scr[d, cidx, hh * HD:(hh + 1) * HD, :]
                    h_st = h_scr[idx]
                    y_t.append(_dot((xh * dt_row).astype(BF16), mm) + in_scr[idx] * jnp.exp(a_row))
                    if hh % 2 == 1:
                        pair = jnp.concatenate(y_t[-2:], axis=0).T
                        out_ref[pl.ds(c0, ch), (hh - 1) * HD:(hh + 1) * HD] = pair
                    a_l = a_row[:, edge:edge + 1]
                    wk = jnp.exp(a_l - a_row) * dt_row
                    h_scr[idx] = jnp.exp(a_l) * h_st + _dot((xh * wk).astype(BF16), bm)
        return carry

    lax.fori_loop(0, nch, chunk, 0)

    @pl.when(j == n_tiles - 1)
    def _():
        hfin_ref[...] = h_scr[...]


def _ssd_scan(xbc, gate, dt_bias, a_log, h0_t, batch, seq_len):
    n = xbc.shape[0]
    tt = min(seq_len, 512)
    nt = seq_len // tt
    ch = 128
    dtb = jnp.zeros((1, GATE_W), F32).at[0, DT0:DT0 + 2 * H_S].set(dt_bias.reshape(-1))
    alog = jnp.zeros((1, GATE_W), F32).at[0, DT0:DT0 + 2 * H_S].set(a_log.reshape(-1))
    fwd = lambda b, j: (b * nt + j, 0)
    bwd = lambda b, j: (b * nt + nt - 1 - j, 0)
    st4 = lambda b, j: (b, 0, 0, 0)
    return pl.pallas_call(
        functools.partial(_ssd_kernel, n_tiles=nt, ch=ch),
        grid=(batch, nt),
        in_specs=[pl.BlockSpec((tt, D_XBC), fwd), pl.BlockSpec((tt, D_XBC), bwd),
                  pl.BlockSpec((tt, GATE_W), fwd), pl.BlockSpec((tt, GATE_W), bwd),
                  pl.BlockSpec((1, GATE_W), lambda b, j: (0, 0)),
                  pl.BlockSpec((1, GATE_W), lambda b, j: (0, 0)),
                  pl.BlockSpec((None, 2 * H_S, N_S, HD), st4)],
        out_specs=(pl.BlockSpec((tt, D_S), fwd), pl.BlockSpec((tt, D_S), bwd),
                   pl.BlockSpec((None, 2 * H_S, N_S, HD), st4)),
        out_shape=(jax.ShapeDtypeStruct((n, D_S), F32), jax.ShapeDtypeStruct((n, D_S), F32),
                   jax.ShapeDtypeStruct((batch, 2 * H_S, N_S, HD), F32)),
        scratch_shapes=[pltpu.VMEM((2 * H_S, N_S, HD), F32),
                        pltpu.VMEM((2, tt // ch, 16, ch), F32), pltpu.VMEM((2, tt // ch, 16, ch), F32),
                        pltpu.VMEM((2, tt // ch, ch, 128), F32), pltpu.VMEM((2, tt // ch, D_S, ch), F32),
                        pltpu.VMEM((2 * G_S, ch, ch), F32), pltpu.VMEM((2 * H_S, HD, ch), F32)],
        compiler_params=_cparams(("arbitrary", "arbitrary")),
    )(xbc, xbc, gate, gate, dtb, alog, h0_t)


def _ctx_attn_kernel(q_ref, k_ref, v_ref, o_ref):
    for h in range(H_NA):
        sl = slice(h * HD, (h + 1) * HD)
        q = (q_ref[:, sl] * (HD ** -0.5)).astype(BF16)
        s = _dot_nt(q, k_ref[:, sl].astype(BF16))
        m = jnp.max(s, axis=-1, keepdims=True)
        p = jnp.exp(s - m)
        l = jnp.sum(p, axis=-1, keepdims=True)
        o_ref[:, sl] = _dot(p.astype(BF16), v_ref[:, sl].astype(BF16)) / l


def _ctx_attention(q, k, v, batch, seq_len):
    spec = pl.BlockSpec((seq_len, D_NA), lambda b: (b, 0))
    return pl.pallas_call(
        _ctx_attn_kernel,
        grid=(batch,),
        in_specs=[spec, spec, spec],
        out_specs=spec,
        out_shape=jax.ShapeDtypeStruct(q.shape, F32),
        compiler_params=_cparams(("arbitrary",)),
    )(q, k, v)


def _row_start(r, rows):
    return jnp.clip(r - WIN_H // 2, 0, rows - WIN_H)


def _natten_kernel(q_ref, k_ref, v_ref, ck_ref, cv_ref, bias_ref, o_ref, sl_scr, sc_scr, *, rows):
    r = pl.program_id(1)
    k0 = pl.multiple_of(_row_start(r, rows) * GRID_W, GRID_W)
    nloc = WIN_H * GRID_W
    npast = ck_ref.shape[0]
    first = lax.broadcasted_iota(jnp.int32, (GRID_W, 2 * HD), 1) < HD
    ones_loc = jnp.ones((nloc, 2 * HD), BF16)
    ones_ctx = jnp.ones((npast, 2 * HD), BF16)
    for p in range(H_NA // 2):
        sl = slice(p * 2 * HD, (p + 1) * 2 * HD)
        qp = q_ref[:, sl]
        kw = k_ref[pl.ds(k0, nloc), sl].astype(BF16)
        ckp = ck_ref[:, sl].astype(BF16)
        for half in range(2):
            mine = first if half == 0 else jnp.logical_not(first)
            q = (jnp.where(mine, qp, jnp.zeros_like(qp)) * (HD ** -0.5)).astype(BF16)
            sl_scr[2 * p + half] = _dot_nt(q, kw) + bias_ref[2 * p + half]
            sc_scr[2 * p + half] = _dot_nt(q, ckp)
    for p in range(H_NA // 2):
        sl = slice(p * 2 * HD, (p + 1) * 2 * HD)
        vw = v_ref[pl.ds(k0, nloc), sl].astype(BF16)
        cvp = cv_ref[:, sl].astype(BF16)
        outs = []
        for half in range(2):
            s_loc = sl_scr[2 * p + half]
            s_ctx = sc_scr[2 * p + half]
            m = jnp.maximum(jnp.max(s_loc, axis=-1, keepdims=True), jnp.max(s_ctx, axis=-1, keepdims=True))
            p_loc = jnp.exp(s_loc - m).astype(BF16)
            p_ctx = jnp.exp(s_ctx - m).astype(BF16)
            l = _dot(p_loc, ones_loc) + _dot(p_ctx, ones_ctx)
            outs.append((_dot(p_loc, vw) + _dot(p_ctx, cvp)) / l)
        o_ref[:, sl] = jnp.where(first, outs[0], outs[1])


def _natten_bias(rpb, rows):
    qc = np.arange(GRID_W)[:, None]
    kc = np.arange(GRID_W)[None, :]
    cstart = np.clip(qc - WIN_W // 2, 0, GRID_W - WIN_W)
    ok = (kc >= cstart) & (kc < cstart + WIN_W)
    col_off = np.clip(kc - qc + WIN_W - 1, 0, 2 * WIN_W - 2)
    t = jnp.where(jnp.asarray(ok)[None, None], rpb[:, :, col_off], -jnp.inf)
    var = np.arange(WIN_H)[:, None] + np.arange(WIN_H)[None, :]
    b = t[:, var]
    return b.transpose(0, 1, 3, 2, 4).reshape(H_NA, WIN_H, GRID_W, WIN_H * GRID_W).astype(F32)


def _neighbourhood_attention(q, k, v, ck, cv, bias, batch, seq_len):
    rows = seq_len // GRID_W
    past = ck.shape[0] // batch
    bvar = lambda b, r: (0, _row_start(r, rows) - r + WIN_H - 1, 0, 0)
    return pl.pallas_call(
        functools.partial(_natten_kernel, rows=rows),
        grid=(batch, rows),
        in_specs=[pl.BlockSpec((GRID_W, D_NA), lambda b, r: (b * rows + r, 0)),
                  pl.BlockSpec((seq_len, D_NA), lambda b, r: (b, 0)),
                  pl.BlockSpec((seq_len, D_NA), lambda b, r: (b, 0)),
                  pl.BlockSpec((past, D_NA), lambda b, r: (b, 0)),
                  pl.BlockSpec((past, D_NA), lambda b, r: (b, 0)),
                  pl.BlockSpec((H_NA, None, GRID_W, WIN_H * GRID_W), bvar)],
        out_specs=pl.BlockSpec((GRID_W, D_NA), lambda b, r: (b * rows + r, 0)),
        out_shape=jax.ShapeDtypeStruct(q.shape, F32),
        scratch_shapes=[pltpu.VMEM((H_NA, GRID_W, WIN_H * GRID_W), F32), pltpu.VMEM((H_NA, GRID_W, past), F32)],
        compiler_params=_cparams(("arbitrary", "arbitrary")),
    )(q, k, v, ck, cv, bias)


def _outproj_kernel(x_ref, hf_ref, hb_ref, mo_ref, yf_ref, yb_ref, xbc_ref, sz_ref, a_ref,
                    mg_ref, dsk_ref, sg_ref, bd_ref, w_ref, g1_ref, n2_ref, sh2_ref, sc2_ref, rw_ref,
                    xo_ref, h2_ref, aff_ref):
    hm = hf_ref[...] + hb_ref[...]
    ssq = jnp.dot(hm * hm, bd_ref[...], precision=HI, preferred_element_type=F32)
    m_out = hm * lax.rsqrt(ssq * (1.0 / HD) + EPS) * mg_ref[...] * _sigmoid(mo_ref[...])
    sz = sz_ref[...]
    ys = (yf_ref[...] + yb_ref[...] + dsk_ref[...] * xbc_ref[:, 0:D_S]) * (sz * _sigmoid(sz))
    s_out = ys * lax.rsqrt(jnp.mean(ys * ys, axis=-1, keepdims=True) + EPS) * sg_ref[...]
    y = (_dot(m_out.astype(BF16), w_ref[0:D_M, :])
         + _dot(s_out.astype(BF16), w_ref[D_M:D_M + D_S, :])
         + _dot(a_ref[...].astype(BF16), w_ref[D_M + D_S:D, :]))
    xn = x_ref[...] + g1_ref[...] * y
    xo_ref[...] = xn
    h2 = xn * lax.rsqrt(jnp.mean(xn * xn, axis=-1, keepdims=True) + EPS) * n2_ref[...]
    h2 = h2 * (1.0 + sc2_ref[...]) + sh2_ref[...]
    h2_ref[...] = h2.T.astype(BF16)
    logits = lax.dot_general(rw_ref[...], h2, (((1,), (1,)), ((), ())), precision=HI,
                             preferred_element_type=F32)
    mx = jnp.max(logits, axis=0, keepdims=True)
    ex = jnp.exp(logits - mx)
    aff_ref[...] = ex / jnp.sum(ex, axis=0, keepdims=True)


def _out_projection(x, hf, hb, mo, yf, yb, xbc, sz, a_out, mnorm_g, d_skip, snorm_g, w_out_bf,
                    g1, norm2_g, sh2, sc2, router_wt, seq_len):
    n = x.shape[0]
    tm = GATHER_TOK
    groups = g1.shape[0]
    if groups == 1:
        gidx = lambda i: (0, 0, 0)
    else:
        gidx = lambda i: ((i * tm) // seq_len, 0, 0)
    row = lambda w: pl.BlockSpec((tm, w), lambda i: (i, 0))
    const = lambda s: pl.BlockSpec(s, lambda i: (0,) * len(s))
    hid = np.arange(D_M) // HD
    blockdiag = jnp.asarray((hid[:, None] == hid[None, :]).astype(np.float32))
    return pl.pallas_call(
        _outproj_kernel,
        grid=(n // tm,),
        in_specs=[row(D), row(D_M), row(D_M), row(D_M), row(D_S), row(D_S), row(D_XBC), row(D_S), row(D_NA),
                  const((1, D_M)), const((1, D_S)), const((1, D_S)), const((D_M, D_M)), const((D, D)),
                  pl.BlockSpec((None, 1, D), gidx), const((1, D)),
                  pl.BlockSpec((None, 1, D), gidx), pl.BlockSpec((None, 1, D), gidx),
                  const((N_EXPERTS, D))],
        out_specs=(row(D), pl.BlockSpec((None, D, tm), lambda i: (i, 0, 0)),
                   pl.BlockSpec((N_EXPERTS, tm), lambda i: (0, i))),
        out_shape=(jax.ShapeDtypeStruct((n, D), F32), jax.ShapeDtypeStruct((n // tm, D, tm), BF16),
                   jax.ShapeDtypeStruct((N_EXPERTS, n), F32)),
        compiler_params=_cparams(("arbitrary",)),
    )(x, hf, hb, mo, yf, yb, xbc, sz, a_out,
      mnorm_g.reshape(1, D_M), jnp.repeat(d_skip, HD).reshape(1, D_S), snorm_g.reshape(1, D_S), blockdiag,
      w_out_bf, g1, norm2_g.reshape(1, D), sh2, sc2, router_wt)


def _select_kernel(aff_ref, gate_ref, slot_ref, start_ref, *, cap):
    aff = aff_ref[...]
    bits = pltpu.bitcast(aff, jnp.int32)
    n_tok = aff.shape[1]

    def step(i, prefix):
        cand = prefix | (jnp.int32(1) << (30 - i))
        cnt = jnp.sum((bits >= cand).astype(F32), axis=1, keepdims=True)
        return jnp.where(cnt >= cap, cand, prefix)

    thr = lax.fori_loop(0, 31, step, jnp.zeros((aff.shape[0], 1), jnp.int32))
    gt = bits > thr
    eq = bits == thr
    need = cap - jnp.sum(gt.astype(F32), axis=1, keepdims=True)
    r_i = lax.broadcasted_iota(jnp.int32, (128, 128), 0)
    c_i = lax.broadcasted_iota(jnp.int32, (128, 128), 1)
    strict = (r_i < c_i).astype(BF16)
    run_eq = jnp.zeros((aff.shape[0], 1), F32)
    run_sel = jnp.zeros((aff.shape[0], 1), F32)
    lane = lax.broadcasted_iota(jnp.int32, (aff.shape[0], 128), 1)
    starts = jnp.zeros((aff.shape[0], 128), F32)
    for blk in range(n_tok // TOK_BLK):
        sl = slice(blk * TOK_BLK, (blk + 1) * TOK_BLK)
        e = eq[:, sl]
        rank = _dot(e.astype(BF16), strict) + run_eq
        keep = gt[:, sl] | (e & (rank < need))
        kf = jnp.where(keep, 1.0, 0.0)
        slot = _dot(kf.astype(BF16), strict) + run_sel
        gate_ref[:, sl] = jnp.where(keep, aff[:, sl], 0.0)
        slot_ref[:, sl] = jnp.where(keep, slot, -1.0).astype(jnp.int32)
        starts = jnp.where(lane == blk, run_sel, starts)
        run_eq = run_eq + jnp.sum(e.astype(F32), axis=1, keepdims=True)
        run_sel = run_sel + jnp.sum(kf, axis=1, keepdims=True)
    start_ref[...] = starts.astype(jnp.int32)


def _select(aff_t):
    n = aff_t.shape[1]
    cap = EC_FACTOR * n // N_EXPERTS
    assert n // TOK_BLK <= 128
    return pl.pallas_call(
        functools.partial(_select_kernel, cap=float(cap)),
        out_shape=(jax.ShapeDtypeStruct(aff_t.shape, F32), jax.ShapeDtypeStruct(aff_t.shape, jnp.int32),
                   jax.ShapeDtypeStruct((N_EXPERTS, 128), jnp.int32)),
        compiler_params=pltpu.CompilerParams(vmem_limit_bytes=VMEM_LIMIT),
    )(aff_t)


def _expert_kernel(lo_ref, hi_ref, slot_ref, h2t_ref, w1_ref, w3_ref, w2_ref, ye_ref, xe_scr, acc_scr, *, cap):
    e = pl.program_id(0)
    nsb = cap // GATHER_SLOT
    slot_iota = lax.broadcasted_iota(jnp.int32, (GATHER_SLOT, GATHER_TOK), 0)
    for sb in range(nsb):
        acc_scr[...] = jnp.zeros_like(acc_scr)
        want = slot_iota + sb * GATHER_SLOT

        def body(tb, carry):
            t0 = pl.multiple_of(tb * GATHER_TOK, GATHER_TOK)
            onehot = jnp.where(slot_ref[:, pl.ds(t0, GATHER_TOK)] == want, 1.0, 0.0).astype(BF16)
            acc_scr[...] += _dot_nt(h2t_ref[tb], onehot)
            return carry

        lax.fori_loop(lo_ref[e * nsb + sb], hi_ref[e * nsb + sb] + 1, body, 0)
        xe_scr[sb * GATHER_SLOT:(sb + 1) * GATHER_SLOT, :] = acc_scr[...].T.astype(BF16)
    half = cap // 2
    for r in range(2):
        x = xe_scr[r * half:(r + 1) * half, :]
        a = _dot(x, w1_ref[...])
        b = _dot(x, w3_ref[...])
        hid = (a * _sigmoid(a) * b).astype(BF16)
        ye_ref[r * half:(r + 1) * half, :] = _dot(hid, w2_ref[...]).astype(BF16)


def _combine_kernel(win_ref, slot_ref, gate_ref, x_ref, g2_ref, ye_ref, o_ref, acc_scr):
    tb = pl.program_id(0)
    lane = lax.broadcasted_iota(jnp.int32, (TOK_BLK, 2 * SLOT_BLK), 1)
    fill = jnp.zeros((TOK_BLK - N_EXPERTS, TOK_BLK), F32)
    slot_t = jnp.concatenate([slot_ref[...].astype(F32), fill], axis=0).T
    gate_t = jnp.concatenate([gate_ref[...], fill], axis=0).T
    acc_scr[...] = jnp.zeros_like(acc_scr)
    for e in range(N_EXPERTS):
        s0 = pl.multiple_of(win_ref[tb * N_EXPERTS + e] * SLOT_BLK, SLOT_BLK)
        want = (lane + s0).astype(F32)
        onehot = jnp.where(slot_t[:, e:e + 1] == want, 1.0, 0.0).astype(BF16)
        acc_scr[...] += _dot(onehot, ye_ref[e, pl.ds(s0, 2 * SLOT_BLK), :]) * gate_t[:, e:e + 1]
    o_ref[...] = x_ref[...] + g2_ref[...] * acc_scr[...]


def _moe(h2, aff_t, x, g2, w1, w3, w2, layer, seq_len):
    n = x.shape[0]
    cap = EC_FACTOR * n // N_EXPERTS
    nsb = cap // SLOT_BLK
    ntb = n // TOK_BLK
    gate, slot, starts = _select(aff_t)

    cs = starts[:, :ntb]
    win = jnp.minimum(cs // SLOT_BLK, nsb - 2).T.astype(jnp.int32)
    gs = cs[:, ::GATHER_TOK // TOK_BLK]
    gend = jnp.concatenate([gs[:, 1:], jnp.full((N_EXPERTS, 1), cap, jnp.int32)], axis=1)
    edges = jnp.arange(cap // GATHER_SLOT, dtype=jnp.int32) * GATHER_SLOT
    lo = jnp.sum(gend[:, None, :] <= edges[None, :, None], axis=-1).astype(jnp.int32)
    hi = jnp.sum(gs[:, None, :] < (edges + GATHER_SLOT)[None, :, None], axis=-1).astype(jnp.int32) - 1

    wspec = pl.BlockSpec((None, None, D, D), lambda e, *_: (layer, e, 0, 0))
    ye = pl.pallas_call(
        functools.partial(_expert_kernel, cap=cap),
        grid_spec=pltpu.PrefetchScalarGridSpec(
            num_scalar_prefetch=2,
            grid=(N_EXPERTS,),
            in_specs=[pl.BlockSpec((None, 1, n), lambda e, *_: (e, 0, 0)),
                      pl.BlockSpec((n // GATHER_TOK, D, GATHER_TOK), lambda e, *_: (0, 0, 0),
                                   pipeline_mode=pl.Buffered(1)),
                      wspec, wspec, wspec],
            out_specs=pl.BlockSpec((None, cap, D), lambda e, *_: (e, 0, 0)),
            scratch_shapes=[pltpu.VMEM((cap, D), BF16), pltpu.VMEM((D, GATHER_SLOT), F32)]),
        out_shape=jax.ShapeDtypeStruct((N_EXPERTS, cap, D), BF16),
        compiler_params=_cparams(("arbitrary",)),
    )(lo.reshape(-1), hi.reshape(-1), slot.reshape(N_EXPERTS, 1, n), h2, w1, w3, w2)

    groups = g2.shape[0]
    if groups == 1:
        gidx = lambda i, *_: (0, 0, 0)
    else:
        gidx = lambda i, *_: ((i * TOK_BLK) // seq_len, 0, 0)
    return pl.pallas_call(
        _combine_kernel,
        grid_spec=pltpu.PrefetchScalarGridSpec(
            num_scalar_prefetch=1,
            grid=(ntb,),
            in_specs=[pl.BlockSpec((N_EXPERTS, TOK_BLK), lambda i, *_: (0, i)),
                      pl.BlockSpec((N_EXPERTS, TOK_BLK), lambda i, *_: (0, i)),
                      pl.BlockSpec((TOK_BLK, D), lambda i, *_: (i, 0)),
                      pl.BlockSpec((None, 1, D), gidx),
                      pl.BlockSpec((N_EXPERTS, cap, D), lambda i, *_: (0, 0, 0), pipeline_mode=pl.Buffered(1))],
            out_specs=pl.BlockSpec((TOK_BLK, D), lambda i, *_: (i, 0)),
            scratch_shapes=[pltpu.VMEM((TOK_BLK, D), F32)]),
        out_shape=jax.ShapeDtypeStruct((n, D), F32),
        compiler_params=_cparams(("arbitrary",)),
    )(win.reshape(-1), slot, gate, x, g2, ye)


def _final_kernel(x_ref, g_ref, o_ref):
    x = x_ref[...]
    o_ref[...] = x * lax.rsqrt(jnp.mean(x * x, axis=-1, keepdims=True) + EPS) * g_ref[...]


def _final_norm(x, g):
    n = x.shape[0]
    tm = 1024
    return pl.pallas_call(
        _final_kernel,
        grid=(n // tm,),
        in_specs=[pl.BlockSpec((tm, D), lambda i: (i, 0)), pl.BlockSpec((1, D), lambda i: (0, 0))],
        out_specs=pl.BlockSpec((tm, D), lambda i: (i, 0)),
        out_shape=jax.ShapeDtypeStruct((n, D), F32),
        compiler_params=_cparams(("arbitrary",)),
    )(x, g.reshape(1, D))


def _layer(x, mod, prm, batch, seq_len, caug0, m0, h0_t, ck=None, cv=None, bias=None):
    sh1, sc1, g1, sh2, sc2, g2 = [m[:, None, :] for m in jnp.split(mod, 6, axis=-1)]
    attn_dtype = F32 if ck is None else BF16
    qkv, mo, sz, sxbc, nq, nk, nv, gate = _in_projection(x, prm["norm1_g"], sh1, sc1, prm["w_in"], seq_len, attn_dtype)
    xbc = _conv_silu(sxbc, prm["conv_w"], prm["conv_b"], seq_len)
    hf, hb, cfin, mfin = _mlstm_scan(qkv, gate, prm["f_bias"], caug0, m0, batch, seq_len)
    yf, yb, hfin = _ssd_scan(xbc, gate, prm["dt_bias"], prm["a_log"], h0_t, batch, seq_len)
    if ck is None:
        a_out = _ctx_attention(nq, nk, nv, batch, seq_len)
    else:
        a_out = _neighbourhood_attention(nq, nk, nv, ck, cv, bias, batch, seq_len)
    xn, h2, aff_t = _out_projection(x, hf, hb, mo, yf, yb, xbc, sz, a_out, prm["mnorm_g"], prm["d_skip"],
                                    prm["snorm_g"], prm["w_out"], g1, prm["norm2_g"], sh2, sc2,
                                    prm["router_wt"], seq_len)
    xo = _moe(h2, aff_t, xn, g2, prm["w1"], prm["w3"], prm["w2"], prm["layer"], seq_len)
    return xo, nk, nv, cfin, mfin, hfin


def _pack_w_in(w_in):
    o = np.cumsum((0, D_M, D_M, D_M, D_M, 2 * H_M, 2 * H_M, D_S, D_XBC, 2 * H_S, D_NA, D_NA, D_NA))
    pad = jnp.zeros(w_in.shape[:-1] + (GATE_W - 4 * H_M - 2 * H_S,), w_in.dtype)
    parts = [w_in[..., o[0]:o[4]], w_in[..., o[6]:o[8]], w_in[..., o[9]:o[12]],
             w_in[..., o[4]:o[6]], w_in[..., o[8]:o[9]], pad]
    return jnp.concatenate(parts, axis=-1).astype(BF16)


def kernel(x_prompt, x_sample, cache_na_k, cache_na_v, state_mlstm_c, state_mlstm_n, state_mlstm_m, state_ssm, c, c_ctx, ada_w, ada_b, norm1_g, norm2_g, w_in, mlstm_f_bias, mlstm_norm_g, conv_w, conv_b, ssm_dt_bias, ssm_a_log, ssm_d, ssm_norm_g, na_rpb, w_out, router_w, exp_w1, exp_w3, exp_w2, final_g):
    bp, sp, _ = x_prompt.shape
    bs, ss, _ = x_sample.shape
    past = cache_na_k.shape[2]

    cvec = jnp.zeros((8, D), F32).at[0].set(c_ctx).at[1:1 + bs].set(c)
    mod = _modulation(cvec, ada_w, ada_b)

    w_in_p = _pack_w_in(w_in)
    w_out_bf = w_out.astype(BF16)
    w1_bf, w3_bf, w2_bf = exp_w1.astype(BF16), exp_w3.astype(BF16), exp_w2.astype(BF16)
    router_wt = jnp.swapaxes(router_w, 1, 2)

    def aug(cs, ns):
        b = cs.shape[0]
        pad = jnp.zeros(cs.shape[:-2] + (128 - HD - 1, HD), F32)
        return jnp.concatenate([jnp.swapaxes(cs, -1, -2), ns[..., None, :], pad], axis=-2).reshape(b, 2 * H_M, 128, HD)

    def rep_m(ms):
        b = ms.shape[0]
        return jnp.broadcast_to(ms.reshape(b, 2 * H_M, 1, 1), (b, 2 * H_M, 8, 128)).astype(F32)

    ctx_caug0 = jnp.zeros((bp, 2 * H_M, 128, HD), F32)
    ctx_m0 = jnp.full((bp, 2 * H_M, 8, 128), NEG_INIT, F32)
    ctx_h0 = jnp.zeros((bp, 2 * H_S, N_S, HD), F32)

    xp = x_prompt.reshape(bp * sp, D)
    xs = x_sample.reshape(bs * ss, D)
    ks, vs, cs, ns, ms, hs = [], [], [], [], [], []
    for l in range(DEPTH):
        prm = dict(norm1_g=norm1_g[l], norm2_g=norm2_g[l], w_in=w_in_p[l], f_bias=mlstm_f_bias[l],
                   mnorm_g=mlstm_norm_g[l], conv_w=conv_w[l], conv_b=conv_b[l], dt_bias=ssm_dt_bias[l],
                   a_log=ssm_a_log[l], d_skip=ssm_d[l], snorm_g=ssm_norm_g[l], w_out=w_out_bf[l],
                   router_wt=router_wt[l], w1=w1_bf, w3=w3_bf, w2=w2_bf, layer=l)
        xp, nk, nv, cfin, mfin, hfin = _layer(xp, mod[l, 0:1], prm, bp, sp, ctx_caug0, ctx_m0, ctx_h0)
        ks.append(nk.reshape(bp, sp, H_NA, HD))
        vs.append(nv.reshape(bp, sp, H_NA, HD))
        cs.append(jnp.swapaxes(cfin[..., 0:HD, :], -1, -2).reshape(bp, 2, H_M, HD, HD))
        ns.append(cfin[..., HD, :].reshape(bp, 2, H_M, HD))
        ms.append(mfin[..., 0, 0].reshape(bp, 2, H_M))
        hs.append(hfin.reshape(bp, 2, H_S, HD, N_S))

        lat_caug0 = aug(state_mlstm_c[:, l], state_mlstm_n[:, l])
        lat_m0 = rep_m(state_mlstm_m[:, l])
        lat_h0 = state_ssm[:, l].reshape(bs, 2 * H_S, HD, N_S)
        ck = cache_na_k[:, l].reshape(bs * past, D_NA).astype(BF16)
        cv = cache_na_v[:, l].reshape(bs * past, D_NA).astype(BF16)
        bias = _natten_bias(na_rpb[l], ss // GRID_W)
        xs, *_ = _layer(xs, mod[l, 1:1 + bs], prm, bs, ss, lat_caug0, lat_m0, lat_h0, ck, cv, bias)

    y_prompt = _final_norm(xp, final_g).reshape(bp, sp, D)
    y_sample = _final_norm(xs, final_g).reshape(bs, ss, D)
    return (y_prompt, y_sample, jnp.stack(ks, axis=1), jnp.stack(vs, axis=1), jnp.stack(cs, axis=1),
            jnp.stack(ns, axis=1), jnp.stack(ms, axis=1), jnp.stack(hs, axis=1))
```

```python
import functools

import numpy as np
import jax
import jax.numpy as jnp
from jax import lax
from jax.experimental import pallas as pl
from jax.experimental.pallas import tpu as pltpu

F32 = jnp.float32
BF16 = jnp.bfloat16
HI = lax.Precision.HIGHEST

D = 1024
DEPTH = 4
HD = 64
H_M = 4
D_M = H_M * HD
H_S = 6
D_S = H_S * HD
G_S = 2
R_S = H_S // G_S
N_S = 64
D_XBC = D_S + 2 * G_S * N_S
H_NA = 6
D_NA = H_NA * HD
GRID_W = 64
WIN_H = 8
WIN_W = 16
N_EXPERTS = 16
EC_FACTOR = 2
EPS = 1e-6
NEG_INIT = -1e30
TOK_BLK = 128
SLOT_BLK = 128
GATHER_TOK = 512
GATHER_SLOT = 256
GATHER_WIN = 6
GATE_W = 128
MI0, MF0, DT0 = 0, 2 * H_M, 4 * H_M
W_PACKED = 3 * D_M + D_M + D_S + D_XBC + 3 * D_NA + GATE_W

VMEM_LIMIT = 56 * 1024 * 1024


def _cparams(sem):
    return pltpu.CompilerParams(dimension_semantics=sem, vmem_limit_bytes=VMEM_LIMIT)


def _sigmoid(x):
    return 1.0 / (1.0 + jnp.exp(-x))


def _softplus(x):
    return jnp.maximum(x, 0.0) + jnp.log1p(jnp.exp(-jnp.abs(x)))


def _dot(a, b):
    return jnp.dot(a, b, preferred_element_type=F32)


def _dot_nt(a, b):
    return lax.dot_general(a, b, (((1,), (1,)), ((), ())), preferred_element_type=F32)


def _dot_tn(a, b):
    return lax.dot_general(a, b, (((0,), (0,)), ((), ())), preferred_element_type=F32)


def _mod_kernel(c_ref, w_ref, b_ref, o_ref):
    cv = c_ref[...]
    s = cv * _sigmoid(cv)
    o_ref[...] = jnp.dot(s, w_ref[...], precision=HI, preferred_element_type=F32) + b_ref[...]


def _modulation(cvec, ada_w, ada_b):
    tn = 1536
    return pl.pallas_call(
        _mod_kernel,
        grid=(DEPTH, 6 * D // tn),
        in_specs=[pl.BlockSpec((8, D), lambda l, j: (0, 0)),
                  pl.BlockSpec((None, D, tn), lambda l, j: (l, 0, j)),
                  pl.BlockSpec((None, 1, tn), lambda l, j: (l, 0, j))],
        out_specs=pl.BlockSpec((None, 8, tn), lambda l, j: (l, 0, j)),
        out_shape=jax.ShapeDtypeStruct((DEPTH, 8, 6 * D), F32),
        compiler_params=_cparams(("arbitrary", "arbitrary")),
    )(cvec, ada_w, ada_b.reshape(DEPTH, 1, 6 * D))


_IN_SEGS = (("qkv", 0, 768), ("mo", 768, 1024), ("sz", 1024, 1408), ("xbc", 1408, 2048),
            ("nq", 2048, 2432), ("nk", 2432, 2816), ("nv", 2816, 3200), ("gate", 3200, 3328))


def _inproj_kernel(x_ref, g_ref, sh_ref, sc_ref, w_ref, *rest, n_alias, n_seg):
    outs = rest[n_alias:]
    x = x_ref[...]
    ms = jnp.mean(x * x, axis=-1, keepdims=True)
    h = x * lax.rsqrt(ms + EPS) * g_ref[...]
    h = h * (1.0 + sc_ref[...]) + sh_ref[...]
    hb = h.astype(BF16)
    vals = {}
    for o_ref, (name, lo, hi) in zip(outs[:n_seg], _IN_SEGS):
        vals[name] = _dot(hb, w_ref[:, lo:hi])
        o_ref[...] = vals[name].astype(o_ref.dtype)
    for o_ref, name in zip(outs[n_seg:], ("nk", "nv")):
        o_ref[...] = vals[name].reshape(o_ref.shape)


def _in_projection(x, norm_g, shift, scale, w_packed, seq_len, attn_dtype, kv_stack=None):
    n = x.shape[0]
    tm = 512
    groups = shift.shape[0]
    if groups == 1:
        gidx = lambda i: (0, 0, 0)
    else:
        gidx = lambda i: ((i * tm) // seq_len, 0, 0)
    out_shape = [jax.ShapeDtypeStruct((n, hi - lo), attn_dtype if name in ("nq", "nk", "nv") else F32)
                 for name, lo, hi in _IN_SEGS]
    out_specs = [pl.BlockSpec((tm, hi - lo), lambda i: (i, 0)) for _, lo, hi in _IN_SEGS]
    in_specs = [pl.BlockSpec((tm, D), lambda i: (i, 0)),
                pl.BlockSpec((1, D), lambda i: (0, 0)),
                pl.BlockSpec((None, 1, D), gidx),
                pl.BlockSpec((None, 1, D), gidx),
                pl.BlockSpec((D, W_PACKED), lambda i: (0, 0))]
    args = [x, norm_g.reshape(1, D), shift, scale, w_packed]
    aliases = {}
    n_alias = 0
    if kv_stack is not None:
        layer, depth, k_prev, v_prev = kv_stack
        bpt = tm // seq_len
        stack_shape = jax.ShapeDtypeStruct((n // seq_len, depth, seq_len, D_NA), F32)
        for prev in (k_prev, v_prev):
            out_shape.append(stack_shape)
            out_specs.append(pl.BlockSpec((bpt, None, seq_len, D_NA), lambda i: (i, layer, 0, 0)))
            if prev is not None:
                aliases[len(args)] = len(out_shape) - 1
                in_specs.append(pl.BlockSpec(memory_space=pl.ANY))
                args.append(prev)
                n_alias += 1
    return pl.pallas_call(
        functools.partial(_inproj_kernel, n_alias=n_alias, n_seg=len(_IN_SEGS)),
        grid=(n // tm,),
        in_specs=in_specs,
        out_specs=tuple(out_specs),
        out_shape=tuple(out_shape),
        input_output_aliases=aliases,
        compiler_params=_cparams(("arbitrary",)),
    )(*args)


def _conv_kernel(cur_ref, prev_ref, next_ref, w_ref, b_ref, o_ref, *, tiles_per_seq):
    i = pl.program_id(0)
    tt = cur_ref.shape[0]
    cur = cur_ref[...]
    first = (i % tiles_per_seq) == 0
    last = (i % tiles_per_seq) == tiles_per_seq - 1
    prow = jnp.where(first, 0.0, prev_ref[7:8, :])
    nrow = jnp.where(last, 0.0, next_ref[0:1, :])
    rows = lax.broadcasted_iota(jnp.int32, cur.shape, 0)
    prev = jnp.where(rows == 0, prow, pltpu.roll(cur, 1, 0))
    nxt = jnp.where(rows == tt - 1, nrow, pltpu.roll(cur, tt - 1, 0))
    y = prev * w_ref[0:1, :] + cur * w_ref[1:2, :] + nxt * w_ref[2:3, :] + b_ref[...]
    o_ref[...] = y * _sigmoid(y)


def _conv_silu(sxbc, conv_w, conv_b, seq_len):
    n = sxbc.shape[0]
    tt = min(seq_len, 512)
    tps = seq_len // tt
    nb8 = n // 8
    return pl.pallas_call(
        functools.partial(_conv_kernel, tiles_per_seq=tps),
        grid=(n // tt,),
        in_specs=[pl.BlockSpec((tt, D_XBC), lambda i: (i, 0)),
                  pl.BlockSpec((8, D_XBC), lambda i: (jnp.maximum(i * (tt // 8) - 1, 0), 0)),
                  pl.BlockSpec((8, D_XBC), lambda i: (jnp.minimum((i + 1) * (tt // 8), nb8 - 1), 0)),
                  pl.BlockSpec((3, D_XBC), lambda i: (0, 0)),
                  pl.BlockSpec((1, D_XBC), lambda i: (0, 0))],
        out_specs=pl.BlockSpec((tt, D_XBC), lambda i: (i, 0)),
        out_shape=jax.ShapeDtypeStruct((n, D_XBC), F32),
        compiler_params=_cparams(("arbitrary",)),
    )(sxbc, sxbc, sxbc, conv_w, conv_b.reshape(1, D_XBC))


def _tri_masks(ch):
    r = lax.broadcasted_iota(jnp.int32, (ch, ch), 0)
    c = lax.broadcasted_iota(jnp.int32, (ch, ch), 1)
    return r >= c, r <= c


def _mlstm_kernel(qkv_f, qkv_b, gt_f, gt_b, fb_ref, c0_ref, m0_ref,
                  hf_ref, hb_ref, cfin_ref, mfin_ref, c_scr, m_scr, it_scr, bt_scr, rc_scr, vt_scr,
                  kq_scr, in_scr, *, n_tiles, ch):
    j = pl.program_id(1)

    @pl.when(j == 0)
    def _():
        c_scr[...] = c0_ref[...]
        m_scr[...] = m0_ref[...]

    tt = qkv_f.shape[0]
    nch = tt // ch
    fb = fb_ref[...]
    low, upp = _tri_masks(ch)
    ones_row = (lax.broadcasted_iota(jnp.int32, (HD, ch), 0) == 0).astype(F32)
    pad_rows = jnp.zeros((ch - 2 * H_M, ch), F32)

    dc = [(d, c) for d in range(2) for c in range(nch)]
    i_rows, logf_rows, b_rows = {}, {}, {}
    for d, c in dc:
        z_t = ((gt_f, gt_b)[d][c * ch:(c + 1) * ch, :] + fb).T
        i_rows[d, c] = z_t[MI0:MI0 + 2 * H_M, :]
        zf = z_t[MF0:MF0 + 2 * H_M, :]
        logf_rows[d, c] = jnp.minimum(zf, 0.0) - jnp.log1p(jnp.exp(-jnp.abs(zf)))
    for d, c in dc:
        mask = (upp, low)[d]
        b_rows[d, c] = jnp.dot(logf_rows[d, c], mask.astype(F32), precision=HI, preferred_element_type=F32)
        it_scr[d, c] = i_rows[d, c]
        bt_scr[d, c] = b_rows[d, c]
    for d, c in dc:
        rc_scr[d, c] = jnp.concatenate([i_rows[d, c] - b_rows[d, c], pad_rows], axis=0).T
    for d, c in dc:
        for p in range(H_M // 2):
            vt_scr[d, c, p * 2 * HD:(p + 1) * 2 * HD, :] = (
                (qkv_f, qkv_b)[d][c * ch:(c + 1) * ch, 2 * D_M + p * 2 * HD:2 * D_M + (p + 1) * 2 * HD].T)

    def chunk(ci, carry):
        for d in range(2):
            qkv_ref = (qkv_f, qkv_b)[d]
            c0 = pl.multiple_of((ci if d == 0 else nch - 1 - ci) * ch, ch)
            for h in range(H_M):
                idx = d * H_M + h
                q = qkv_ref[pl.ds(c0, ch), h * HD:(h + 1) * HD].astype(BF16)
                k = (qkv_ref[pl.ds(c0, ch), D_M + h * HD:D_M + (h + 1) * HD] * (HD ** -0.5)).astype(BF16)
                kq_scr[idx] = _dot_nt(k, q)
                in_scr[idx] = _dot_nt(c_scr[idx].astype(BF16), q)
        for d in range(2):
            qkv_ref = (qkv_f, qkv_b)[d]
            out_ref = (hf_ref, hb_ref)[d]
            mask = (upp, low)[d]
            cidx = ci if d == 0 else nch - 1 - ci
            c0 = pl.multiple_of(cidx * ch, ch)
            i_t = it_scr[d, cidx]
            b_t = bt_scr[d, cidx]
            r_cols = rc_scr[d, cidx]
            edge = ch - 1 if d == 0 else 0
            h_t = []
            for h in range(H_M):
                idx = d * H_M + h
                vaug = jnp.concatenate([vt_scr[d, cidx, h * HD:(h + 1) * HD, :], ones_row], axis=0)
                k = (qkv_ref[pl.ds(c0, ch), D_M + h * HD:D_M + (h + 1) * HD] * (HD ** -0.5)).astype(BF16)
                b_row = b_t[idx:idx + 1, :]
                i_row = i_t[idx:idx + 1, :]
                m_prev = m_scr[idx][0:1, 0:1]
                caug = c_scr[idx]
                logd = jnp.where(mask, r_cols[:, idx:idx + 1] + b_row, -jnp.inf)
                m_inter = b_row + m_prev
                m_t = jnp.maximum(jnp.max(logd, axis=0, keepdims=True), m_inter)
                dmat = jnp.exp(logd - m_t)
                w_inter = jnp.exp(m_inter - m_t)
                s = (kq_scr[idx] * dmat).astype(BF16)
                tot = _dot(vaug.astype(BF16), s) + w_inter * in_scr[idx]
                den = jnp.maximum(jnp.abs(tot[HD:HD + 1, :]), jnp.exp(-m_t))
                h_t.append(tot[0:HD, :] / den)
                if h % 2 == 1:
                    pair = jnp.concatenate(h_t[-2:], axis=0).T
                    out_ref[pl.ds(c0, ch), (h - 1) * HD:(h + 1) * HD] = pair
                b_l = b_row[:, edge:edge + 1]
                lw = b_l - b_row + i_row
                m_new = jnp.maximum(b_l + m_prev, jnp.max(lw, axis=1, keepdims=True))
                wk = jnp.exp(lw - m_new)
                decay = jnp.exp(b_l + m_prev - m_new)
                c_scr[idx] = decay * caug + _dot((vaug * wk).astype(BF16), k)
                m_scr[idx] = jnp.broadcast_to(m_new, (8, 128))
        return carry

    lax.fori_loop(0, nch, chunk, 0)

    @pl.when(j == n_tiles - 1)
    def _():
        cfin_ref[...] = c_scr[...]
        mfin_ref[...] = m_scr[...]


def _mlstm_scan(qkv, gate, f_bias, caug0, m0, batch, seq_len):
    n = qkv.shape[0]
    tt = min(seq_len, 512)
    nt = seq_len // tt
    ch = 128
    fb = jnp.zeros((1, GATE_W), F32).at[0, MF0:MF0 + 2 * H_M].set(f_bias.reshape(-1))
    fwd = lambda b, j: (b * nt + j, 0)
    bwd = lambda b, j: (b * nt + nt - 1 - j, 0)
    st4 = lambda b, j: (b, 0, 0, 0)
    return pl.pallas_call(
        functools.partial(_mlstm_kernel, n_tiles=nt, ch=ch),
        grid=(batch, nt),
        in_specs=[pl.BlockSpec((tt, 3 * D_M), fwd), pl.BlockSpec((tt, 3 * D_M), bwd),
                  pl.BlockSpec((tt, GATE_W), fwd), pl.BlockSpec((tt, GATE_W), bwd),
                  pl.BlockSpec((1, GATE_W), lambda b, j: (0, 0)),
                  pl.BlockSpec((None, 2 * H_M, 128, HD), st4),
                  pl.BlockSpec((None, 2 * H_M, 8, 128), st4)],
        out_specs=(pl.BlockSpec((tt, D_M), fwd), pl.BlockSpec((tt, D_M), bwd),
                   pl.BlockSpec((None, 2 * H_M, 128, HD), st4),
                   pl.BlockSpec((None, 2 * H_M, 8, 128), st4)),
        out_shape=(jax.ShapeDtypeStruct((n, D_M), F32), jax.ShapeDtypeStruct((n, D_M), F32),
                   jax.ShapeDtypeStruct((batch, 2 * H_M, 128, HD), F32),
                   jax.ShapeDtypeStruct((batch, 2 * H_M, 8, 128), F32)),
        scratch_shapes=[pltpu.VMEM((2 * H_M, 128, HD), F32), pltpu.VMEM((2 * H_M, 8, 128), F32),
                        pltpu.VMEM((2, tt // ch, 2 * H_M, ch), F32), pltpu.VMEM((2, tt // ch, 2 * H_M, ch), F32),
                        pltpu.VMEM((2, tt // ch, ch, 128), F32), pltpu.VMEM((2, tt // ch, D_M, ch), F32),
                        pltpu.VMEM((2 * H_M, ch, ch), F32), pltpu.VMEM((2 * H_M, 128, ch), F32)],
        compiler_params=_cparams(("arbitrary", "arbitrary")),
    )(qkv, qkv, gate, gate, fb, caug0, m0)


def _ssd_kernel(xbc_f, xbc_b, gt_f, gt_b, dtb_ref, alog_ref, h0_ref,
                yf_ref, yb_ref, hfin_ref, h_scr, dtt_scr, at_scr, ac_scr, xt_scr, cb_scr, in_scr, *, n_tiles, ch):
    j = pl.program_id(1)

    @pl.when(j == 0)
    def _():
        h_scr[...] = h0_ref[...]

    tt = xbc_f.shape[0]
    nch = tt // ch
    dtb = dtb_ref[...]
    a_neg = -jnp.exp(alog_ref[...])
    low, upp = _tri_masks(ch)
    nrow = 16
    pad_rows = jnp.zeros((ch - nrow, ch), F32)

    dc = [(d, c) for d in range(2) for c in range(nch)]
    da_rows, a_rows = {}, {}
    for d, c in dc:
        dt = _softplus((gt_f, gt_b)[d][c * ch:(c + 1) * ch, :] + dtb)
        dtt_scr[d, c] = dt.T[DT0:DT0 + nrow, :]
        da_rows[d, c] = (dt * a_neg).T[DT0:DT0 + nrow, :]
    for d, c in dc:
        mask = (upp, low)[d]
        a_rows[d, c] = jnp.dot(da_rows[d, c], mask.astype(F32), precision=HI, preferred_element_type=F32)
        at_scr[d, c] = a_rows[d, c]
    for d, c in dc:
        ac_scr[d, c] = jnp.concatenate([a_rows[d, c], pad_rows], axis=0).T
    for d, c in dc:
        for p in range(H_S // 2):
            xt_scr[d, c, p * 2 * HD:(p + 1) * 2 * HD, :] = (
                (xbc_f, xbc_b)[d][c * ch:(c + 1) * ch, p * 2 * HD:(p + 1) * 2 * HD].T)

    def chunk(ci, carry):
        for d in range(2):
            x_ref = (xbc_f, xbc_b)[d]
            c0 = pl.multiple_of((ci if d == 0 else nch - 1 - ci) * ch, ch)
            for grp in range(G_S):
                bm = x_ref[pl.ds(c0, ch), D_S + grp * N_S:D_S + (grp + 1) * N_S].astype(BF16)
                cm = x_ref[pl.ds(c0, ch), D_S + G_S * N_S + grp * N_S:D_S + G_S * N_S + (grp + 1) * N_S].astype(BF16)
                cb_scr[d * G_S + grp] = _dot_nt(bm, cm)
                for r in range(R_S):
                    idx = d * H_S + grp * R_S + r
                    in_scr[idx] = _dot_nt(h_scr[idx].astype(BF16), cm)
        for d in range(2):
            x_ref = (xbc_f, xbc_b)[d]
            out_ref = (yf_ref, yb_ref)[d]
            mask = (upp, low)[d]
            cidx = ci if d == 0 else nch - 1 - ci
            c0 = pl.multiple_of(cidx * ch, ch)
            dt_t = dtt_scr[d, cidx]
            a_t = at_scr[d, cidx]
            a_cols = ac_scr[d, cidx]
            edge = ch - 1 if d == 0 else 0
            y_t = []
            for grp in range(G_S):
                bm = x_ref[pl.ds(c0, ch), D_S + grp * N_S:D_S + (grp + 1) * N_S].astype(BF16)
                for r in range(R_S):
                    hh = grp * R_S + r
                    idx = d * H_S + hh
                    a_row = a_t[idx:idx + 1, :]
                    dt_row = dt_t[idx:idx + 1, :]
                    seg = jnp.exp(jnp.where(mask, a_row - a_cols[:, idx:idx + 1], -jnp.inf))
                    mm = (cb_scr[d * G_S + grp] * seg).astype(BF16)
                    xh = xt_scr[d, cidx, hh * HD:(hh + 1) * HD, :]
                    h_st = h_scr[idx]
                    y_t.append(_dot((xh * dt_row).astype(BF16), mm) + in_scr[idx] * jnp.exp(a_row))
                    if hh % 2 == 1:
                        pair = jnp.concatenate(y_t[-2:], axis=0).T
                        out_ref[pl.ds(c0, ch), (hh - 1) * HD:(hh + 1) * HD] = pair
                    a_l = a_row[:, edge:edge + 1]
                    wk = jnp.exp(a_l - a_row) * dt_row
                    h_scr[idx] = jnp.exp(a_l) * h_st + _dot((xh * wk).astype(BF16), bm)
        return carry

    lax.fori_loop(0, nch, chunk, 0)

    @pl.when(j == n_tiles - 1)
    def _():
        hfin_ref[...] = h_scr[...]


def _ssd_scan(xbc, gate, dt_bias, a_log, h0_t, batch, seq_len):
    n = xbc.shape[0]
    tt = min(seq_len, 512)
    nt = seq_len // tt
    ch = 128
    dtb = jnp.zeros((1, GATE_W), F32).at[0, DT0:DT0 + 2 * H_S].set(dt_bias.reshape(-1))
    alog = jnp.zeros((1, GATE_W), F32).at[0, DT0:DT0 + 2 * H_S].set(a_log.reshape(-1))
    fwd = lambda b, j: (b * nt + j, 0)
    bwd = lambda b, j: (b * nt + nt - 1 - j, 0)
    st4 = lambda b, j: (b, 0, 0, 0)
    return pl.pallas_call(
        functools.partial(_ssd_kernel, n_tiles=nt, ch=ch),
        grid=(batch, nt),
        in_specs=[pl.BlockSpec((tt, D_XBC), fwd), pl.BlockSpec((tt, D_XBC), bwd),
                  pl.BlockSpec((tt, GATE_W), fwd), pl.BlockSpec((tt, GATE_W), bwd),
                  pl.BlockSpec((1, GATE_W), lambda b, j: (0, 0)),
                  pl.BlockSpec((1, GATE_W), lambda b, j: (0, 0)),
                  pl.BlockSpec((None, 2 * H_S, N_S, HD), st4)],
        out_specs=(pl.BlockSpec((tt, D_S), fwd), pl.BlockSpec((tt, D_S), bwd),
                   pl.BlockSpec((None, 2 * H_S, N_S, HD), st4)),
        out_shape=(jax.ShapeDtypeStruct((n, D_S), F32), jax.ShapeDtypeStruct((n, D_S), F32),
                   jax.ShapeDtypeStruct((batch, 2 * H_S, N_S, HD), F32)),
        scratch_shapes=[pltpu.VMEM((2 * H_S, N_S, HD), F32),
                        pltpu.VMEM((2, tt // ch, 16, ch), F32), pltpu.VMEM((2, tt // ch, 16, ch), F32),
                        pltpu.VMEM((2, tt // ch, ch, 128), F32), pltpu.VMEM((2, tt // ch, D_S, ch), F32),
                        pltpu.VMEM((2 * G_S, ch, ch), F32), pltpu.VMEM((2 * H_S, HD, ch), F32)],
        compiler_params=_cparams(("arbitrary", "arbitrary")),
    )(xbc, xbc, gate, gate, dtb, alog, h0_t)


def _ctx_attn_kernel(q_ref, k_ref, v_ref, o_ref):
    for h in range(H_NA):
        sl = slice(h * HD, (h + 1) * HD)
        q = (q_ref[:, sl] * (HD ** -0.5)).astype(BF16)
        s = _dot_nt(q, k_ref[:, sl].astype(BF16))
        m = jnp.max(s, axis=-1, keepdims=True)
        p = jnp.exp(s - m)
        l = jnp.sum(p, axis=-1, keepdims=True)
        o_ref[:, sl] = _dot(p.astype(BF16), v_ref[:, sl].astype(BF16)) / l


def _ctx_attention(q, k, v, batch, seq_len):
    spec = pl.BlockSpec((seq_len, D_NA), lambda b: (b, 0))
    return pl.pallas_call(
        _ctx_attn_kernel,
        grid=(batch,),
        in_specs=[spec, spec, spec],
        out_specs=spec,
        out_shape=jax.ShapeDtypeStruct(q.shape, F32),
        compiler_params=_cparams(("arbitrary",)),
    )(q, k, v)


def _row_start(r, rows):
    return jnp.clip(r - WIN_H // 2, 0, rows - WIN_H)


def _natten_kernel(q_ref, k_ref, v_ref, ck_ref, cv_ref, bias_ref, o_ref, sl_scr, sc_scr, *, rows):
    r = pl.program_id(1)
    k0 = pl.multiple_of(_row_start(r, rows) * GRID_W, GRID_W)
    off0 = _row_start(r, rows) - r + WIN_H - 1
    nloc = WIN_H * GRID_W
    npast = ck_ref.shape[0]
    first = lax.broadcasted_iota(jnp.int32, (GRID_W, 2 * HD), 1) < HD
    ones_loc = jnp.ones((nloc, 2 * HD), BF16)
    ones_ctx = jnp.ones((npast, 2 * HD), BF16)
    for p in range(H_NA // 2):
        sl = slice(p * 2 * HD, (p + 1) * 2 * HD)
        qp = q_ref[:, sl]
        kw = k_ref[pl.ds(k0, nloc), sl].astype(BF16)
        ckp = ck_ref[:, sl].astype(BF16)
        for half in range(2):
            mine = first if half == 0 else jnp.logical_not(first)
            q = (jnp.where(mine, qp, jnp.zeros_like(qp)) * (HD ** -0.5)).astype(BF16)
            bias = jnp.concatenate([bias_ref[2 * p + half, off0 + 2 * i] for i in range(WIN_H // 2)], axis=1)
            sl_scr[2 * p + half] = _dot_nt(q, kw) + bias
            sc_scr[2 * p + half] = _dot_nt(q, ckp)
    for p in range(H_NA // 2):
        sl = slice(p * 2 * HD, (p + 1) * 2 * HD)
        vw = v_ref[pl.ds(k0, nloc), sl].astype(BF16)
        cvp = cv_ref[:, sl].astype(BF16)
        outs = []
        for half in range(2):
            s_loc = sl_scr[2 * p + half]
            s_ctx = sc_scr[2 * p + half]
            m = jnp.maximum(jnp.max(s_loc, axis=-1, keepdims=True), jnp.max(s_ctx, axis=-1, keepdims=True))
            p_loc = jnp.exp(s_loc - m).astype(BF16)
            p_ctx = jnp.exp(s_ctx - m).astype(BF16)
            l = _dot(p_loc, ones_loc) + _dot(p_ctx, ones_ctx)
            outs.append((_dot(p_loc, vw) + _dot(p_ctx, cvp)) / l)
        o_ref[:, sl] = jnp.where(first, outs[0], outs[1])


def _natten_bias(rpb, rows):
    qc = np.arange(GRID_W)[:, None]
    kc = np.arange(GRID_W)[None, :]
    cstart = np.clip(qc - WIN_W // 2, 0, GRID_W - WIN_W)
    ok = (kc >= cstart) & (kc < cstart + WIN_W)
    col_off = np.clip(kc - qc + WIN_W - 1, 0, 2 * WIN_W - 2)
    t = jnp.where(jnp.asarray(ok)[None, None], rpb[:, :, col_off], -jnp.inf).astype(F32)
    return jnp.concatenate([t[:, :-1], t[:, 1:]], axis=-1)


def _neighbourhood_attention(q, k, v, ck, cv, bias, batch, seq_len):
    rows = seq_len // GRID_W
    past = ck.shape[0] // batch
    return pl.pallas_call(
        functools.partial(_natten_kernel, rows=rows),
        grid=(batch, rows),
        in_specs=[pl.BlockSpec((GRID_W, D_NA), lambda b, r: (b * rows + r, 0)),
                  pl.BlockSpec((seq_len, D_NA), lambda b, r: (b, 0)),
                  pl.BlockSpec((seq_len, D_NA), lambda b, r: (b, 0)),
                  pl.BlockSpec((past, D_NA), lambda b, r: (b, 0)),
                  pl.BlockSpec((past, D_NA), lambda b, r: (b, 0)),
                  pl.BlockSpec((H_NA, 2 * WIN_H - 2, GRID_W, 2 * GRID_W), lambda b, r: (0, 0, 0, 0))],
        out_specs=pl.BlockSpec((GRID_W, D_NA), lambda b, r: (b * rows + r, 0)),
        out_shape=jax.ShapeDtypeStruct(q.shape, F32),
        scratch_shapes=[pltpu.VMEM((H_NA, GRID_W, WIN_H * GRID_W), F32), pltpu.VMEM((H_NA, GRID_W, past), F32)],
        compiler_params=_cparams(("arbitrary", "arbitrary")),
    )(q, k, v, ck, cv, bias)


def _outproj_kernel(x_ref, hf_ref, hb_ref, mo_ref, yf_ref, yb_ref, xbc_ref, sz_ref, a_ref,
                    mg_ref, dsk_ref, sg_ref, bd_ref, w_ref, g1_ref, n2_ref, sh2_ref, sc2_ref, rw_ref,
                    xo_ref, h2_ref, aff_ref):
    hm = hf_ref[...] + hb_ref[...]
    ssq = jnp.dot(hm * hm, bd_ref[...], precision=HI, preferred_element_type=F32)
    m_out = hm * lax.rsqrt(ssq * (1.0 / HD) + EPS) * mg_ref[...] * _sigmoid(mo_ref[...])
    sz = sz_ref[...]
    ys = (yf_ref[...] + yb_ref[...] + dsk_ref[...] * xbc_ref[:, 0:D_S]) * (sz * _sigmoid(sz))
    s_out = ys * lax.rsqrt(jnp.mean(ys * ys, axis=-1, keepdims=True) + EPS) * sg_ref[...]
    y = (_dot(m_out.astype(BF16), w_ref[0:D_M, :])
         + _dot(s_out.astype(BF16), w_ref[D_M:D_M + D_S, :])
         + _dot(a_ref[...].astype(BF16), w_ref[D_M + D_S:D, :]))
    xn = x_ref[...] + g1_ref[...] * y
    xo_ref[...] = xn
    h2 = xn * lax.rsqrt(jnp.mean(xn * xn, axis=-1, keepdims=True) + EPS) * n2_ref[...]
    h2 = h2 * (1.0 + sc2_ref[...]) + sh2_ref[...]
    h2_ref[...] = h2.T.astype(BF16)
    logits = lax.dot_general(rw_ref[...], h2, (((1,), (1,)), ((), ())), precision=HI,
                             preferred_element_type=F32)
    mx = jnp.max(logits, axis=0, keepdims=True)
    ex = jnp.exp(logits - mx)
    aff_ref[...] = ex / jnp.sum(ex, axis=0, keepdims=True)


def _out_projection(x, hf, hb, mo, yf, yb, xbc, sz, a_out, mnorm_g, d_skip, snorm_g, w_out_bf,
                    g1, norm2_g, sh2, sc2, router_wt, seq_len):
    n = x.shape[0]
    tm = GATHER_TOK
    groups = g1.shape[0]
    if groups == 1:
        gidx = lambda i: (0, 0, 0)
    else:
        gidx = lambda i: ((i * tm) // seq_len, 0, 0)
    row = lambda w: pl.BlockSpec((tm, w), lambda i: (i, 0))
    const = lambda s: pl.BlockSpec(s, lambda i: (0,) * len(s))
    hid = np.arange(D_M) // HD
    blockdiag = jnp.asarray((hid[:, None] == hid[None, :]).astype(np.float32))
    return pl.pallas_call(
        _outproj_kernel,
        grid=(n // tm,),
        in_specs=[row(D), row(D_M), row(D_M), row(D_M), row(D_S), row(D_S), row(D_XBC), row(D_S), row(D_NA),
                  const((1, D_M)), const((1, D_S)), const((1, D_S)), const((D_M, D_M)), const((D, D)),
                  pl.BlockSpec((None, 1, D), gidx), const((1, D)),
                  pl.BlockSpec((None, 1, D), gidx), pl.BlockSpec((None, 1, D), gidx),
                  const((N_EXPERTS, D))],
        out_specs=(row(D), pl.BlockSpec((None, D, tm), lambda i: (i, 0, 0)),
                   pl.BlockSpec((N_EXPERTS, tm), lambda i: (0, i))),
        out_shape=(jax.ShapeDtypeStruct((n, D), F32), jax.ShapeDtypeStruct((n // tm, D, tm), BF16),
                   jax.ShapeDtypeStruct((N_EXPERTS, n), F32)),
        compiler_params=_cparams(("arbitrary",)),
    )(x, hf, hb, mo, yf, yb, xbc, sz, a_out,
      mnorm_g.reshape(1, D_M), jnp.repeat(d_skip, HD).reshape(1, D_S), snorm_g.reshape(1, D_S), blockdiag,
      w_out_bf, g1, norm2_g.reshape(1, D), sh2, sc2, router_wt)


def _select_kernel(aff_ref, gate_ref, slot_ref, start_ref, *, cap):
    aff = aff_ref[...]
    bits = pltpu.bitcast(aff, jnp.int32)
    n_tok = aff.shape[1]

    def step(i, prefix):
        cand = prefix | (jnp.int32(1) << (30 - i))
        cnt = jnp.sum((bits >= cand).astype(F32), axis=1, keepdims=True)
        return jnp.where(cnt >= cap, cand, prefix)

    thr = lax.fori_loop(0, 31, step, jnp.zeros((aff.shape[0], 1), jnp.int32))
    gt = bits > thr
    eq = bits == thr
    need = cap - jnp.sum(gt.astype(F32), axis=1, keepdims=True)
    r_i = lax.broadcasted_iota(jnp.int32, (128, 128), 0)
    c_i = lax.broadcasted_iota(jnp.int32, (128, 128), 1)
    strict = (r_i < c_i).astype(BF16)
    run_eq = jnp.zeros((aff.shape[0], 1), F32)
    run_sel = jnp.zeros((aff.shape[0], 1), F32)
    lane = lax.broadcasted_iota(jnp.int32, (aff.shape[0], 128), 1)
    starts = jnp.zeros((aff.shape[0], 128), F32)
    for blk in range(n_tok // TOK_BLK):
        sl = slice(blk * TOK_BLK, (blk + 1) * TOK_BLK)
        e = eq[:, sl]
        rank = _dot(e.astype(BF16), strict) + run_eq
        keep = gt[:, sl] | (e & (rank < need))
        kf = jnp.where(keep, 1.0, 0.0)
        slot = _dot(kf.astype(BF16), strict) + run_sel
        gate_ref[:, sl] = jnp.where(keep, aff[:, sl], 0.0)
        slot_ref[:, sl] = jnp.where(keep, slot, -1.0).astype(jnp.int32)
        starts = jnp.where(lane == blk, run_sel, starts)
        run_eq = run_eq + jnp.sum(e.astype(F32), axis=1, keepdims=True)
        run_sel = run_sel + jnp.sum(kf, axis=1, keepdims=True)
    start_ref[...] = starts.astype(jnp.int32)


def _select(aff_t):
    n = aff_t.shape[1]
    cap = EC_FACTOR * n // N_EXPERTS
    assert n // TOK_BLK <= 128
    return pl.pallas_call(
        functools.partial(_select_kernel, cap=float(cap)),
        out_shape=(jax.ShapeDtypeStruct(aff_t.shape, F32), jax.ShapeDtypeStruct(aff_t.shape, jnp.int32),
                   jax.ShapeDtypeStruct((N_EXPERTS, 128), jnp.int32)),
        compiler_params=pltpu.CompilerParams(vmem_limit_bytes=VMEM_LIMIT),
    )(aff_t)


def _gather_rows(e, lo_ref, hi_ref, slot_ref, h2t_ref, xe_scr, acc_scr, col_scr, cap):
    nsb = cap // GATHER_SLOT
    nchunk = h2t_ref.shape[0]
    fill = jnp.zeros((TOK_BLK - 8, GATHER_TOK), F32)
    for c in range(nchunk):
        row = jnp.broadcast_to(slot_ref[:, c * GATHER_TOK:(c + 1) * GATHER_TOK].astype(F32), (8, GATHER_TOK))
        col_scr[c] = jnp.concatenate([row, fill], axis=0).T
    lane = lax.broadcasted_iota(jnp.int32, (GATHER_TOK, GATHER_SLOT), 1).astype(F32)
    win = min(GATHER_WIN, nchunk)

    def onehot_t(c, want):
        return jnp.where(col_scr[c][:, 0:1] == want, 1.0, 0.0).astype(BF16)

    for sb in range(nsb):
        want = lane + float(sb * GATHER_SLOT)
        first = jnp.minimum(lo_ref[e * nsb + sb], nchunk - win)
        acc = _dot(h2t_ref[first], onehot_t(first, want))
        for i in range(1, win):
            acc = acc + _dot(h2t_ref[first + i], onehot_t(first + i, want))
        acc_scr[...] = acc

        def body(c, carry):
            acc_scr[...] += _dot(h2t_ref[c], onehot_t(c, want))
            return carry

        lax.fori_loop(first + win, hi_ref[e * nsb + sb] + 1, body, 0)
        xe_scr[sb * GATHER_SLOT:(sb + 1) * GATHER_SLOT, :] = acc_scr[...].T.astype(BF16)


def _expert_kernel(lo_ref, hi_ref, slot_ref, h2t_ref, w1_ref, w3_ref, w2_ref, ye_ref,
                   xe_scr, acc_scr, col_scr, y_scr, *, cap):
    e = pl.program_id(0)
    f = pl.program_id(1)

    @pl.when(f == 0)
    def _():
        _gather_rows(e, lo_ref, hi_ref, slot_ref, h2t_ref, xe_scr, acc_scr, col_scr, cap)

    x = xe_scr[...]
    a = _dot(x, w1_ref[...].astype(BF16))
    b = _dot(x, w3_ref[...].astype(BF16))
    hid = (a * _sigmoid(a) * b).astype(BF16)
    part = _dot(hid, w2_ref[...].astype(BF16))

    @pl.when(f == 0)
    def _():
        y_scr[...] = part

    @pl.when(f == 1)
    def _():
        ye_ref[...] = (y_scr[...] + part).astype(BF16)


def _combine_kernel(win_ref, slot_ref, gate_ref, x_ref, g2_ref, ye_ref, o_ref, acc_scr):
    tb = pl.program_id(0)
    lane = lax.broadcasted_iota(jnp.int32, (TOK_BLK, 2 * SLOT_BLK), 1)
    fill = jnp.zeros((TOK_BLK - N_EXPERTS, TOK_BLK), F32)
    slot_t = jnp.concatenate([slot_ref[...].astype(F32), fill], axis=0).T
    gate_t = jnp.concatenate([gate_ref[...], fill], axis=0).T
    acc_scr[...] = jnp.zeros_like(acc_scr)
    for e in range(N_EXPERTS):
        s0 = pl.multiple_of(win_ref[tb * N_EXPERTS + e] * SLOT_BLK, SLOT_BLK)
        want = (lane + s0).astype(F32)
        onehot = jnp.where(slot_t[:, e:e + 1] == want, 1.0, 0.0).astype(BF16)
        acc_scr[...] += _dot(onehot, ye_ref[e, pl.ds(s0, 2 * SLOT_BLK), :]) * gate_t[:, e:e + 1]
    o_ref[...] = x_ref[...] + g2_ref[...] * acc_scr[...]


def _moe(h2, aff_t, x, g2, w1, w3, w2, layer, seq_len):
    n = x.shape[0]
    cap = EC_FACTOR * n // N_EXPERTS
    nsb = cap // SLOT_BLK
    ntb = n // TOK_BLK
    gate, slot, starts = _select(aff_t)

    cs = starts[:, :ntb]
    win = jnp.minimum(cs // SLOT_BLK, nsb - 2).T.astype(jnp.int32)
    gs = cs[:, ::GATHER_TOK // TOK_BLK]
    gend = jnp.concatenate([gs[:, 1:], jnp.full((N_EXPERTS, 1), cap, jnp.int32)], axis=1)
    edges = jnp.arange(cap // GATHER_SLOT, dtype=jnp.int32) * GATHER_SLOT
    lo = jnp.sum(gend[:, None, :] <= edges[None, :, None], axis=-1).astype(jnp.int32)
    hi = jnp.sum(gs[:, None, :] < (edges + GATHER_SLOT)[None, :, None], axis=-1).astype(jnp.int32) - 1

    d_ff = w1.shape[-1]
    up_spec = pl.BlockSpec((None, None, D, d_ff // 2), lambda e, f, *_: (layer, e, 0, f))
    down_spec = pl.BlockSpec((None, None, d_ff // 2, D), lambda e, f, *_: (layer, e, f, 0))
    ye = pl.pallas_call(
        functools.partial(_expert_kernel, cap=cap),
        grid_spec=pltpu.PrefetchScalarGridSpec(
            num_scalar_prefetch=2,
            grid=(N_EXPERTS, 2),
            in_specs=[pl.BlockSpec((None, 1, n), lambda e, f, *_: (e, 0, 0)),
                      pl.BlockSpec((n // GATHER_TOK, D, GATHER_TOK), lambda e, f, *_: (0, 0, 0),
                                   pipeline_mode=pl.Buffered(1)),
                      up_spec, up_spec, down_spec],
            out_specs=pl.BlockSpec((None, cap, D), lambda e, f, *_: (e, 0, 0)),
            scratch_shapes=[pltpu.VMEM((cap, D), BF16), pltpu.VMEM((D, GATHER_SLOT), F32),
                            pltpu.VMEM((n // GATHER_TOK, GATHER_TOK, TOK_BLK), F32),
                            pltpu.VMEM((cap, D), F32)]),
        out_shape=jax.ShapeDtypeStruct((N_EXPERTS, cap, D), BF16),
        compiler_params=_cparams(("arbitrary", "arbitrary")),
    )(lo.reshape(-1), hi.reshape(-1), slot.reshape(N_EXPERTS, 1, n), h2, w1, w3, w2)

    groups = g2.shape[0]
    if groups == 1:
        gidx = lambda i, *_: (0, 0, 0)
    else:
        gidx = lambda i, *_: ((i * TOK_BLK) // seq_len, 0, 0)
    return pl.pallas_call(
        _combine_kernel,
        grid_spec=pltpu.PrefetchScalarGridSpec(
            num_scalar_prefetch=1,
            grid=(ntb,),
            in_specs=[pl.BlockSpec((N_EXPERTS, TOK_BLK), lambda i, *_: (0, i)),
                      pl.BlockSpec((N_EXPERTS, TOK_BLK), lambda i, *_: (0, i)),
                      pl.BlockSpec((TOK_BLK, D), lambda i, *_: (i, 0)),
                      pl.BlockSpec((None, 1, D), gidx),
                      pl.BlockSpec((N_EXPERTS, cap, D), lambda i, *_: (0, 0, 0), pipeline_mode=pl.Buffered(1))],
            out_specs=pl.BlockSpec((TOK_BLK, D), lambda i, *_: (i, 0)),
            scratch_shapes=[pltpu.VMEM((TOK_BLK, D), F32)]),
        out_shape=jax.ShapeDtypeStruct((n, D), F32),
        compiler_params=_cparams(("arbitrary",)),
    )(win.reshape(-1), slot, gate, x, g2, ye)


def _final_kernel(x_ref, g_ref, o_ref):
    x = x_ref[...]
    o_ref[...] = x * lax.rsqrt(jnp.mean(x * x, axis=-1, keepdims=True) + EPS) * g_ref[...]


def _final_norm(x, g):
    n = x.shape[0]
    tm = 1024
    return pl.pallas_call(
        _final_kernel,
        grid=(n // tm,),
        in_specs=[pl.BlockSpec((tm, D), lambda i: (i, 0)), pl.BlockSpec((1, D), lambda i: (0, 0))],
        out_specs=pl.BlockSpec((tm, D), lambda i: (i, 0)),
        out_shape=jax.ShapeDtypeStruct((n, D), F32),
        compiler_params=_cparams(("arbitrary",)),
    )(x, g.reshape(1, D))


def _layer(x, mod, prm, batch, seq_len, caug0, m0, h0_t, ck=None, cv=None, bias=None, kv_stack=None):
    sh1, sc1, g1, sh2, sc2, g2 = [m[:, None, :] for m in jnp.split(mod, 6, axis=-1)]
    attn_dtype = F32 if ck is None else BF16
    qkv, mo, sz, sxbc, nq, nk, nv, gate, *kv_out = _in_projection(
        x, prm["norm1_g"], sh1, sc1, prm["w_in"], seq_len, attn_dtype, kv_stack)
    xbc = _conv_silu(sxbc, prm["conv_w"], prm["conv_b"], seq_len)
    hf, hb, cfin, mfin = _mlstm_scan(qkv, gate, prm["f_bias"], caug0, m0, batch, seq_len)
    yf, yb, hfin = _ssd_scan(xbc, gate, prm["dt_bias"], prm["a_log"], h0_t, batch, seq_len)
    if ck is None:
        a_out = _ctx_attention(nq, nk, nv, batch, seq_len)
    else:
        a_out = _neighbourhood_attention(nq, nk, nv, ck, cv, bias, batch, seq_len)
    xn, h2, aff_t = _out_projection(x, hf, hb, mo, yf, yb, xbc, sz, a_out, prm["mnorm_g"], prm["d_skip"],
                                    prm["snorm_g"], prm["w_out"], g1, prm["norm2_g"], sh2, sc2,
                                    prm["router_wt"], seq_len)
    xo = _moe(h2, aff_t, xn, g2, prm["w1"], prm["w3"], prm["w2"], prm["layer"], seq_len)
    return xo, kv_out, cfin, mfin, hfin


def _pack_w_in(w_in):
    o = np.cumsum((0, D_M, D_M, D_M, D_M, 2 * H_M, 2 * H_M, D_S, D_XBC, 2 * H_S, D_NA, D_NA, D_NA))
    pad = jnp.zeros(w_in.shape[:-1] + (GATE_W - 4 * H_M - 2 * H_S,), w_in.dtype)
    parts = [w_in[..., o[0]:o[4]], w_in[..., o[6]:o[8]], w_in[..., o[9]:o[12]],
             w_in[..., o[4]:o[6]], w_in[..., o[8]:o[9]], pad]
    return jnp.concatenate(parts, axis=-1).astype(BF16)


def kernel(x_prompt, x_sample, cache_na_k, cache_na_v, state_mlstm_c, state_mlstm_n, state_mlstm_m, state_ssm, c, c_ctx, ada_w, ada_b, norm1_g, norm2_g, w_in, mlstm_f_bias, mlstm_norm_g, conv_w, conv_b, ssm_dt_bias, ssm_a_log, ssm_d, ssm_norm_g, na_rpb, w_out, router_w, exp_w1, exp_w3, exp_w2, final_g):
    bp, sp, _ = x_prompt.shape
    bs, ss, _ = x_sample.shape
    past = cache_na_k.shape[2]

    cvec = jnp.zeros((8, D), F32).at[0].set(c_ctx).at[1:1 + bs].set(c)
    mod = _modulation(cvec, ada_w, ada_b)

    w_in_p = _pack_w_in(w_in)
    w_out_bf = w_out.astype(BF16)
    router_wt = jnp.swapaxes(router_w, 1, 2)

    def aug(cs, ns):
        pad = jnp.zeros(cs.shape[:-2] + (128 - HD - 1, HD), F32)
        out = jnp.concatenate([jnp.swapaxes(cs, -1, -2), ns[..., None, :], pad], axis=-2)
        return out.reshape(cs.shape[:2] + (2 * H_M, 128, HD))

    lat_caug_all = aug(state_mlstm_c, state_mlstm_n)

    def rep_m(ms):
        b = ms.shape[0]
        return jnp.broadcast_to(ms.reshape(b, 2 * H_M, 1, 1), (b, 2 * H_M, 8, 128)).astype(F32)

    ctx_caug0 = jnp.zeros((bp, 2 * H_M, 128, HD), F32)
    ctx_m0 = jnp.full((bp, 2 * H_M, 8, 128), NEG_INIT, F32)
    ctx_h0 = jnp.zeros((bp, 2 * H_S, N_S, HD), F32)

    xp = x_prompt.reshape(bp * sp, D)
    xs = x_sample.reshape(bs * ss, D)
    cs, ns, ms, hs = [], [], [], []
    k_stack = v_stack = None
    for l in range(DEPTH):
        prm = dict(norm1_g=norm1_g[l], norm2_g=norm2_g[l], w_in=w_in_p[l], f_bias=mlstm_f_bias[l],
                   mnorm_g=mlstm_norm_g[l], conv_w=conv_w[l], conv_b=conv_b[l], dt_bias=ssm_dt_bias[l],
                   a_log=ssm_a_log[l], d_skip=ssm_d[l], snorm_g=ssm_norm_g[l], w_out=w_out_bf[l],
                   router_wt=router_wt[l], w1=exp_w1, w3=exp_w3, w2=exp_w2, layer=l)
        xp, (k_stack, v_stack), cfin, mfin, hfin = _layer(xp, mod[l, 0:1], prm, bp, sp, ctx_caug0, ctx_m0, ctx_h0,
                                                            kv_stack=(l, DEPTH, k_stack, v_stack))
        cs.append(cfin[..., 0:HD, :].reshape(bp, 2, H_M, HD, HD))
        ns.append(cfin[..., HD, :].reshape(bp, 2, H_M, HD))
        ms.append(mfin[..., 0, 0].reshape(bp, 2, H_M))
        hs.append(hfin.reshape(bp, 2, H_S, HD, N_S))

        lat_caug0 = lat_caug_all[:, l]
        lat_m0 = rep_m(state_mlstm_m[:, l])
        lat_h0 = state_ssm[:, l].reshape(bs, 2 * H_S, HD, N_S)
        ck = cache_na_k[:, l].reshape(bs * past, D_NA).astype(BF16)
        cv = cache_na_v[:, l].reshape(bs * past, D_NA).astype(BF16)
        bias = _natten_bias(na_rpb[l], ss // GRID_W)
        xs = _layer(xs, mod[l, 1:1 + bs], prm, bs, ss, lat_caug0, lat_m0, lat_h0, ck, cv, bias)[0]

    y_prompt = _final_norm(xp, final_g).reshape(bp, sp, D)
    y_sample = _final_norm(xs, final_g).reshape(bs, ss, D)
    return (y_prompt, y_sample, k_stack.reshape(bp, DEPTH, sp, H_NA, HD), v_stack.reshape(bp, DEPTH, sp, H_NA, HD),
            jnp.swapaxes(jnp.stack(cs, axis=1), -1, -2),
            jnp.stack(ns, axis=1), jnp.stack(ms, axis=1), jnp.stack(hs, axis=1))
```

```python
import functools

import numpy as np
import jax
import jax.numpy as jnp
from jax import lax
from jax.experimental import pallas as pl
from jax.experimental.pallas import tpu as pltpu

F32 = jnp.float32
BF16 = jnp.bfloat16
HI = lax.Precision.HIGHEST

D = 1024
DEPTH = 4
HD = 64
H_M = 4
D_M = H_M * HD
H_S = 6
D_S = H_S * HD
G_S = 2
R_S = H_S // G_S
N_S = 64
D_XBC = D_S + 2 * G_S * N_S
H_NA = 6
D_NA = H_NA * HD
GRID_W = 64
WIN_H = 8
WIN_W = 16
N_EXPERTS = 16
EC_FACTOR = 2
EPS = 1e-6
NEG_INIT = -1e30
TOK_BLK = 128
SLOT_BLK = 128
GATHER_TOK = 512
GATHER_SLOT = 256
GATHER_WIN = 6
GATE_W = 128
MI0, MF0, DT0 = 0, 2 * H_M, 4 * H_M
W_PACKED = 3 * D_M + D_M + D_S + D_XBC + 3 * D_NA + GATE_W

VMEM_LIMIT = 56 * 1024 * 1024


def _cparams(sem):
    return pltpu.CompilerParams(dimension_semantics=sem, vmem_limit_bytes=VMEM_LIMIT)


def _sigmoid(x):
    return 1.0 / (1.0 + jnp.exp(-x))


def _softplus(x):
    return jnp.maximum(x, 0.0) + jnp.log1p(jnp.exp(-jnp.abs(x)))


def _dot(a, b):
    return jnp.dot(a, b, preferred_element_type=F32)


def _dot_nt(a, b):
    return lax.dot_general(a, b, (((1,), (1,)), ((), ())), preferred_element_type=F32)


def _dot_tn(a, b):
    return lax.dot_general(a, b, (((0,), (0,)), ((), ())), preferred_element_type=F32)


def _mod_kernel(c_ref, w_ref, b_ref, o_ref):
    cv = c_ref[...]
    s = cv * _sigmoid(cv)
    o_ref[...] = jnp.dot(s, w_ref[...], precision=HI, preferred_element_type=F32) + b_ref[...]


def _modulation(cvec, ada_w, ada_b):
    tn = 1536
    return pl.pallas_call(
        _mod_kernel,
        grid=(DEPTH, 6 * D // tn),
        in_specs=[pl.BlockSpec((8, D), lambda l, j: (0, 0)),
                  pl.BlockSpec((None, D, tn), lambda l, j: (l, 0, j)),
                  pl.BlockSpec((None, 1, tn), lambda l, j: (l, 0, j))],
        out_specs=pl.BlockSpec((None, 8, tn), lambda l, j: (l, 0, j)),
        out_shape=jax.ShapeDtypeStruct((DEPTH, 8, 6 * D), F32),
        compiler_params=_cparams(("arbitrary", "arbitrary")),
    )(cvec, ada_w, ada_b.reshape(DEPTH, 1, 6 * D))


_IN_SEGS = (("qkv", 0, 768), ("mo", 768, 1024), ("sz", 1024, 1408), ("xbc", 1408, 2048),
            ("nq", 2048, 2432), ("nk", 2432, 2816), ("nv", 2816, 3200), ("gate", 3200, 3328))


def _inproj_kernel(x_ref, xp_ref, xn_ref, g_ref, sh_ref, sc_ref, w_ref, cw_ref, cb_ref, *rest,
                   n_alias, n_seg, seq_len):
    outs = rest[n_alias:]
    tm = x_ref.shape[0]

    def norm_mod(x):
        ms = jnp.mean(x * x, axis=-1, keepdims=True)
        h = x * lax.rsqrt(ms + EPS) * g_ref[...]
        return (h * (1.0 + sc_ref[...]) + sh_ref[...]).astype(BF16)

    hb = norm_mod(x_ref[...])
    vals = {}
    for o_ref, (name, lo, hi) in zip(outs[:n_seg], _IN_SEGS):
        vals[name] = _dot(hb, w_ref[:, lo:hi])
        if name == "xbc":
            cur = vals[name]
            edge = _dot(norm_mod(jnp.concatenate([xp_ref[...], xn_ref[...]], axis=0)), w_ref[:, lo:hi])
            rows = lax.broadcasted_iota(jnp.int32, cur.shape, 0)
            if seq_len >= tm:
                i = pl.program_id(0)
                tps = seq_len // tm
                starts = (rows == 0) & ((i % tps) == 0)
                ends = (rows == tm - 1) & ((i % tps) == tps - 1)
            else:
                starts = functools.reduce(jnp.logical_or, [rows == k * seq_len for k in range(tm // seq_len)])
                ends = functools.reduce(jnp.logical_or, [rows == (k + 1) * seq_len - 1 for k in range(tm // seq_len)])
            prev = jnp.where(rows == 0, edge[7:8, :], pltpu.roll(cur, 1, 0))
            nxt = jnp.where(rows == tm - 1, edge[8:9, :], pltpu.roll(cur, tm - 1, 0))
            prev = jnp.where(starts, 0.0, prev)
            nxt = jnp.where(ends, 0.0, nxt)
            y = prev * cw_ref[0:1, :] + cur * cw_ref[1:2, :] + nxt * cw_ref[2:3, :] + cb_ref[...]
            vals[name] = y * _sigmoid(y)
        o_ref[...] = vals[name].astype(o_ref.dtype)
    for o_ref, name in zip(outs[n_seg:], ("nk", "nv")):
        o_ref[...] = vals[name].reshape(o_ref.shape)


def _in_projection(x, norm_g, shift, scale, w_packed, conv_w, conv_b, seq_len, attn_dtype, kv_stack=None):
    n = x.shape[0]
    tm = 512
    groups = shift.shape[0]
    if groups == 1:
        gidx = lambda i: (0, 0, 0)
    else:
        gidx = lambda i: ((i * tm) // seq_len, 0, 0)
    out_shape = [jax.ShapeDtypeStruct((n, hi - lo), attn_dtype if name in ("nq", "nk", "nv") else F32)
                 for name, lo, hi in _IN_SEGS]
    out_specs = [pl.BlockSpec((tm, hi - lo), lambda i: (i, 0)) for _, lo, hi in _IN_SEGS]
    assert seq_len % tm == 0 or tm % seq_len == 0
    nb8 = n // 8
    in_specs = [pl.BlockSpec((tm, D), lambda i: (i, 0)),
                pl.BlockSpec((8, D), lambda i: (jnp.maximum(i * (tm // 8) - 1, 0), 0)),
                pl.BlockSpec((8, D), lambda i: (jnp.minimum((i + 1) * (tm // 8), nb8 - 1), 0)),
                pl.BlockSpec((1, D), lambda i: (0, 0)),
                pl.BlockSpec((None, 1, D), gidx),
                pl.BlockSpec((None, 1, D), gidx),
                pl.BlockSpec((D, W_PACKED), lambda i: (0, 0)),
                pl.BlockSpec((3, D_XBC), lambda i: (0, 0)),
                pl.BlockSpec((1, D_XBC), lambda i: (0, 0))]
    args = [x, x, x, norm_g.reshape(1, D), shift, scale, w_packed, conv_w, conv_b.reshape(1, D_XBC)]
    aliases = {}
    n_alias = 0
    if kv_stack is not None:
        layer, depth, k_prev, v_prev = kv_stack
        bpt = tm // seq_len
        stack_shape = jax.ShapeDtypeStruct((n // seq_len, depth, seq_len, D_NA), F32)
        for prev in (k_prev, v_prev):
            out_shape.append(stack_shape)
            out_specs.append(pl.BlockSpec((bpt, None, seq_len, D_NA), lambda i: (i, layer, 0, 0)))
            if prev is not None:
                aliases[len(args)] = len(out_shape) - 1
                in_specs.append(pl.BlockSpec(memory_space=pl.ANY))
                args.append(prev)
                n_alias += 1
    return pl.pallas_call(
        functools.partial(_inproj_kernel, n_alias=n_alias, n_seg=len(_IN_SEGS), seq_len=seq_len),
        grid=(n // tm,),
        in_specs=in_specs,
        out_specs=tuple(out_specs),
        out_shape=tuple(out_shape),
        input_output_aliases=aliases,
        compiler_params=_cparams(("arbitrary",)),
    )(*args)


def _tri_masks(ch):
    r = lax.broadcasted_iota(jnp.int32, (ch, ch), 0)
    c = lax.broadcasted_iota(jnp.int32, (ch, ch), 1)
    return r >= c, r <= c


def _mlstm_kernel(qkv_f, qkv_b, gt_f, gt_b, fb_ref, c0_ref, m0_ref,
                  hf_ref, hb_ref, cfin_ref, mfin_ref, c_scr, m_scr, it_scr, bt_scr, rc_scr, vt_scr,
                  kq_scr, in_scr, *, n_tiles, ch):
    j = pl.program_id(1)

    @pl.when(j == 0)
    def _():
        c_scr[...] = c0_ref[...]
        m_scr[...] = m0_ref[...]

    tt = qkv_f.shape[0]
    nch = tt // ch
    fb = fb_ref[...]
    low, upp = _tri_masks(ch)
    ones_row = (lax.broadcasted_iota(jnp.int32, (HD, ch), 0) == 0).astype(F32)
    pad_rows = jnp.zeros((ch - 2 * H_M, ch), F32)

    dc = [(d, c) for d in range(2) for c in range(nch)]
    i_rows, logf_rows, b_rows = {}, {}, {}
    for d, c in dc:
        z_t = ((gt_f, gt_b)[d][c * ch:(c + 1) * ch, :] + fb).T
        i_rows[d, c] = z_t[MI0:MI0 + 2 * H_M, :]
        zf = z_t[MF0:MF0 + 2 * H_M, :]
        logf_rows[d, c] = jnp.minimum(zf, 0.0) - jnp.log1p(jnp.exp(-jnp.abs(zf)))
    for d, c in dc:
        mask = (upp, low)[d]
        b_rows[d, c] = jnp.dot(logf_rows[d, c], mask.astype(F32), precision=HI, preferred_element_type=F32)
        it_scr[d, c] = i_rows[d, c]
        bt_scr[d, c] = b_rows[d, c]
    for d, c in dc:
        rc_scr[d, c] = jnp.concatenate([i_rows[d, c] - b_rows[d, c], pad_rows], axis=0).T
    for d, c in dc:
        for p in range(H_M // 2):
            vt_scr[d, c, p * 2 * HD:(p + 1) * 2 * HD, :] = (
                (qkv_f, qkv_b)[d][c * ch:(c + 1) * ch, 2 * D_M + p * 2 * HD:2 * D_M + (p + 1) * 2 * HD].T)

    def chunk(ci, carry):
        for d in range(2):
            qkv_ref = (qkv_f, qkv_b)[d]
            c0 = pl.multiple_of((ci if d == 0 else nch - 1 - ci) * ch, ch)
            for h in range(H_M):
                idx = d * H_M + h
                q = qkv_ref[pl.ds(c0, ch), h * HD:(h + 1) * HD].astype(BF16)
                k = (qkv_ref[pl.ds(c0, ch), D_M + h * HD:D_M + (h + 1) * HD] * (HD ** -0.5)).astype(BF16)
                kq_scr[idx] = _dot_nt(k, q)
                in_scr[idx] = _dot_nt(c_scr[idx].astype(BF16), q)
        for d in range(2):
            qkv_ref = (qkv_f, qkv_b)[d]
            out_ref = (hf_ref, hb_ref)[d]
            mask = (upp, low)[d]
            cidx = ci if d == 0 else nch - 1 - ci
            c0 = pl.multiple_of(cidx * ch, ch)
            i_t = it_scr[d, cidx]
            b_t = bt_scr[d, cidx]
            r_cols = rc_scr[d, cidx]
            edge = ch - 1 if d == 0 else 0
            h_t = []
            for h in range(H_M):
                idx = d * H_M + h
                vaug = jnp.concatenate([vt_scr[d, cidx, h * HD:(h + 1) * HD, :], ones_row], axis=0)
                k = (qkv_ref[pl.ds(c0, ch), D_M + h * HD:D_M + (h + 1) * HD] * (HD ** -0.5)).astype(BF16)
                b_row = b_t[idx:idx + 1, :]
                i_row = i_t[idx:idx + 1, :]
                m_prev = m_scr[idx][0:1, 0:1]
                caug = c_scr[idx]
                logd = jnp.where(mask, r_cols[:, idx:idx + 1] + b_row, -jnp.inf)
                m_inter = b_row + m_prev
                m_t = jnp.maximum(jnp.max(logd, axis=0, keepdims=True), m_inter)
                dmat = jnp.exp(logd - m_t)
                w_inter = jnp.exp(m_inter - m_t)
                s = (kq_scr[idx] * dmat).astype(BF16)
                tot = _dot(vaug.astype(BF16), s) + w_inter * in_scr[idx]
                den = jnp.maximum(jnp.abs(tot[HD:HD + 1, :]), jnp.exp(-m_t))
                h_t.append(tot[0:HD, :] / den)
                if h % 2 == 1:
                    pair = jnp.concatenate(h_t[-2:], axis=0).T
                    out_ref[pl.ds(c0, ch), (h - 1) * HD:(h + 1) * HD] = pair
                b_l = b_row[:, edge:edge + 1]
                lw = b_l - b_row + i_row
                m_new = jnp.maximum(b_l + m_prev, jnp.max(lw, axis=1, keepdims=True))
                wk = jnp.exp(lw - m_new)
                decay = jnp.exp(b_l + m_prev - m_new)
                c_scr[idx] = decay * caug + _dot((vaug * wk).astype(BF16), k)
                m_scr[idx] = jnp.broadcast_to(m_new, (8, 128))
        return carry

    lax.fori_loop(0, nch, chunk, 0)

    @pl.when(j == n_tiles - 1)
    def _():
        cfin_ref[...] = c_scr[...]
        mfin_ref[...] = m_scr[...]


def _mlstm_scan(qkv, gate, f_bias, caug0, m0, batch, seq_len, layer=0, depth=1, prev=None):
    n = qkv.shape[0]
    tt = min(seq_len, 512)
    nt = seq_len // tt
    ch = 128
    fb = jnp.zeros((1, GATE_W), F32).at[0, MF0:MF0 + 2 * H_M].set(f_bias.reshape(-1))
    fwd = lambda b, j: (b * nt + j, 0)
    bwd = lambda b, j: (b * nt + nt - 1 - j, 0)
    st4 = lambda b, j: (b, 0, 0, 0)
    st5 = lambda b, j: (b, layer, 0, 0, 0)
    n_in = 7
    prev = tuple(prev or ())
    kern = functools.partial(_mlstm_kernel, n_tiles=nt, ch=ch)
    return pl.pallas_call(
        lambda *refs: kern(*refs[:n_in], *refs[n_in + len(prev):]),
        grid=(batch, nt),
        in_specs=[pl.BlockSpec((tt, 3 * D_M), fwd), pl.BlockSpec((tt, 3 * D_M), bwd),
                  pl.BlockSpec((tt, GATE_W), fwd), pl.BlockSpec((tt, GATE_W), bwd),
                  pl.BlockSpec((1, GATE_W), lambda b, j: (0, 0)),
                  pl.BlockSpec((None, 2 * H_M, 128, HD), st4),
                  pl.BlockSpec((None, 2 * H_M, 8, 128), st4)] + [pl.BlockSpec(memory_space=pl.ANY)] * len(prev),
        out_specs=(pl.BlockSpec((tt, D_M), fwd), pl.BlockSpec((tt, D_M), bwd),
                   pl.BlockSpec((None, None, 2 * H_M, 128, HD), st5),
                   pl.BlockSpec((None, None, 2 * H_M, 8, 128), st5)),
        out_shape=(jax.ShapeDtypeStruct((n, D_M), F32), jax.ShapeDtypeStruct((n, D_M), F32),
                   jax.ShapeDtypeStruct((batch, depth, 2 * H_M, 128, HD), F32),
                   jax.ShapeDtypeStruct((batch, depth, 2 * H_M, 8, 128), F32)),
        input_output_aliases={n_in + i: 2 + i for i in range(len(prev))},
        scratch_shapes=[pltpu.VMEM((2 * H_M, 128, HD), F32), pltpu.VMEM((2 * H_M, 8, 128), F32),
                        pltpu.VMEM((2, tt // ch, 2 * H_M, ch), F32), pltpu.VMEM((2, tt // ch, 2 * H_M, ch), F32),
                        pltpu.VMEM((2, tt // ch, ch, 128), F32), pltpu.VMEM((2, tt // ch, D_M, ch), F32),
                        pltpu.VMEM((2 * H_M, ch, ch), F32), pltpu.VMEM((2 * H_M, 128, ch), F32)],
        compiler_params=_cparams(("arbitrary", "arbitrary")),
    )(qkv, qkv, gate, gate, fb, caug0, m0, *prev)


def _ssd_kernel(xbc_f, xbc_b, gt_f, gt_b, dtb_ref, alog_ref, h0_ref,
                yf_ref, yb_ref, hfin_ref, h_scr, dtt_scr, at_scr, ac_scr, xt_scr, cb_scr, in_scr, *, n_tiles, ch):
    j = pl.program_id(1)

    @pl.when(j == 0)
    def _():
        h_scr[...] = h0_ref[...]

    tt = xbc_f.shape[0]
    nch = tt // ch
    dtb = dtb_ref[...]
    a_neg = -jnp.exp(alog_ref[...])
    low, upp = _tri_masks(ch)
    nrow = 16
    pad_rows = jnp.zeros((ch - nrow, ch), F32)

    dc = [(d, c) for d in range(2) for c in range(nch)]
    da_rows, a_rows = {}, {}
    for d, c in dc:
        dt = _softplus((gt_f, gt_b)[d][c * ch:(c + 1) * ch, :] + dtb)
        dtt_scr[d, c] = dt.T[DT0:DT0 + nrow, :]
        da_rows[d, c] = (dt * a_neg).T[DT0:DT0 + nrow, :]
    for d, c in dc:
        mask = (upp, low)[d]
        a_rows[d, c] = jnp.dot(da_rows[d, c], mask.astype(F32), precision=HI, preferred_element_type=F32)
        at_scr[d, c] = a_rows[d, c]
    for d, c in dc:
        ac_scr[d, c] = jnp.concatenate([a_rows[d, c], pad_rows], axis=0).T
    for d, c in dc:
        for p in range(H_S // 2):
            xt_scr[d, c, p * 2 * HD:(p + 1) * 2 * HD, :] = (
                (xbc_f, xbc_b)[d][c * ch:(c + 1) * ch, p * 2 * HD:(p + 1) * 2 * HD].T)

    def chunk(ci, carry):
        for d in range(2):
            x_ref = (xbc_f, xbc_b)[d]
            c0 = pl.multiple_of((ci if d == 0 else nch - 1 - ci) * ch, ch)
            for grp in range(G_S):
                bm = x_ref[pl.ds(c0, ch), D_S + grp * N_S:D_S + (grp + 1) * N_S].astype(BF16)
                cm = x_ref[pl.ds(c0, ch), D_S + G_S * N_S + grp * N_S:D_S + G_S * N_S + (grp + 1) * N_S].astype(BF16)
                cb_scr[d * G_S + grp] = _dot_nt(bm, cm)
                for r in range(R_S):
                    idx = d * H_S + grp * R_S + r
                    in_scr[idx] = _dot_nt(h_scr[idx].astype(BF16), cm)
        for d in range(2):
            x_ref = (xbc_f, xbc_b)[d]
            out_ref = (yf_ref, yb_ref)[d]
            mask = (upp, low)[d]
            cidx = ci if d == 0 else nch - 1 - ci
            c0 = pl.multiple_of(cidx * ch, ch)
            dt_t = dtt_scr[d, cidx]
            a_t = at_scr[d, cidx]
            a_cols = ac_scr[d, cidx]
            edge = ch - 1 if d == 0 else 0
            y_t = []
            for grp in range(G_S):
                bm = x_ref[pl.ds(c0, ch), D_S + grp * N_S:D_S + (grp + 1) * N_S].astype(BF16)
                for r in range(R_S):
                    hh = grp * R_S + r
                    idx = d * H_S + hh
                    a_row = a_t[idx:idx + 1, :]
                    dt_row = dt_t[idx:idx + 1, :]
                    seg = jnp.exp(jnp.where(mask, a_row - a_cols[:, idx:idx + 1], -jnp.inf))
                    mm = (cb_scr[d * G_S + grp] * seg).astype(BF16)
                    xh = xt_scr[d, cidx, hh * HD:(hh + 1) * HD, :]
                    h_st = h_scr[idx]
                    y_t.append(_dot((xh * dt_row).astype(BF16), mm) + in_scr[idx] * jnp.exp(a_row))
                    if hh % 2 == 1:
                        pair = jnp.concatenate(y_t[-2:], axis=0).T
                        out_ref[pl.ds(c0, ch), (hh - 1) * HD:(hh + 1) * HD] = pair
                    a_l = a_row[:, edge:edge + 1]
                    wk = jnp.exp(a_l - a_row) * dt_row
                    h_scr[idx] = jnp.exp(a_l) * h_st + _dot((xh * wk).astype(BF16), bm)
        return carry

    lax.fori_loop(0, nch, chunk, 0)

    @pl.when(j == n_tiles - 1)
    def _():
        hfin_ref[...] = h_scr[...]


def _ssd_scan(xbc, gate, dt_bias, a_log, h0_t, batch, seq_len, layer=0, depth=1, prev=None):
    n = xbc.shape[0]
    tt = min(seq_len, 512)
    nt = seq_len // tt
    ch = 128
    dtb = jnp.zeros((1, GATE_W), F32).at[0, DT0:DT0 + 2 * H_S].set(dt_bias.reshape(-1))
    alog = jnp.zeros((1, GATE_W), F32).at[0, DT0:DT0 + 2 * H_S].set(a_log.reshape(-1))
    fwd = lambda b, j: (b * nt + j, 0)
    bwd = lambda b, j: (b * nt + nt - 1 - j, 0)
    st4 = lambda b, j: (b, 0, 0, 0)
    n_in = 7
    prev = tuple(prev or ())
    kern = functools.partial(_ssd_kernel, n_tiles=nt, ch=ch)
    return pl.pallas_call(
        lambda *refs: kern(*refs[:n_in], *refs[n_in + len(prev):]),
        grid=(batch, nt),
        in_specs=[pl.BlockSpec((tt, D_XBC), fwd), pl.BlockSpec((tt, D_XBC), bwd),
                  pl.BlockSpec((tt, GATE_W), fwd), pl.BlockSpec((tt, GATE_W), bwd),
                  pl.BlockSpec((1, GATE_W), lambda b, j: (0, 0)),
                  pl.BlockSpec((1, GATE_W), lambda b, j: (0, 0)),
                  pl.BlockSpec((None, 2 * H_S, N_S, HD), st4)] + [pl.BlockSpec(memory_space=pl.ANY)] * len(prev),
        out_specs=(pl.BlockSpec((tt, D_S), fwd), pl.BlockSpec((tt, D_S), bwd),
                   pl.BlockSpec((None, None, 2 * H_S, N_S, HD), lambda b, j: (b, layer, 0, 0, 0))),
        out_shape=(jax.ShapeDtypeStruct((n, D_S), F32), jax.ShapeDtypeStruct((n, D_S), F32),
                   jax.ShapeDtypeStruct((batch, depth, 2 * H_S, N_S, HD), F32)),
        input_output_aliases={n_in + i: 2 + i for i in range(len(prev))},
        scratch_shapes=[pltpu.VMEM((2 * H_S, N_S, HD), F32),
                        pltpu.VMEM((2, tt // ch, 16, ch), F32), pltpu.VMEM((2, tt // ch, 16, ch), F32),
                        pltpu.VMEM((2, tt // ch, ch, 128), F32), pltpu.VMEM((2, tt // ch, D_S, ch), F32),
                        pltpu.VMEM((2 * G_S, ch, ch), F32), pltpu.VMEM((2 * H_S, HD, ch), F32)],
        compiler_params=_cparams(("arbitrary", "arbitrary")),
    )(xbc, xbc, gate, gate, dtb, alog, h0_t, *prev)


def _ctx_attn_kernel(q_ref, k_ref, v_ref, o_ref, s_scr):
    n = q_ref.shape[0]
    first = lax.broadcasted_iota(jnp.int32, (n, 2 * HD), 1) < HD
    ones = jnp.ones((n, 2 * HD), BF16)
    for p in range(H_NA // 2):
        sl = slice(p * 2 * HD, (p + 1) * 2 * HD)
        qp = q_ref[:, sl]
        kp = k_ref[:, sl].astype(BF16)
        for half in range(2):
            mine = first if half == 0 else jnp.logical_not(first)
            q = (jnp.where(mine, qp, jnp.zeros_like(qp)) * (HD ** -0.5)).astype(BF16)
            s_scr[2 * p + half] = _dot_nt(q, kp)
    for p in range(H_NA // 2):
        sl = slice(p * 2 * HD, (p + 1) * 2 * HD)
        vp = v_ref[:, sl].astype(BF16)
        outs = []
        for half in range(2):
            s = s_scr[2 * p + half]
            pr = jnp.exp(s - jnp.max(s, axis=-1, keepdims=True)).astype(BF16)
            outs.append(_dot(pr, vp) / _dot(pr, ones))
        o_ref[:, sl] = jnp.where(first, outs[0], outs[1])


def _ctx_attention(q, k, v, batch, seq_len):
    spec = pl.BlockSpec((seq_len, D_NA), lambda b: (b, 0))
    return pl.pallas_call(
        _ctx_attn_kernel,
        grid=(batch,),
        in_specs=[spec, spec, spec],
        out_specs=spec,
        out_shape=jax.ShapeDtypeStruct(q.shape, F32),
        scratch_shapes=[pltpu.VMEM((H_NA, seq_len, seq_len), F32)],
        compiler_params=_cparams(("arbitrary",)),
    )(q, k, v)


def _row_start(r, rows):
    return jnp.clip(r - WIN_H // 2, 0, rows - WIN_H)


def _natten_kernel(q_ref, k_ref, v_ref, ck_ref, cv_ref, bias_ref, o_ref, sl_scr, sc_scr, *, rows):
    r = pl.program_id(1)
    k0 = pl.multiple_of(_row_start(r, rows) * GRID_W, GRID_W)
    off0 = _row_start(r, rows) - r + WIN_H - 1
    nloc = WIN_H * GRID_W
    npast = ck_ref.shape[0]
    first = lax.broadcasted_iota(jnp.int32, (GRID_W, 2 * HD), 1) < HD
    ones_loc = jnp.ones((nloc, 2 * HD), BF16)
    ones_ctx = jnp.ones((npast, 2 * HD), BF16)
    for p in range(H_NA // 2):
        sl = slice(p * 2 * HD, (p + 1) * 2 * HD)
        qp = q_ref[:, sl]
        kw = k_ref[pl.ds(k0, nloc), sl].astype(BF16)
        ckp = ck_ref[:, sl].astype(BF16)
        for half in range(2):
            mine = first if half == 0 else jnp.logical_not(first)
            q = (jnp.where(mine, qp, jnp.zeros_like(qp)) * (HD ** -0.5)).astype(BF16)
            bias = jnp.concatenate([bias_ref[2 * p + half, off0 + 2 * i] for i in range(WIN_H // 2)], axis=1)
            sl_scr[2 * p + half] = _dot_nt(q, kw) + bias
            sc_scr[2 * p + half] = _dot_nt(q, ckp)
    for p in range(H_NA // 2):
        sl = slice(p * 2 * HD, (p + 1) * 2 * HD)
        vw = v_ref[pl.ds(k0, nloc), sl].astype(BF16)
        cvp = cv_ref[:, sl].astype(BF16)
        outs = []
        for half in range(2):
            s_loc = sl_scr[2 * p + half]
            s_ctx = sc_scr[2 * p + half]
            m = jnp.maximum(jnp.max(s_loc, axis=-1, keepdims=True), jnp.max(s_ctx, axis=-1, keepdims=True))
            p_loc = jnp.exp(s_loc - m).astype(BF16)
            p_ctx = jnp.exp(s_ctx - m).astype(BF16)
            l = _dot(p_loc, ones_loc) + _dot(p_ctx, ones_ctx)
            outs.append((_dot(p_loc, vw) + _dot(p_ctx, cvp)) / l)
        o_ref[:, sl] = jnp.where(first, outs[0], outs[1])


def _natten_bias(rpb, rows):
    qc = np.arange(GRID_W)[:, None]
    kc = np.arange(GRID_W)[None, :]
    cstart = np.clip(qc - WIN_W // 2, 0, GRID_W - WIN_W)
    ok = (kc >= cstart) & (kc < cstart + WIN_W)
    col_off = np.clip(kc - qc + WIN_W - 1, 0, 2 * WIN_W - 2)
    pick = jnp.asarray((col_off[None] == np.arange(2 * WIN_W - 1)[:, None, None]).astype(np.float32))
    toep = jnp.einsum("hoj,jqk->hoqk", rpb.astype(F32), pick, precision=HI)
    t = jnp.where(jnp.asarray(ok)[None, None], toep, -jnp.inf)
    return jnp.concatenate([t[:, :-1], t[:, 1:]], axis=-1)


def _neighbourhood_attention(q, k, v, ck, cv, bias, batch, seq_len):
    rows = seq_len // GRID_W
    past = ck.shape[0] // batch
    return pl.pallas_call(
        functools.partial(_natten_kernel, rows=rows),
        grid=(batch, rows),
        in_specs=[pl.BlockSpec((GRID_W, D_NA), lambda b, r: (b * rows + r, 0)),
                  pl.BlockSpec((seq_len, D_NA), lambda b, r: (b, 0)),
                  pl.BlockSpec((seq_len, D_NA), lambda b, r: (b, 0)),
                  pl.BlockSpec((past, D_NA), lambda b, r: (b, 0)),
                  pl.BlockSpec((past, D_NA), lambda b, r: (b, 0)),
                  pl.BlockSpec((H_NA, 2 * WIN_H - 2, GRID_W, 2 * GRID_W), lambda b, r: (0, 0, 0, 0))],
        out_specs=pl.BlockSpec((GRID_W, D_NA), lambda b, r: (b * rows + r, 0)),
        out_shape=jax.ShapeDtypeStruct(q.shape, F32),
        scratch_shapes=[pltpu.VMEM((H_NA, GRID_W, WIN_H * GRID_W), F32), pltpu.VMEM((H_NA, GRID_W, past), F32)],
        compiler_params=_cparams(("arbitrary", "arbitrary")),
    )(q, k, v, ck, cv, bias)


def _outproj_kernel(x_ref, hf_ref, hb_ref, mo_ref, yf_ref, yb_ref, xbc_ref, sz_ref, a_ref,
                    mg_ref, dsk_ref, sg_ref, bd_ref, w_ref, g1_ref, n2_ref, sh2_ref, sc2_ref, rw_ref,
                    xo_ref, h2_ref, aff_ref):
    y_attn = _dot(a_ref[...].astype(BF16), w_ref[D_M + D_S:D, :])
    hm = hf_ref[...] + hb_ref[...]
    ssq = jnp.dot(hm * hm, bd_ref[...], precision=HI, preferred_element_type=F32)
    m_out = hm * lax.rsqrt(ssq * (1.0 / HD) + EPS) * mg_ref[...] * _sigmoid(mo_ref[...])
    sz = sz_ref[...]
    ys = (yf_ref[...] + yb_ref[...] + dsk_ref[...] * xbc_ref[:, 0:D_S]) * (sz * _sigmoid(sz))
    s_out = ys * lax.rsqrt(jnp.mean(ys * ys, axis=-1, keepdims=True) + EPS) * sg_ref[...]
    y = y_attn + _dot(m_out.astype(BF16), w_ref[0:D_M, :]) + _dot(s_out.astype(BF16), w_ref[D_M:D_M + D_S, :])
    xn = x_ref[...] + g1_ref[...] * y
    xo_ref[...] = xn
    h2 = xn * lax.rsqrt(jnp.mean(xn * xn, axis=-1, keepdims=True) + EPS) * n2_ref[...]
    h2 = h2 * (1.0 + sc2_ref[...]) + sh2_ref[...]
    h2_ref[...] = h2.T.astype(BF16)
    logits = lax.dot_general(rw_ref[...], h2, (((1,), (1,)), ((), ())), precision=HI,
                             preferred_element_type=F32)
    mx = jnp.max(logits, axis=0, keepdims=True)
    ex = jnp.exp(logits - mx)
    aff_ref[...] = ex / jnp.sum(ex, axis=0, keepdims=True)


def _out_projection(x, hf, hb, mo, yf, yb, xbc, sz, a_out, mnorm_g, d_skip, snorm_g, w_out_bf,
                    g1, norm2_g, sh2, sc2, router_wt, seq_len):
    n = x.shape[0]
    tm = GATHER_TOK
    groups = g1.shape[0]
    if groups == 1:
        gidx = lambda i: (0, 0, 0)
    else:
        gidx = lambda i: ((i * tm) // seq_len, 0, 0)
    row = lambda w: pl.BlockSpec((tm, w), lambda i: (i, 0))
    const = lambda s: pl.BlockSpec(s, lambda i: (0,) * len(s))
    hid = np.arange(D_M) // HD
    blockdiag = jnp.asarray((hid[:, None] == hid[None, :]).astype(np.float32))
    return pl.pallas_call(
        _outproj_kernel,
        grid=(n // tm,),
        in_specs=[row(D), row(D_M), row(D_M), row(D_M), row(D_S), row(D_S), row(D_XBC), row(D_S), row(D_NA),
                  const((1, D_M)), const((1, D_S)), const((1, D_S)), const((D_M, D_M)), const((D, D)),
                  pl.BlockSpec((None, 1, D), gidx), const((1, D)),
                  pl.BlockSpec((None, 1, D), gidx), pl.BlockSpec((None, 1, D), gidx),
                  const((N_EXPERTS, D))],
        out_specs=(row(D), pl.BlockSpec((None, D, tm), lambda i: (i, 0, 0)),
                   pl.BlockSpec((N_EXPERTS, tm), lambda i: (0, i))),
        out_shape=(jax.ShapeDtypeStruct((n, D), F32), jax.ShapeDtypeStruct((n // tm, D, tm), BF16),
                   jax.ShapeDtypeStruct((N_EXPERTS, n), F32)),
        compiler_params=_cparams(("arbitrary",)),
    )(x, hf, hb, mo, yf, yb, xbc, sz, a_out,
      mnorm_g.reshape(1, D_M), jnp.repeat(d_skip, HD).reshape(1, D_S), snorm_g.reshape(1, D_S), blockdiag,
      w_out_bf, g1, norm2_g.reshape(1, D), sh2, sc2, router_wt)


def _select_kernel(aff_ref, gate_ref, slot_ref, start_ref, *, cap):
    aff = aff_ref[...]
    bits = pltpu.bitcast(aff, jnp.int32)
    n_tok = aff.shape[1]

    def step(i, prefix):
        cand = prefix | (jnp.int32(1) << (30 - i))
        cnt = jnp.sum((bits >= cand).astype(F32), axis=1, keepdims=True)
        return jnp.where(cnt >= cap, cand, prefix)

    thr = lax.fori_loop(0, 31, step, jnp.zeros((aff.shape[0], 1), jnp.int32))
    gt = bits > thr
    eq = bits == thr
    need = cap - jnp.sum(gt.astype(F32), axis=1, keepdims=True)
    r_i = lax.broadcasted_iota(jnp.int32, (128, 128), 0)
    c_i = lax.broadcasted_iota(jnp.int32, (128, 128), 1)
    strict = (r_i < c_i).astype(BF16)
    run_eq = jnp.zeros((aff.shape[0], 1), F32)
    run_sel = jnp.zeros((aff.shape[0], 1), F32)
    lane = lax.broadcasted_iota(jnp.int32, (aff.shape[0], 128), 1)
    starts = jnp.zeros((aff.shape[0], 128), F32)
    for blk in range(n_tok // TOK_BLK):
        sl = slice(blk * TOK_BLK, (blk + 1) * TOK_BLK)
        e = eq[:, sl]
        rank = _dot(e.astype(BF16), strict) + run_eq
        keep = gt[:, sl] | (e & (rank < need))
        kf = jnp.where(keep, 1.0, 0.0)
        slot = _dot(kf.astype(BF16), strict) + run_sel
        gate_ref[:, sl] = jnp.where(keep, aff[:, sl], 0.0)
        slot_ref[:, sl] = jnp.where(keep, slot, -1.0).astype(jnp.int32)
        starts = jnp.where(lane == blk, run_sel, starts)
        run_eq = run_eq + jnp.sum(e.astype(F32), axis=1, keepdims=True)
        run_sel = run_sel + jnp.sum(kf, axis=1, keepdims=True)
    start_ref[...] = starts.astype(jnp.int32)


def _select(aff_t):
    n = aff_t.shape[1]
    cap = EC_FACTOR * n // N_EXPERTS
    assert n // TOK_BLK <= 128
    return pl.pallas_call(
        functools.partial(_select_kernel, cap=float(cap)),
        out_shape=(jax.ShapeDtypeStruct(aff_t.shape, F32), jax.ShapeDtypeStruct(aff_t.shape, jnp.int32),
                   jax.ShapeDtypeStruct((N_EXPERTS, 128), jnp.int32)),
        compiler_params=pltpu.CompilerParams(vmem_limit_bytes=VMEM_LIMIT),
    )(aff_t)


def _gather_rows(e, lo_ref, hi_ref, slot_ref, h2t_ref, xe_scr, acc_scr, col_scr, cap):
    nsb = cap // GATHER_SLOT
    nchunk = h2t_ref.shape[0]
    fill = jnp.zeros((TOK_BLK - 8, GATHER_TOK), F32)
    for c in range(nchunk):
        row = jnp.broadcast_to(slot_ref[:, c * GATHER_TOK:(c + 1) * GATHER_TOK].astype(F32), (8, GATHER_TOK))
        col_scr[c] = jnp.concatenate([row, fill], axis=0).T
    lane = lax.broadcasted_iota(jnp.int32, (GATHER_TOK, GATHER_SLOT), 1).astype(F32)
    win = min(GATHER_WIN, nchunk)

    def onehot_t(c, want):
        return jnp.where(col_scr[c][:, 0:1] == want, 1.0, 0.0).astype(BF16)

    for sb in range(nsb):
        want = lane + float(sb * GATHER_SLOT)
        first = jnp.minimum(lo_ref[e * nsb + sb], nchunk - win)
        acc = _dot(h2t_ref[first], onehot_t(first, want))
        for i in range(1, win):
            acc = acc + _dot(h2t_ref[first + i], onehot_t(first + i, want))
        acc_scr[...] = acc

        def body(c, carry):
            acc_scr[...] += _dot(h2t_ref[c], onehot_t(c, want))
            return carry

        lax.fori_loop(first + win, hi_ref[e * nsb + sb] + 1, body, 0)
        xe_scr[sb * GATHER_SLOT:(sb + 1) * GATHER_SLOT, :] = acc_scr[...].T.astype(BF16)


def _expert_kernel(lo_ref, hi_ref, slot_ref, h2t_ref, w1_ref, w3_ref, w2_ref, ye_ref,
                   xe_scr, acc_scr, col_scr, y_scr, *, cap):
    e = pl.program_id(0)
    f = pl.program_id(1)

    @pl.when(f == 0)
    def _():
        _gather_rows(e, lo_ref, hi_ref, slot_ref, h2t_ref, xe_scr, acc_scr, col_scr, cap)

    x = xe_scr[...]
    a = _dot(x, w1_ref[...].astype(BF16))
    b = _dot(x, w3_ref[...].astype(BF16))
    hid = (a * _sigmoid(a) * b).astype(BF16)
    part = _dot(hid, w2_ref[...].astype(BF16))

    @pl.when(f == 0)
    def _():
        y_scr[...] = part

    @pl.when(f == 1)
    def _():
        ye_ref[...] = (y_scr[...] + part).astype(BF16)


def _combine_kernel(win_ref, slot_ref, gate_ref, x_ref, g2_ref, ye_ref, o_ref, acc_scr):
    tb = pl.program_id(0)
    lane = lax.broadcasted_iota(jnp.int32, (TOK_BLK, 2 * SLOT_BLK), 1)
    fill = jnp.zeros((TOK_BLK - N_EXPERTS, TOK_BLK), F32)
    slot_t = jnp.concatenate([slot_ref[...].astype(F32), fill], axis=0).T
    gate_t = jnp.concatenate([gate_ref[...], fill], axis=0).T
    acc_scr[...] = jnp.zeros_like(acc_scr)
    for e in range(N_EXPERTS):
        s0 = pl.multiple_of(win_ref[tb * N_EXPERTS + e] * SLOT_BLK, SLOT_BLK)
        want = (lane + s0).astype(F32)
        onehot = jnp.where(slot_t[:, e:e + 1] == want, 1.0, 0.0).astype(BF16)
        acc_scr[...] += _dot(onehot, ye_ref[e, pl.ds(s0, 2 * SLOT_BLK), :]) * gate_t[:, e:e + 1]
    o_ref[...] = x_ref[...] + g2_ref[...] * acc_scr[...]


def _moe(h2, aff_t, x, g2, w1, w3, w2, layer, seq_len):
    n = x.shape[0]
    cap = EC_FACTOR * n // N_EXPERTS
    nsb = cap // SLOT_BLK
    ntb = n // TOK_BLK
    gate, slot, starts = _select(aff_t)

    cs = starts[:, :ntb]
    win = jnp.minimum(cs // SLOT_BLK, nsb - 2).T.astype(jnp.int32)
    gs = cs[:, ::GATHER_TOK // TOK_BLK]
    gend = jnp.concatenate([gs[:, 1:], jnp.full((N_EXPERTS, 1), cap, jnp.int32)], axis=1)
    edges = jnp.arange(cap // GATHER_SLOT, dtype=jnp.int32) * GATHER_SLOT
    lo = jnp.sum(gend[:, None, :] <= edges[None, :, None], axis=-1).astype(jnp.int32)
    hi = jnp.sum(gs[:, None, :] < (edges + GATHER_SLOT)[None, :, None], axis=-1).astype(jnp.int32) - 1

    d_ff = w1.shape[-1]
    up_spec = pl.BlockSpec((None, None, D, d_ff // 2), lambda e, f, *_: (layer, e, 0, f))
    down_spec = pl.BlockSpec((None, None, d_ff // 2, D), lambda e, f, *_: (layer, e, f, 0))
    ye = pl.pallas_call(
        functools.partial(_expert_kernel, cap=cap),
        grid_spec=pltpu.PrefetchScalarGridSpec(
            num_scalar_prefetch=2,
            grid=(N_EXPERTS, 2),
            in_specs=[pl.BlockSpec((None, 1, n), lambda e, f, *_: (e, 0, 0)),
                      pl.BlockSpec((n // GATHER_TOK, D, GATHER_TOK), lambda e, f, *_: (0, 0, 0),
                                   pipeline_mode=pl.Buffered(1)),
                      up_spec, up_spec, down_spec],
            out_specs=pl.BlockSpec((None, cap, D), lambda e, f, *_: (e, 0, 0)),
            scratch_shapes=[pltpu.VMEM((cap, D), BF16), pltpu.VMEM((D, GATHER_SLOT), F32),
                            pltpu.VMEM((n // GATHER_TOK, GATHER_TOK, TOK_BLK), F32),
                            pltpu.VMEM((cap, D), F32)]),
        out_shape=jax.ShapeDtypeStruct((N_EXPERTS, cap, D), BF16),
        compiler_params=_cparams(("arbitrary", "arbitrary")),
    )(lo.reshape(-1), hi.reshape(-1), slot.reshape(N_EXPERTS, 1, n), h2, w1, w3, w2)

    groups = g2.shape[0]
    if groups == 1:
        gidx = lambda i, *_: (0, 0, 0)
    else:
        gidx = lambda i, *_: ((i * TOK_BLK) // seq_len, 0, 0)
    return pl.pallas_call(
        _combine_kernel,
        grid_spec=pltpu.PrefetchScalarGridSpec(
            num_scalar_prefetch=1,
            grid=(ntb,),
            in_specs=[pl.BlockSpec((N_EXPERTS, TOK_BLK), lambda i, *_: (0, i)),
                      pl.BlockSpec((N_EXPERTS, TOK_BLK), lambda i, *_: (0, i)),
                      pl.BlockSpec((TOK_BLK, D), lambda i, *_: (i, 0)),
                      pl.BlockSpec((None, 1, D), gidx),
                      pl.BlockSpec((N_EXPERTS, cap, D), lambda i, *_: (0, 0, 0), pipeline_mode=pl.Buffered(1))],
            out_specs=pl.BlockSpec((TOK_BLK, D), lambda i, *_: (i, 0)),
            scratch_shapes=[pltpu.VMEM((TOK_BLK, D), F32)]),
        out_shape=jax.ShapeDtypeStruct((n, D), F32),
        compiler_params=_cparams(("arbitrary",)),
    )(win.reshape(-1), slot, gate, x, g2, ye)


def _final_kernel(x_ref, g_ref, o_ref):
    x = x_ref[...]
    o_ref[...] = x * lax.rsqrt(jnp.mean(x * x, axis=-1, keepdims=True) + EPS) * g_ref[...]


def _final_norm(x, g):
    n = x.shape[0]
    tm = 1024
    return pl.pallas_call(
        _final_kernel,
        grid=(n // tm,),
        in_specs=[pl.BlockSpec((tm, D), lambda i: (i, 0)), pl.BlockSpec((1, D), lambda i: (0, 0))],
        out_specs=pl.BlockSpec((tm, D), lambda i: (i, 0)),
        out_shape=jax.ShapeDtypeStruct((n, D), F32),
        compiler_params=_cparams(("arbitrary",)),
    )(x, g.reshape(1, D))


def _layer(x, mod, prm, batch, seq_len, caug0, m0, h0_t, ck=None, cv=None, bias=None, kv_stack=None,
           state_stack=None):
    st_layer, st_depth, st_cm, st_h = state_stack or (0, 1, None, None)
    sh1, sc1, g1, sh2, sc2, g2 = [m[:, None, :] for m in jnp.split(mod, 6, axis=-1)]
    attn_dtype = F32 if ck is None else BF16
    qkv, mo, sz, xbc, nq, nk, nv, gate, *kv_out = _in_projection(
        x, prm["norm1_g"], sh1, sc1, prm["w_in"], prm["conv_w"], prm["conv_b"], seq_len, attn_dtype, kv_stack)
    hf, hb, cfin, mfin = _mlstm_scan(qkv, gate, prm["f_bias"], caug0, m0, batch, seq_len, st_layer, st_depth, st_cm)
    yf, yb, hfin = _ssd_scan(xbc, gate, prm["dt_bias"], prm["a_log"], h0_t, batch, seq_len, st_layer, st_depth, st_h)
    if ck is None:
        a_out = _ctx_attention(nq, nk, nv, batch, seq_len)
    else:
        a_out = _neighbourhood_attention(nq, nk, nv, ck, cv, bias, batch, seq_len)
    xn, h2, aff_t = _out_projection(x, hf, hb, mo, yf, yb, xbc, sz, a_out, prm["mnorm_g"], prm["d_skip"],
                                    prm["snorm_g"], prm["w_out"], g1, prm["norm2_g"], sh2, sc2,
                                    prm["router_wt"], seq_len)
    xo = _moe(h2, aff_t, xn, g2, prm["w1"], prm["w3"], prm["w2"], prm["layer"], seq_len)
    return xo, kv_out, cfin, mfin, hfin


def _pack_w_in(w_in):
    o = np.cumsum((0, D_M, D_M, D_M, D_M, 2 * H_M, 2 * H_M, D_S, D_XBC, 2 * H_S, D_NA, D_NA, D_NA))
    pad = jnp.zeros(w_in.shape[:-1] + (GATE_W - 4 * H_M - 2 * H_S,), w_in.dtype)
    parts = [w_in[..., o[0]:o[4]], w_in[..., o[6]:o[8]], w_in[..., o[9]:o[12]],
             w_in[..., o[4]:o[6]], w_in[..., o[8]:o[9]], pad]
    return jnp.concatenate(parts, axis=-1).astype(BF16)


def kernel(x_prompt, x_sample, cache_na_k, cache_na_v, state_mlstm_c, state_mlstm_n, state_mlstm_m, state_ssm, c, c_ctx, ada_w, ada_b, norm1_g, norm2_g, w_in, mlstm_f_bias, mlstm_norm_g, conv_w, conv_b, ssm_dt_bias, ssm_a_log, ssm_d, ssm_norm_g, na_rpb, w_out, router_w, exp_w1, exp_w3, exp_w2, final_g):
    bp, sp, _ = x_prompt.shape
    bs, ss, _ = x_sample.shape
    past = cache_na_k.shape[2]

    cvec = jnp.zeros((8, D), F32).at[0].set(c_ctx).at[1:1 + bs].set(c)
    mod = _modulation(cvec, ada_w, ada_b)

    w_in_p = _pack_w_in(w_in)
    w_out_bf = w_out.astype(BF16)
    router_wt = jnp.swapaxes(router_w, 1, 2)

    def aug(cs, ns):
        pad = jnp.zeros(cs.shape[:-2] + (128 - HD - 1, HD), F32)
        out = jnp.concatenate([jnp.swapaxes(cs, -1, -2), ns[..., None, :], pad], axis=-2)
        return out.reshape(cs.shape[:2] + (2 * H_M, 128, HD))

    lat_caug_all = aug(state_mlstm_c, state_mlstm_n)

    def rep_m(ms):
        b = ms.shape[0]
        return jnp.broadcast_to(ms.reshape(b, 2 * H_M, 1, 1), (b, 2 * H_M, 8, 128)).astype(F32)

    ctx_caug0 = jnp.zeros((bp, 2 * H_M, 128, HD), F32)
    ctx_m0 = jnp.full((bp, 2 * H_M, 8, 128), NEG_INIT, F32)
    ctx_h0 = jnp.zeros((bp, 2 * H_S, N_S, HD), F32)

    xp = x_prompt.reshape(bp * sp, D)
    xs = x_sample.reshape(bs * ss, D)
    k_stack = v_stack = cm_stack = h_stack = None
    for l in range(DEPTH):
        prm = dict(norm1_g=norm1_g[l], norm2_g=norm2_g[l], w_in=w_in_p[l], f_bias=mlstm_f_bias[l],
                   mnorm_g=mlstm_norm_g[l], conv_w=conv_w[l], conv_b=conv_b[l], dt_bias=ssm_dt_bias[l],
                   a_log=ssm_a_log[l], d_skip=ssm_d[l], snorm_g=ssm_norm_g[l], w_out=w_out_bf[l],
                   router_wt=router_wt[l], w1=exp_w1, w3=exp_w3, w2=exp_w2, layer=l)
        xp, (k_stack, v_stack), c_all, m_all, h_all = _layer(
            xp, mod[l, 0:1], prm, bp, sp, ctx_caug0, ctx_m0, ctx_h0, kv_stack=(l, DEPTH, k_stack, v_stack),
            state_stack=(l, DEPTH, cm_stack, h_stack))
        cm_stack, h_stack = (c_all, m_all), (h_all,)

        lat_caug0 = lat_caug_all[:, l]
        lat_m0 = rep_m(state_mlstm_m[:, l])
        lat_h0 = state_ssm[:, l].reshape(bs, 2 * H_S, HD, N_S)
        ck = cache_na_k[:, l].reshape(bs * past, D_NA).astype(BF16)
        cv = cache_na_v[:, l].reshape(bs * past, D_NA).astype(BF16)
        bias = _natten_bias(na_rpb[l], ss // GRID_W)
        xs = _layer(xs, mod[l, 1:1 + bs], prm, bs, ss, lat_caug0, lat_m0, lat_h0, ck, cv, bias)[0]

    y_prompt = _final_norm(xp, final_g).reshape(bp, sp, D)
    y_sample = _final_norm(xs, final_g).reshape(bs, ss, D)
    new_c = jnp.swapaxes(c_all[..., 0:HD, :], -1, -2).reshape(bp, DEPTH, 2, H_M, HD, HD)
    new_n = c_all[..., HD, :].reshape(bp, DEPTH, 2, H_M, HD)
    new_m = m_all[..., 0, 0].reshape(bp, DEPTH, 2, H_M)
    return (y_prompt, y_sample, k_stack.reshape(bp, DEPTH, sp, H_NA, HD), v_stack.reshape(bp, DEPTH, sp, H_NA, HD),
            new_c, new_n, new_m, h_all.reshape(bp, DEPTH, 2, H_S, HD, N_S))
```

```python
import functools

import numpy as np
import jax
import jax.numpy as jnp
from jax import lax
from jax.experimental import pallas as pl
from jax.experimental.pallas import tpu as pltpu

F32 = jnp.float32
BF16 = jnp.bfloat16
HI = lax.Precision.HIGHEST

D = 1024
DEPTH = 4
HD = 64
H_M = 4
D_M = H_M * HD
H_S = 6
D_S = H_S * HD
G_S = 2
R_S = H_S // G_S
N_S = 64
D_XBC = D_S + 2 * G_S * N_S
H_NA = 6
D_NA = H_NA * HD
GRID_W = 64
WIN_H = 8
WIN_W = 16
N_EXPERTS = 16
EC_FACTOR = 2
EPS = 1e-6
NEG_INIT = -1e30
TOK_BLK = 128
SLOT_BLK = 128
COMBINE_BLKS = 2
GATHER_TOK = 256
GATHER_SLOT = 256
GATHER_WIN = 10
GATE_W = 128
MI0, MF0, DT0 = 0, 2 * H_M, 4 * H_M
W_PACKED = 3 * D_M + D_M + D_S + D_XBC + 3 * D_NA + GATE_W

VMEM_LIMIT = 56 * 1024 * 1024


def _cparams(sem):
    return pltpu.CompilerParams(dimension_semantics=sem, vmem_limit_bytes=VMEM_LIMIT)


def _sigmoid(x):
    return 1.0 / (1.0 + jnp.exp(-x))


def _softplus(x):
    return jnp.maximum(x, 0.0) + jnp.log1p(jnp.exp(-jnp.abs(x)))


def _dot(a, b):
    return jnp.dot(a, b, preferred_element_type=F32)


def _dot_nt(a, b):
    return lax.dot_general(a, b, (((1,), (1,)), ((), ())), preferred_element_type=F32)


def _dot_tn(a, b):
    return lax.dot_general(a, b, (((0,), (0,)), ((), ())), preferred_element_type=F32)


def _mod_kernel(c_ref, w_ref, b_ref, o_ref):
    cv = c_ref[...]
    s = cv * _sigmoid(cv)
    o_ref[...] = jnp.dot(s, w_ref[...], precision=HI, preferred_element_type=F32) + b_ref[...]


def _modulation(cvec, ada_w, ada_b):
    tn = 1536
    return pl.pallas_call(
        _mod_kernel,
        grid=(DEPTH, 6 * D // tn),
        in_specs=[pl.BlockSpec((8, D), lambda l, j: (0, 0)),
                  pl.BlockSpec((None, D, tn), lambda l, j: (l, 0, j)),
                  pl.BlockSpec((None, 1, tn), lambda l, j: (l, 0, j))],
        out_specs=pl.BlockSpec((None, 8, tn), lambda l, j: (l, 0, j)),
        out_shape=jax.ShapeDtypeStruct((DEPTH, 8, 6 * D), F32),
        compiler_params=_cparams(("arbitrary", "arbitrary")),
    )(cvec, ada_w, ada_b.reshape(DEPTH, 1, 6 * D))


_IN_SEGS = (("qkv", 0, 768), ("mo", 768, 1024), ("sz", 1024, 1408), ("xbc", 1408, 2048),
            ("nq", 2048, 2432), ("nk", 2432, 2816), ("nv", 2816, 3200), ("gate", 3200, 3328))


def _inproj_kernel(x_ref, xp_ref, xn_ref, g_ref, sh_ref, sc_ref, w_ref, cw_ref, cb_ref, *rest,
                   n_alias, n_seg, seq_len):
    outs = rest[n_alias:]
    tm = x_ref.shape[0]

    def norm_mod(x):
        ms = jnp.mean(x * x, axis=-1, keepdims=True)
        h = x * lax.rsqrt(ms + EPS) * g_ref[...]
        return (h * (1.0 + sc_ref[...]) + sh_ref[...]).astype(BF16)

    hb = norm_mod(x_ref[...])
    vals = {}
    for o_ref, (name, lo, hi) in zip(outs[:n_seg], _IN_SEGS):
        vals[name] = _dot(hb, w_ref[:, lo:hi])
        if name == "xbc":
            cur = vals[name]
            edge = _dot(norm_mod(jnp.concatenate([xp_ref[...], xn_ref[...]], axis=0)), w_ref[:, lo:hi])
            rows = lax.broadcasted_iota(jnp.int32, cur.shape, 0)
            if seq_len >= tm:
                i = pl.program_id(0)
                tps = seq_len // tm
                starts = (rows == 0) & ((i % tps) == 0)
                ends = (rows == tm - 1) & ((i % tps) == tps - 1)
            else:
                starts = functools.reduce(jnp.logical_or, [rows == k * seq_len for k in range(tm // seq_len)])
                ends = functools.reduce(jnp.logical_or, [rows == (k + 1) * seq_len - 1 for k in range(tm // seq_len)])
            prev = jnp.where(rows == 0, edge[7:8, :], pltpu.roll(cur, 1, 0))
            nxt = jnp.where(rows == tm - 1, edge[8:9, :], pltpu.roll(cur, tm - 1, 0))
            prev = jnp.where(starts, 0.0, prev)
            nxt = jnp.where(ends, 0.0, nxt)
            y = prev * cw_ref[0:1, :] + cur * cw_ref[1:2, :] + nxt * cw_ref[2:3, :] + cb_ref[...]
            vals[name] = y * _sigmoid(y)
        o_ref[...] = vals[name].astype(o_ref.dtype)
    for o_ref, name in zip(outs[n_seg:], ("nk", "nv")):
        o_ref[...] = vals[name].reshape(o_ref.shape)


def _in_projection(x, norm_g, shift, scale, w_packed, conv_w, conv_b, seq_len, attn_dtype, kv_stack=None):
    n = x.shape[0]
    tm = 512
    groups = shift.shape[0]
    if groups == 1:
        gidx = lambda i: (0, 0, 0)
    else:
        gidx = lambda i: ((i * tm) // seq_len, 0, 0)
    out_shape = [jax.ShapeDtypeStruct((n, hi - lo), attn_dtype if name in ("nq", "nk", "nv") else F32)
                 for name, lo, hi in _IN_SEGS]
    out_specs = [pl.BlockSpec((tm, hi - lo), lambda i: (i, 0)) for _, lo, hi in _IN_SEGS]
    assert seq_len % tm == 0 or tm % seq_len == 0
    nb8 = n // 8
    in_specs = [pl.BlockSpec((tm, D), lambda i: (i, 0)),
                pl.BlockSpec((8, D), lambda i: (jnp.maximum(i * (tm // 8) - 1, 0), 0)),
                pl.BlockSpec((8, D), lambda i: (jnp.minimum((i + 1) * (tm // 8), nb8 - 1), 0)),
                pl.BlockSpec((1, D), lambda i: (0, 0)),
                pl.BlockSpec((None, 1, D), gidx),
                pl.BlockSpec((None, 1, D), gidx),
                pl.BlockSpec((D, W_PACKED), lambda i: (0, 0)),
                pl.BlockSpec((3, D_XBC), lambda i: (0, 0)),
                pl.BlockSpec((1, D_XBC), lambda i: (0, 0))]
    args = [x, x, x, norm_g.reshape(1, D), shift, scale, w_packed, conv_w, conv_b.reshape(1, D_XBC)]
    aliases = {}
    n_alias = 0
    if kv_stack is not None:
        layer, depth, k_prev, v_prev = kv_stack
        bpt = tm // seq_len
        stack_shape = jax.ShapeDtypeStruct((n // seq_len, depth, seq_len, D_NA), F32)
        for prev in (k_prev, v_prev):
            out_shape.append(stack_shape)
            out_specs.append(pl.BlockSpec((bpt, None, seq_len, D_NA), lambda i: (i, layer, 0, 0)))
            if prev is not None:
                aliases[len(args)] = len(out_shape) - 1
                in_specs.append(pl.BlockSpec(memory_space=pl.ANY))
                args.append(prev)
                n_alias += 1
    return pl.pallas_call(
        functools.partial(_inproj_kernel, n_alias=n_alias, n_seg=len(_IN_SEGS), seq_len=seq_len),
        grid=(n // tm,),
        in_specs=in_specs,
        out_specs=tuple(out_specs),
        out_shape=tuple(out_shape),
        input_output_aliases=aliases,
        compiler_params=_cparams(("arbitrary",)),
    )(*args)


def _tri_masks(ch):
    r = lax.broadcasted_iota(jnp.int32, (ch, ch), 0)
    c = lax.broadcasted_iota(jnp.int32, (ch, ch), 1)
    return r >= c, r <= c


def _mlstm_kernel(qkv_f, qkv_b, gt_f, gt_b, fb_ref, c0_ref, m0_ref,
                  hf_ref, hb_ref, cfin_ref, mfin_ref, c_scr, m_scr, it_scr, bt_scr, rc_scr, vt_scr,
                  kq_scr, in_scr, *, n_tiles, ch):
    j = pl.program_id(1)

    @pl.when(j == 0)
    def _():
        c_scr[...] = c0_ref[...]
        m_scr[...] = m0_ref[...]

    tt = qkv_f.shape[0]
    nch = tt // ch
    fb = fb_ref[...]
    low, upp = _tri_masks(ch)
    ones_row = (lax.broadcasted_iota(jnp.int32, (HD, ch), 0) == 0).astype(F32)
    pad_rows = jnp.zeros((ch - 2 * H_M, ch), F32)

    dc = [(d, c) for d in range(2) for c in range(nch)]
    i_rows, logf_rows, b_rows = {}, {}, {}
    for d, c in dc:
        z_t = ((gt_f, gt_b)[d][c * ch:(c + 1) * ch, :] + fb).T
        i_rows[d, c] = z_t[MI0:MI0 + 2 * H_M, :]
        zf = z_t[MF0:MF0 + 2 * H_M, :]
        logf_rows[d, c] = jnp.minimum(zf, 0.0) - jnp.log1p(jnp.exp(-jnp.abs(zf)))
    for d, c in dc:
        mask = (upp, low)[d]
        b_rows[d, c] = jnp.dot(logf_rows[d, c], mask.astype(F32), precision=HI, preferred_element_type=F32)
        it_scr[d, c] = i_rows[d, c]
        bt_scr[d, c] = b_rows[d, c]
    for d, c in dc:
        rc_scr[d, c] = jnp.concatenate([i_rows[d, c] - b_rows[d, c], pad_rows], axis=0).T
    for d, c in dc:
        for p in range(H_M // 2):
            vt_scr[d, c, p * 2 * HD:(p + 1) * 2 * HD, :] = (
                (qkv_f, qkv_b)[d][c * ch:(c + 1) * ch, 2 * D_M + p * 2 * HD:2 * D_M + (p + 1) * 2 * HD].T)

    def chunk(ci, carry):
        for d in range(2):
            qkv_ref = (qkv_f, qkv_b)[d]
            c0 = pl.multiple_of((ci if d == 0 else nch - 1 - ci) * ch, ch)
            for h in range(H_M):
                idx = d * H_M + h
                q = qkv_ref[pl.ds(c0, ch), h * HD:(h + 1) * HD].astype(BF16)
                k = (qkv_ref[pl.ds(c0, ch), D_M + h * HD:D_M + (h + 1) * HD] * (HD ** -0.5)).astype(BF16)
                kq_scr[idx] = _dot_nt(k, q)
                in_scr[idx] = _dot_nt(c_scr[idx].astype(BF16), q)
        for d in range(2):
            qkv_ref = (qkv_f, qkv_b)[d]
            out_ref = (hf_ref, hb_ref)[d]
            mask = (upp, low)[d]
            cidx = ci if d == 0 else nch - 1 - ci
            c0 = pl.multiple_of(cidx * ch, ch)
            i_t = it_scr[d, cidx]
            b_t = bt_scr[d, cidx]
            r_cols = rc_scr[d, cidx]
            edge = ch - 1 if d == 0 else 0
            h_t = []
            for h in range(H_M):
                idx = d * H_M + h
                vaug = jnp.concatenate([vt_scr[d, cidx, h * HD:(h + 1) * HD, :], ones_row], axis=0)
                k = (qkv_ref[pl.ds(c0, ch), D_M + h * HD:D_M + (h + 1) * HD] * (HD ** -0.5)).astype(BF16)
                b_row = b_t[idx:idx + 1, :]
                i_row = i_t[idx:idx + 1, :]
                m_prev = m_scr[idx][0:1, 0:1]
                caug = c_scr[idx]
                logd = jnp.where(mask, r_cols[:, idx:idx + 1] + b_row, -jnp.inf)
                m_inter = b_row + m_prev
                m_t = jnp.maximum(jnp.max(logd, axis=0, keepdims=True), m_inter)
                dmat = jnp.exp(logd - m_t)
                w_inter = jnp.exp(m_inter - m_t)
                s = (kq_scr[idx] * dmat).astype(BF16)
                tot = _dot(vaug.astype(BF16), s) + w_inter * in_scr[idx]
                den = jnp.maximum(jnp.abs(tot[HD:HD + 1, :]), jnp.exp(-m_t))
                h_t.append(tot[0:HD, :] / den)
                if h % 2 == 1:
                    pair = jnp.concatenate(h_t[-2:], axis=0).T
                    out_ref[pl.ds(c0, ch), (h - 1) * HD:(h + 1) * HD] = pair
                b_l = b_row[:, edge:edge + 1]
                lw = b_l - b_row + i_row
                m_new = jnp.maximum(b_l + m_prev, jnp.max(lw, axis=1, keepdims=True))
                wk = jnp.exp(lw - m_new)
                decay = jnp.exp(b_l + m_prev - m_new)
                c_scr[idx] = decay * caug + _dot((vaug * wk).astype(BF16), k)
                m_scr[idx] = jnp.broadcast_to(m_new, (8, 128))
        return carry

    lax.fori_loop(0, nch, chunk, 0)

    @pl.when(j == n_tiles - 1)
    def _():
        cfin_ref[...] = c_scr[...]
        mfin_ref[...] = m_scr[...]


def _mlstm_scan(qkv, gate, f_bias, caug0, m0, batch, seq_len, layer=0, depth=1, prev=None):
    n = qkv.shape[0]
    tt = min(seq_len, 512)
    nt = seq_len // tt
    ch = 128
    fb = jnp.zeros((1, GATE_W), F32).at[0, MF0:MF0 + 2 * H_M].set(f_bias.reshape(-1))
    fwd = lambda b, j: (b * nt + j, 0)
    bwd = lambda b, j: (b * nt + nt - 1 - j, 0)
    st4 = lambda b, j: (b, 0, 0, 0)
    st5 = lambda b, j: (b, layer, 0, 0, 0)
    n_in = 7
    prev = tuple(prev or ())
    kern = functools.partial(_mlstm_kernel, n_tiles=nt, ch=ch)
    return pl.pallas_call(
        lambda *refs: kern(*refs[:n_in], *refs[n_in + len(prev):]),
        grid=(batch, nt),
        in_specs=[pl.BlockSpec((tt, 3 * D_M), fwd), pl.BlockSpec((tt, 3 * D_M), bwd),
                  pl.BlockSpec((tt, GATE_W), fwd), pl.BlockSpec((tt, GATE_W), bwd),
                  pl.BlockSpec((1, GATE_W), lambda b, j: (0, 0)),
                  pl.BlockSpec((None, 2 * H_M, 128, HD), st4),
                  pl.BlockSpec((None, 2 * H_M, 8, 128), st4)] + [pl.BlockSpec(memory_space=pl.ANY)] * len(prev),
        out_specs=(pl.BlockSpec((tt, D_M), fwd), pl.BlockSpec((tt, D_M), bwd),
                   pl.BlockSpec((None, None, 2 * H_M, 128, HD), st5),
                   pl.BlockSpec((None, None, 2 * H_M, 8, 128), st5)),
        out_shape=(jax.ShapeDtypeStruct((n, D_M), F32), jax.ShapeDtypeStruct((n, D_M), F32),
                   jax.ShapeDtypeStruct((batch, depth, 2 * H_M, 128, HD), F32),
                   jax.ShapeDtypeStruct((batch, depth, 2 * H_M, 8, 128), F32)),
        input_output_aliases={n_in + i: 2 + i for i in range(len(prev))},
        scratch_shapes=[pltpu.VMEM((2 * H_M, 128, HD), F32), pltpu.VMEM((2 * H_M, 8, 128), F32),
                        pltpu.VMEM((2, tt // ch, 2 * H_M, ch), F32), pltpu.VMEM((2, tt // ch, 2 * H_M, ch), F32),
                        pltpu.VMEM((2, tt // ch, ch, 128), F32), pltpu.VMEM((2, tt // ch, D_M, ch), F32),
                        pltpu.VMEM((2 * H_M, ch, ch), F32), pltpu.VMEM((2 * H_M, 128, ch), F32)],
        compiler_params=_cparams(("arbitrary", "arbitrary")),
    )(qkv, qkv, gate, gate, fb, caug0, m0, *prev)


def _ssd_kernel(xbc_f, xbc_b, gt_f, gt_b, dtb_ref, alog_ref, h0_ref,
                yf_ref, yb_ref, hfin_ref, h_scr, dtt_scr, at_scr, ac_scr, xt_scr, cb_scr, in_scr, *, n_tiles, ch):
    j = pl.program_id(1)

    @pl.when(j == 0)
    def _():
        h_scr[...] = h0_ref[...]

    tt = xbc_f.shape[0]
    nch = tt // ch
    dtb = dtb_ref[...]
    a_neg = -jnp.exp(alog_ref[...])
    low, upp = _tri_masks(ch)
    nrow = 16
    pad_rows = jnp.zeros((ch - nrow, ch), F32)

    dc = [(d, c) for d in range(2) for c in range(nch)]
    da_rows, a_rows = {}, {}
    for d, c in dc:
        dt = _softplus((gt_f, gt_b)[d][c * ch:(c + 1) * ch, :] + dtb)
        dtt_scr[d, c] = dt.T[DT0:DT0 + nrow, :]
        da_rows[d, c] = (dt * a_neg).T[DT0:DT0 + nrow, :]
    for d, c in dc:
        mask = (upp, low)[d]
        a_rows[d, c] = jnp.dot(da_rows[d, c], mask.astype(F32), precision=HI, preferred_element_type=F32)
        at_scr[d, c] = a_rows[d, c]
    for d, c in dc:
        ac_scr[d, c] = jnp.concatenate([a_rows[d, c], pad_rows], axis=0).T
    for d, c in dc:
        for p in range(H_S // 2):
            xt_scr[d, c, p * 2 * HD:(p + 1) * 2 * HD, :] = (
                (xbc_f, xbc_b)[d][c * ch:(c + 1) * ch, p * 2 * HD:(p + 1) * 2 * HD].T)

    def chunk(ci, carry):
        for d in range(2):
            x_ref = (xbc_f, xbc_b)[d]
            c0 = pl.multiple_of((ci if d == 0 else nch - 1 - ci) * ch, ch)
            for grp in range(G_S):
                bm = x_ref[pl.ds(c0, ch), D_S + grp * N_S:D_S + (grp + 1) * N_S].astype(BF16)
                cm = x_ref[pl.ds(c0, ch), D_S + G_S * N_S + grp * N_S:D_S + G_S * N_S + (grp + 1) * N_S].astype(BF16)
                cb_scr[d * G_S + grp] = _dot_nt(bm, cm)
                for r in range(R_S):
                    idx = d * H_S + grp * R_S + r
                    in_scr[idx] = _dot_nt(h_scr[idx].astype(BF16), cm)
        for d in range(2):
            x_ref = (xbc_f, xbc_b)[d]
            out_ref = (yf_ref, yb_ref)[d]
            mask = (upp, low)[d]
            cidx = ci if d == 0 else nch - 1 - ci
            c0 = pl.multiple_of(cidx * ch, ch)
            dt_t = dtt_scr[d, cidx]
            a_t = at_scr[d, cidx]
            a_cols = ac_scr[d, cidx]
            edge = ch - 1 if d == 0 else 0
            y_t = []
            for grp in range(G_S):
                bm = x_ref[pl.ds(c0, ch), D_S + grp * N_S:D_S + (grp + 1) * N_S].astype(BF16)
                for r in range(R_S):
                    hh = grp * R_S + r
                    idx = d * H_S + hh
                    a_row = a_t[idx:idx + 1, :]
                    dt_row = dt_t[idx:idx + 1, :]
                    seg = jnp.exp(jnp.where(mask, a_row - a_cols[:, idx:idx + 1], -jnp.inf))
                    mm = (cb_scr[d * G_S + grp] * seg).astype(BF16)
                    xh = xt_scr[d, cidx, hh * HD:(hh + 1) * HD, :]
                    h_st = h_scr[idx]
                    y_t.append(_dot((xh * dt_row).astype(BF16), mm) + in_scr[idx] * jnp.exp(a_row))
                    if hh % 2 == 1:
                        pair = jnp.concatenate(y_t[-2:], axis=0).T
                        out_ref[pl.ds(c0, ch), (hh - 1) * HD:(hh + 1) * HD] = pair
                    a_l = a_row[:, edge:edge + 1]
                    wk = jnp.exp(a_l - a_row) * dt_row
                    h_scr[idx] = jnp.exp(a_l) * h_st + _dot((xh * wk).astype(BF16), bm)
        return carry

    lax.fori_loop(0, nch, chunk, 0)

    @pl.when(j == n_tiles - 1)
    def _():
        hfin_ref[...] = h_scr[...]


def _ssd_scan(xbc, gate, dt_bias, a_log, h0_t, batch, seq_len, layer=0, depth=1, prev=None):
    n = xbc.shape[0]
    tt = min(seq_len, 512)
    nt = seq_len // tt
    ch = 128
    dtb = jnp.zeros((1, GATE_W), F32).at[0, DT0:DT0 + 2 * H_S].set(dt_bias.reshape(-1))
    alog = jnp.zeros((1, GATE_W), F32).at[0, DT0:DT0 + 2 * H_S].set(a_log.reshape(-1))
    fwd = lambda b, j: (b * nt + j, 0)
    bwd = lambda b, j: (b * nt + nt - 1 - j, 0)
    st4 = lambda b, j: (b, 0, 0, 0)
    n_in = 7
    prev = tuple(prev or ())
    kern = functools.partial(_ssd_kernel, n_tiles=nt, ch=ch)
    return pl.pallas_call(
        lambda *refs: kern(*refs[:n_in], *refs[n_in + len(prev):]),
        grid=(batch, nt),
        in_specs=[pl.BlockSpec((tt, D_XBC), fwd), pl.BlockSpec((tt, D_XBC), bwd),
                  pl.BlockSpec((tt, GATE_W), fwd), pl.BlockSpec((tt, GATE_W), bwd),
                  pl.BlockSpec((1, GATE_W), lambda b, j: (0, 0)),
                  pl.BlockSpec((1, GATE_W), lambda b, j: (0, 0)),
                  pl.BlockSpec((None, 2 * H_S, N_S, HD), st4)] + [pl.BlockSpec(memory_space=pl.ANY)] * len(prev),
        out_specs=(pl.BlockSpec((tt, D_S), fwd), pl.BlockSpec((tt, D_S), bwd),
                   pl.BlockSpec((None, None, 2 * H_S, N_S, HD), lambda b, j: (b, layer, 0, 0, 0))),
        out_shape=(jax.ShapeDtypeStruct((n, D_S), F32), jax.ShapeDtypeStruct((n, D_S), F32),
                   jax.ShapeDtypeStruct((batch, depth, 2 * H_S, N_S, HD), F32)),
        input_output_aliases={n_in + i: 2 + i for i in range(len(prev))},
        scratch_shapes=[pltpu.VMEM((2 * H_S, N_S, HD), F32),
                        pltpu.VMEM((2, tt // ch, 16, ch), F32), pltpu.VMEM((2, tt // ch, 16, ch), F32),
                        pltpu.VMEM((2, tt // ch, ch, 128), F32), pltpu.VMEM((2, tt // ch, D_S, ch), F32),
                        pltpu.VMEM((2 * G_S, ch, ch), F32), pltpu.VMEM((2 * H_S, HD, ch), F32)],
        compiler_params=_cparams(("arbitrary", "arbitrary")),
    )(xbc, xbc, gate, gate, dtb, alog, h0_t, *prev)


def _ctx_attn_kernel(q_ref, k_ref, v_ref, o_ref, s_scr):
    n = q_ref.shape[0]
    first = lax.broadcasted_iota(jnp.int32, (n, 2 * HD), 1) < HD
    ones = jnp.ones((n, 2 * HD), BF16)
    for p in range(H_NA // 2):
        sl = slice(p * 2 * HD, (p + 1) * 2 * HD)
        qp = q_ref[:, sl]
        kp = k_ref[:, sl].astype(BF16)
        for half in range(2):
            mine = first if half == 0 else jnp.logical_not(first)
            q = (jnp.where(mine, qp, jnp.zeros_like(qp)) * (HD ** -0.5)).astype(BF16)
            s_scr[2 * p + half] = _dot_nt(q, kp)
    for p in range(H_NA // 2):
        sl = slice(p * 2 * HD, (p + 1) * 2 * HD)
        vp = v_ref[:, sl].astype(BF16)
        outs = []
        for half in range(2):
            s = s_scr[2 * p + half]
            pr = jnp.exp(s - jnp.max(s, axis=-1, keepdims=True)).astype(BF16)
            outs.append(_dot(pr, vp) / _dot(pr, ones))
        o_ref[:, sl] = jnp.where(first, outs[0], outs[1])


def _ctx_attention(q, k, v, batch, seq_len):
    spec = pl.BlockSpec((seq_len, D_NA), lambda b: (b, 0))
    return pl.pallas_call(
        _ctx_attn_kernel,
        grid=(batch,),
        in_specs=[spec, spec, spec],
        out_specs=spec,
        out_shape=jax.ShapeDtypeStruct(q.shape, F32),
        scratch_shapes=[pltpu.VMEM((H_NA, seq_len, seq_len), F32)],
        compiler_params=_cparams(("arbitrary",)),
    )(q, k, v)


def _row_start(r, rows):
    return jnp.clip(r - WIN_H // 2, 0, rows - WIN_H)


def _natten_kernel(q_ref, k_ref, v_ref, ck_ref, cv_ref, bias_ref, o_ref, sl_scr, sc_scr, *, rows):
    r = pl.program_id(1)
    k0 = pl.multiple_of(_row_start(r, rows) * GRID_W, GRID_W)
    off0 = _row_start(r, rows) - r + WIN_H - 1
    nloc = WIN_H * GRID_W
    npast = ck_ref.shape[0]
    first = lax.broadcasted_iota(jnp.int32, (GRID_W, 2 * HD), 1) < HD
    ones_loc = jnp.ones((nloc, 2 * HD), BF16)
    ones_ctx = jnp.ones((npast, 2 * HD), BF16)
    for p in range(H_NA // 2):
        sl = slice(p * 2 * HD, (p + 1) * 2 * HD)
        qp = q_ref[:, sl]
        kw = k_ref[pl.ds(k0, nloc), sl].astype(BF16)
        ckp = ck_ref[:, sl].astype(BF16)
        for half in range(2):
            mine = first if half == 0 else jnp.logical_not(first)
            q = (jnp.where(mine, qp, jnp.zeros_like(qp)) * (HD ** -0.5)).astype(BF16)
            bias = jnp.concatenate([bias_ref[2 * p + half, off0 + 2 * i] for i in range(WIN_H // 2)], axis=1)
            sl_scr[2 * p + half] = _dot_nt(q, kw) + bias
            sc_scr[2 * p + half] = _dot_nt(q, ckp)
    for p in range(H_NA // 2):
        sl = slice(p * 2 * HD, (p + 1) * 2 * HD)
        vw = v_ref[pl.ds(k0, nloc), sl].astype(BF16)
        cvp = cv_ref[:, sl].astype(BF16)
        outs = []
        for half in range(2):
            s_loc = sl_scr[2 * p + half]
            s_ctx = sc_scr[2 * p + half]
            m = jnp.maximum(jnp.max(s_loc, axis=-1, keepdims=True), jnp.max(s_ctx, axis=-1, keepdims=True))
            p_loc = jnp.exp(s_loc - m).astype(BF16)
            p_ctx = jnp.exp(s_ctx - m).astype(BF16)
            l = _dot(p_loc, ones_loc) + _dot(p_ctx, ones_ctx)
            outs.append((_dot(p_loc, vw) + _dot(p_ctx, cvp)) / l)
        o_ref[:, sl] = jnp.where(first, outs[0], outs[1])


def _natten_bias(rpb, rows):
    qc = np.arange(GRID_W)[:, None]
    kc = np.arange(GRID_W)[None, :]
    cstart = np.clip(qc - WIN_W // 2, 0, GRID_W - WIN_W)
    ok = (kc >= cstart) & (kc < cstart + WIN_W)
    col_off = np.clip(kc - qc + WIN_W - 1, 0, 2 * WIN_W - 2)
    pick = jnp.asarray((col_off[None] == np.arange(2 * WIN_W - 1)[:, None, None]).astype(np.float32))
    toep = jnp.einsum("hoj,jqk->hoqk", rpb.astype(F32), pick, precision=HI)
    t = jnp.where(jnp.asarray(ok)[None, None], toep, -jnp.inf)
    return jnp.concatenate([t[:, :-1], t[:, 1:]], axis=-1)


def _neighbourhood_attention(q, k, v, ck, cv, bias, batch, seq_len):
    rows = seq_len // GRID_W
    past = ck.shape[0] // batch
    return pl.pallas_call(
        functools.partial(_natten_kernel, rows=rows),
        grid=(batch, rows),
        in_specs=[pl.BlockSpec((GRID_W, D_NA), lambda b, r: (b * rows + r, 0)),
                  pl.BlockSpec((seq_len, D_NA), lambda b, r: (b, 0)),
                  pl.BlockSpec((seq_len, D_NA), lambda b, r: (b, 0)),
                  pl.BlockSpec((past, D_NA), lambda b, r: (b, 0)),
                  pl.BlockSpec((past, D_NA), lambda b, r: (b, 0)),
                  pl.BlockSpec((H_NA, 2 * WIN_H - 2, GRID_W, 2 * GRID_W), lambda b, r: (0, 0, 0, 0))],
        out_specs=pl.BlockSpec((GRID_W, D_NA), lambda b, r: (b * rows + r, 0)),
        out_shape=jax.ShapeDtypeStruct(q.shape, F32),
        scratch_shapes=[pltpu.VMEM((H_NA, GRID_W, WIN_H * GRID_W), F32), pltpu.VMEM((H_NA, GRID_W, past), F32)],
        compiler_params=_cparams(("arbitrary", "arbitrary")),
    )(q, k, v, ck, cv, bias)


def _outproj_kernel(x_ref, hf_ref, hb_ref, mo_ref, yf_ref, yb_ref, xbc_ref, sz_ref, a_ref,
                    mg_ref, dsk_ref, sg_ref, bd_ref, w_ref, g1_ref, n2_ref, sh2_ref, sc2_ref, rw_ref,
                    xo_ref, h2_ref, aff_ref):
    hm = hf_ref[...] + hb_ref[...]
    ssq = jnp.dot(hm * hm, bd_ref[...], precision=HI, preferred_element_type=F32)
    m_out = hm * lax.rsqrt(ssq * (1.0 / HD) + EPS) * mg_ref[...] * _sigmoid(mo_ref[...])
    sz = sz_ref[...]
    ys = (yf_ref[...] + yb_ref[...] + dsk_ref[...] * xbc_ref[:, 0:D_S]) * (sz * _sigmoid(sz))
    s_out = ys * lax.rsqrt(jnp.mean(ys * ys, axis=-1, keepdims=True) + EPS) * sg_ref[...]
    y = (_dot(m_out.astype(BF16), w_ref[0:D_M, :])
         + _dot(s_out.astype(BF16), w_ref[D_M:D_M + D_S, :])
         + _dot(a_ref[...].astype(BF16), w_ref[D_M + D_S:D, :]))
    xn = x_ref[...] + g1_ref[...] * y
    xo_ref[...] = xn
    h2 = xn * lax.rsqrt(jnp.mean(xn * xn, axis=-1, keepdims=True) + EPS) * n2_ref[...]
    h2 = h2 * (1.0 + sc2_ref[...]) + sh2_ref[...]
    for c in range(h2_ref.shape[0]):
        h2_ref[c] = h2[c * GATHER_TOK:(c + 1) * GATHER_TOK, :].T.astype(BF16)
    logits = lax.dot_general(rw_ref[...], h2, (((1,), (1,)), ((), ())), precision=HI,
                             preferred_element_type=F32)
    mx = jnp.max(logits, axis=0, keepdims=True)
    ex = jnp.exp(logits - mx)
    aff_ref[...] = ex / jnp.sum(ex, axis=0, keepdims=True)


def _out_projection(x, hf, hb, mo, yf, yb, xbc, sz, a_out, mnorm_g, d_skip, snorm_g, w_out_bf,
                    g1, norm2_g, sh2, sc2, router_wt, seq_len):
    n = x.shape[0]
    tm = 512
    cpt = tm // GATHER_TOK
    groups = g1.shape[0]
    if groups == 1:
        gidx = lambda i: (0, 0, 0)
    else:
        gidx = lambda i: ((i * tm) // seq_len, 0, 0)
    row = lambda w: pl.BlockSpec((tm, w), lambda i: (i, 0))
    const = lambda s: pl.BlockSpec(s, lambda i: (0,) * len(s))
    hid = np.arange(D_M) // HD
    blockdiag = jnp.asarray((hid[:, None] == hid[None, :]).astype(np.float32))
    return pl.pallas_call(
        _outproj_kernel,
        grid=(n // tm,),
        in_specs=[row(D), row(D_M), row(D_M), row(D_M), row(D_S), row(D_S), row(D_XBC), row(D_S), row(D_NA),
                  const((1, D_M)), const((1, D_S)), const((1, D_S)), const((D_M, D_M)), const((D, D)),
                  pl.BlockSpec((None, 1, D), gidx), const((1, D)),
                  pl.BlockSpec((None, 1, D), gidx), pl.BlockSpec((None, 1, D), gidx),
                  const((N_EXPERTS, D))],
        out_specs=(row(D), pl.BlockSpec((cpt, D, GATHER_TOK), lambda i: (i, 0, 0)),
                   pl.BlockSpec((N_EXPERTS, tm), lambda i: (0, i))),
        out_shape=(jax.ShapeDtypeStruct((n, D), F32), jax.ShapeDtypeStruct((n // GATHER_TOK, D, GATHER_TOK), BF16),
                   jax.ShapeDtypeStruct((N_EXPERTS, n), F32)),
        compiler_params=_cparams(("arbitrary",)),
    )(x, hf, hb, mo, yf, yb, xbc, sz, a_out,
      mnorm_g.reshape(1, D_M), jnp.repeat(d_skip, HD).reshape(1, D_S), snorm_g.reshape(1, D_S), blockdiag,
      w_out_bf, g1, norm2_g.reshape(1, D), sh2, sc2, router_wt)


def _select_kernel(aff_ref, gate_ref, slot_ref, start_ref, *, cap):
    aff = aff_ref[...]
    bits = pltpu.bitcast(aff, jnp.int32)
    n_tok = aff.shape[1]

    def step(i, prefix):
        cand = prefix | (jnp.int32(1) << (30 - i))
        cnt = jnp.sum((bits >= cand).astype(F32), axis=1, keepdims=True)
        return jnp.where(cnt >= cap, cand, prefix)

    thr = lax.fori_loop(0, 31, step, jnp.zeros((aff.shape[0], 1), jnp.int32))
    gt = bits > thr
    eq = bits == thr
    need = cap - jnp.sum(gt.astype(F32), axis=1, keepdims=True)
    r_i = lax.broadcasted_iota(jnp.int32, (128, 128), 0)
    c_i = lax.broadcasted_iota(jnp.int32, (128, 128), 1)
    strict = (r_i < c_i).astype(BF16)
    run_eq = jnp.zeros((aff.shape[0], 1), F32)
    run_sel = jnp.zeros((aff.shape[0], 1), F32)
    lane = lax.broadcasted_iota(jnp.int32, (aff.shape[0], 128), 1)
    starts = jnp.zeros((aff.shape[0], 128), F32)
    for blk in range(n_tok // TOK_BLK):
        sl = slice(blk * TOK_BLK, (blk + 1) * TOK_BLK)
        e = eq[:, sl]
        rank = _dot(e.astype(BF16), strict) + run_eq
        keep = gt[:, sl] | (e & (rank < need))
        kf = jnp.where(keep, 1.0, 0.0)
        slot = _dot(kf.astype(BF16), strict) + run_sel
        gate_ref[:, sl] = jnp.where(keep, aff[:, sl], 0.0)
        slot_ref[:, sl] = jnp.where(keep, slot, -1.0).astype(jnp.int32)
        starts = jnp.where(lane == blk, run_sel, starts)
        run_eq = run_eq + jnp.sum(e.astype(F32), axis=1, keepdims=True)
        run_sel = run_sel + jnp.sum(kf, axis=1, keepdims=True)
    start_ref[...] = starts.astype(jnp.int32)


def _select(aff_t):
    n = aff_t.shape[1]
    cap = EC_FACTOR * n // N_EXPERTS
    assert n // TOK_BLK <= 128
    return pl.pallas_call(
        functools.partial(_select_kernel, cap=float(cap)),
        out_shape=(jax.ShapeDtypeStruct(aff_t.shape, F32), jax.ShapeDtypeStruct(aff_t.shape, jnp.int32),
                   jax.ShapeDtypeStruct((N_EXPERTS, 128), jnp.int32)),
        compiler_params=pltpu.CompilerParams(vmem_limit_bytes=VMEM_LIMIT),
    )(aff_t)


def _gather_rows(e, lo_ref, hi_ref, slot_ref, h2t_ref, xe_scr, acc_scr, col_scr, cap):
    nsb = cap // GATHER_SLOT
    nchunk = h2t_ref.shape[0]
    fill = jnp.zeros((TOK_BLK - 8, GATHER_TOK), F32)
    for c in range(nchunk):
        row = jnp.broadcast_to(slot_ref[:, c * GATHER_TOK:(c + 1) * GATHER_TOK].astype(F32), (8, GATHER_TOK))
        col_scr[c] = jnp.concatenate([row, fill], axis=0).T
    lane = lax.broadcasted_iota(jnp.int32, (GATHER_TOK, GATHER_SLOT), 1).astype(F32)
    win = min(GATHER_WIN, nchunk)

    def onehot_t(c, want):
        return jnp.where(col_scr[c][:, 0:1] == want, 1.0, 0.0).astype(BF16)

    for sb in range(nsb):
        want = lane + float(sb * GATHER_SLOT)
        first = jnp.minimum(lo_ref[e * nsb + sb], nchunk - win)
        acc = _dot(h2t_ref[first], onehot_t(first, want))
        for i in range(1, win):
            acc = acc + _dot(h2t_ref[first + i], onehot_t(first + i, want))
        acc_scr[...] = acc

        def body(c, carry):
            acc_scr[...] += _dot(h2t_ref[c], onehot_t(c, want))
            return carry

        lax.fori_loop(first + win, hi_ref[e * nsb + sb] + 1, body, 0)
        xe_scr[sb * GATHER_SLOT:(sb + 1) * GATHER_SLOT, :] = acc_scr[...].T.astype(BF16)


def _expert_kernel(lo_ref, hi_ref, slot_ref, h2t_ref, w1_ref, w3_ref, w2_ref, ye_ref,
                   xe_scr, acc_scr, col_scr, y_scr, *, cap):
    e = pl.program_id(0)
    f = pl.program_id(1)

    @pl.when(f == 0)
    def _():
        _gather_rows(e, lo_ref, hi_ref, slot_ref, h2t_ref, xe_scr, acc_scr, col_scr, cap)

    x = xe_scr[...]
    a = _dot(x, w1_ref[...].astype(BF16))
    b = _dot(x, w3_ref[...].astype(BF16))
    hid = (a * _sigmoid(a) * b).astype(BF16)
    part = _dot(hid, w2_ref[...].astype(BF16))

    @pl.when(f == 0)
    def _():
        y_scr[...] = part

    @pl.when(f == 1)
    def _():
        ye_ref[...] = (y_scr[...] + part).astype(BF16)


def _combine_kernel(win_ref, slot_ref, gate_ref, x_ref, g2_ref, fg_ref, ye_ref, o_ref, acc_scr, *, final):
    lane = lax.broadcasted_iota(jnp.int32, (TOK_BLK, 2 * SLOT_BLK), 1)
    fill = jnp.zeros((TOK_BLK - N_EXPERTS, TOK_BLK), F32)
    for s in range(COMBINE_BLKS):
        tb = pl.program_id(0) * COMBINE_BLKS + s
        rows = slice(s * TOK_BLK, (s + 1) * TOK_BLK)
        slot_t = jnp.concatenate([slot_ref[:, rows].astype(F32), fill], axis=0).T
        gate_t = jnp.concatenate([gate_ref[:, rows], fill], axis=0).T
        acc_scr[...] = jnp.zeros_like(acc_scr)
        for e in range(N_EXPERTS):
            s0 = pl.multiple_of(win_ref[tb * N_EXPERTS + e] * SLOT_BLK, SLOT_BLK)
            want = (lane + s0).astype(F32)
            onehot = jnp.where(slot_t[:, e:e + 1] == want, 1.0, 0.0).astype(BF16)
            acc_scr[...] += _dot(onehot, ye_ref[e, pl.ds(s0, 2 * SLOT_BLK), :]) * gate_t[:, e:e + 1]
        y = x_ref[rows, :] + g2_ref[...] * acc_scr[...]
        if final:
            y = y * lax.rsqrt(jnp.mean(y * y, axis=-1, keepdims=True) + EPS) * fg_ref[...]
        o_ref[rows, :] = y


def _moe(h2, aff_t, x, g2, w1, w3, w2, layer, seq_len, final_g=None):
    n = x.shape[0]
    cap = EC_FACTOR * n // N_EXPERTS
    nsb = cap // SLOT_BLK
    ntb = n // TOK_BLK
    gate, slot, starts = _select(aff_t)

    cs = starts[:, :ntb]
    win = jnp.minimum(cs // SLOT_BLK, nsb - 2).T.astype(jnp.int32)
    gs = cs[:, ::GATHER_TOK // TOK_BLK]
    gend = jnp.concatenate([gs[:, 1:], jnp.full((N_EXPERTS, 1), cap, jnp.int32)], axis=1)
    edges = jnp.arange(cap // GATHER_SLOT, dtype=jnp.int32) * GATHER_SLOT
    lo = jnp.sum(gend[:, None, :] <= edges[None, :, None], axis=-1).astype(jnp.int32)
    hi = jnp.sum(gs[:, None, :] < (edges + GATHER_SLOT)[None, :, None], axis=-1).astype(jnp.int32) - 1

    d_ff = w1.shape[-1]
    up_spec = pl.BlockSpec((None, None, D, d_ff // 2), lambda e, f, *_: (layer, e, 0, f))
    down_spec = pl.BlockSpec((None, None, d_ff // 2, D), lambda e, f, *_: (layer, e, f, 0))
    ye = pl.pallas_call(
        functools.partial(_expert_kernel, cap=cap),
        grid_spec=pltpu.PrefetchScalarGridSpec(
            num_scalar_prefetch=2,
            grid=(N_EXPERTS, 2),
            in_specs=[pl.BlockSpec((None, 1, n), lambda e, f, *_: (e, 0, 0)),
                      pl.BlockSpec((n // GATHER_TOK, D, GATHER_TOK), lambda e, f, *_: (0, 0, 0),
                                   pipeline_mode=pl.Buffered(1)),
                      up_spec, up_spec, down_spec],
            out_specs=pl.BlockSpec((None, cap, D), lambda e, f, *_: (e, 0, 0)),
            scratch_shapes=[pltpu.VMEM((cap, D), BF16), pltpu.VMEM((D, GATHER_SLOT), F32),
                            pltpu.VMEM((n // GATHER_TOK, GATHER_TOK, TOK_BLK), F32),
                            pltpu.VMEM((cap, D), F32)]),
        out_shape=jax.ShapeDtypeStruct((N_EXPERTS, cap, D), BF16),
        compiler_params=_cparams(("arbitrary", "arbitrary")),
    )(lo.reshape(-1), hi.reshape(-1), slot.reshape(N_EXPERTS, 1, n), h2, w1, w3, w2)

    groups = g2.shape[0]
    if groups == 1:
        gidx = lambda i, *_: (0, 0, 0)
    else:
        gidx = lambda i, *_: ((i * tc) // seq_len, 0, 0)
    tc = COMBINE_BLKS * TOK_BLK
    fg = jnp.ones((1, D), F32) if final_g is None else final_g.reshape(1, D)
    return pl.pallas_call(
        functools.partial(_combine_kernel, final=final_g is not None),
        grid_spec=pltpu.PrefetchScalarGridSpec(
            num_scalar_prefetch=1,
            grid=(n // tc,),
            in_specs=[pl.BlockSpec((N_EXPERTS, tc), lambda i, *_: (0, i)),
                      pl.BlockSpec((N_EXPERTS, tc), lambda i, *_: (0, i)),
                      pl.BlockSpec((tc, D), lambda i, *_: (i, 0)),
                      pl.BlockSpec((None, 1, D), gidx),
                      pl.BlockSpec((1, D), lambda i, *_: (0, 0)),
                      pl.BlockSpec((N_EXPERTS, cap, D), lambda i, *_: (0, 0, 0), pipeline_mode=pl.Buffered(1))],
            out_specs=pl.BlockSpec((tc, D), lambda i, *_: (i, 0)),
            scratch_shapes=[pltpu.VMEM((TOK_BLK, D), F32)]),
        out_shape=jax.ShapeDtypeStruct((n, D), F32),
        compiler_params=_cparams(("arbitrary",)),
    )(win.reshape(-1), slot, gate, x, g2, fg, ye)


def _layer(x, mod, prm, batch, seq_len, caug0, m0, h0_t, ck=None, cv=None, bias=None, kv_stack=None,
           state_stack=None):
    st_layer, st_depth, st_cm, st_h = state_stack or (0, 1, None, None)
    sh1, sc1, g1, sh2, sc2, g2 = [m[:, None, :] for m in jnp.split(mod, 6, axis=-1)]
    attn_dtype = F32 if ck is None else BF16
    qkv, mo, sz, xbc, nq, nk, nv, gate, *kv_out = _in_projection(
        x, prm["norm1_g"], sh1, sc1, prm["w_in"], prm["conv_w"], prm["conv_b"], seq_len, attn_dtype, kv_stack)
    hf, hb, cfin, mfin = _mlstm_scan(qkv, gate, prm["f_bias"], caug0, m0, batch, seq_len, st_layer, st_depth, st_cm)
    yf, yb, hfin = _ssd_scan(xbc, gate, prm["dt_bias"], prm["a_log"], h0_t, batch, seq_len, st_layer, st_depth, st_h)
    if ck is None:
        a_out = _ctx_attention(nq, nk, nv, batch, seq_len)
    else:
        a_out = _neighbourhood_attention(nq, nk, nv, ck, cv, bias, batch, seq_len)
    xn, h2, aff_t = _out_projection(x, hf, hb, mo, yf, yb, xbc, sz, a_out, prm["mnorm_g"], prm["d_skip"],
                                    prm["snorm_g"], prm["w_out"], g1, prm["norm2_g"], sh2, sc2,
                                    prm["router_wt"], seq_len)
    xo = _moe(h2, aff_t, xn, g2, prm["w1"], prm["w3"], prm["w2"], prm["layer"], seq_len, prm["final_g"])
    return xo, kv_out, cfin, mfin, hfin


def _pack_w_in(w_in):
    o = np.cumsum((0, D_M, D_M, D_M, D_M, 2 * H_M, 2 * H_M, D_S, D_XBC, 2 * H_S, D_NA, D_NA, D_NA))
    pad = jnp.zeros(w_in.shape[:-1] + (GATE_W - 4 * H_M - 2 * H_S,), w_in.dtype)
    parts = [w_in[..., o[0]:o[4]], w_in[..., o[6]:o[8]], w_in[..., o[9]:o[12]],
             w_in[..., o[4]:o[6]], w_in[..., o[8]:o[9]], pad]
    return jnp.concatenate(parts, axis=-1).astype(BF16)


def kernel(x_prompt, x_sample, cache_na_k, cache_na_v, state_mlstm_c, state_mlstm_n, state_mlstm_m, state_ssm, c, c_ctx, ada_w, ada_b, norm1_g, norm2_g, w_in, mlstm_f_bias, mlstm_norm_g, conv_w, conv_b, ssm_dt_bias, ssm_a_log, ssm_d, ssm_norm_g, na_rpb, w_out, router_w, exp_w1, exp_w3, exp_w2, final_g):
    bp, sp, _ = x_prompt.shape
    bs, ss, _ = x_sample.shape
    past = cache_na_k.shape[2]

    cvec = jnp.zeros((8, D), F32).at[0].set(c_ctx).at[1:1 + bs].set(c)
    mod = _modulation(cvec, ada_w, ada_b)

    w_in_p = _pack_w_in(w_in)
    w_out_bf = w_out.astype(BF16)
    router_wt = jnp.swapaxes(router_w, 1, 2)

    def aug(cs, ns):
        pad = jnp.zeros(cs.shape[:-2] + (128 - HD - 1, HD), F32)
        out = jnp.concatenate([jnp.swapaxes(cs, -1, -2), ns[..., None, :], pad], axis=-2)
        return out.reshape(cs.shape[:2] + (2 * H_M, 128, HD))

    lat_caug_all = aug(state_mlstm_c, state_mlstm_n)

    def rep_m(ms):
        b = ms.shape[0]
        return jnp.broadcast_to(ms.reshape(b, 2 * H_M, 1, 1), (b, 2 * H_M, 8, 128)).astype(F32)

    ctx_caug0 = jnp.zeros((bp, 2 * H_M, 128, HD), F32)
    ctx_m0 = jnp.full((bp, 2 * H_M, 8, 128), NEG_INIT, F32)
    ctx_h0 = jnp.zeros((bp, 2 * H_S, N_S, HD), F32)

    xp = x_prompt.reshape(bp * sp, D)
    xs = x_sample.reshape(bs * ss, D)
    k_stack = v_stack = cm_stack = h_stack = None
    for l in range(DEPTH):
        prm = dict(norm1_g=norm1_g[l], norm2_g=norm2_g[l], w_in=w_in_p[l], f_bias=mlstm_f_bias[l],
                   mnorm_g=mlstm_norm_g[l], conv_w=conv_w[l], conv_b=conv_b[l], dt_bias=ssm_dt_bias[l],
                   a_log=ssm_a_log[l], d_skip=ssm_d[l], snorm_g=ssm_norm_g[l], w_out=w_out_bf[l],
                   router_wt=router_wt[l], w1=exp_w1, w3=exp_w3, w2=exp_w2, layer=l,
                   final_g=final_g if l == DEPTH - 1 else None)
        xp, (k_stack, v_stack), c_all, m_all, h_all = _layer(
            xp, mod[l, 0:1], prm, bp, sp, ctx_caug0, ctx_m0, ctx_h0, kv_stack=(l, DEPTH, k_stack, v_stack),
            state_stack=(l, DEPTH, cm_stack, h_stack))
        cm_stack, h_stack = (c_all, m_all), (h_all,)

        lat_caug0 = lat_caug_all[:, l]
        lat_m0 = rep_m(state_mlstm_m[:, l])
        lat_h0 = state_ssm[:, l].reshape(bs, 2 * H_S, HD, N_S)
        ck = cache_na_k[:, l].reshape(bs * past, D_NA).astype(BF16)
        cv = cache_na_v[:, l].reshape(bs * past, D_NA).astype(BF16)
        bias = _natten_bias(na_rpb[l], ss // GRID_W)
        xs = _layer(xs, mod[l, 1:1 + bs], prm, bs, ss, lat_caug0, lat_m0, lat_h0, ck, cv, bias)[0]

    y_prompt = xp.reshape(bp, sp, D)
    y_sample = xs.reshape(bs, ss, D)
    new_c = jnp.swapaxes(c_all[..., 0:HD, :], -1, -2).reshape(bp, DEPTH, 2, H_M, HD, HD)
    new_n = c_all[..., HD, :].reshape(bp, DEPTH, 2, H_M, HD)
    new_m = m_all[..., 0, 0].reshape(bp, DEPTH, 2, H_M)
    return (y_prompt, y_sample, k_stack.reshape(bp, DEPTH, sp, H_NA, HD), v_stack.reshape(bp, DEPTH, sp, H_NA, HD),
            new_c, new_n, new_m, h_all.reshape(bp, DEPTH, 2, H_S, HD, N_S))
```

```python
import functools

import numpy as np
import jax
import jax.numpy as jnp
from jax import lax
from jax.experimental import pallas as pl
from jax.experimental.pallas import tpu as pltpu

F32 = jnp.float32
BF16 = jnp.bfloat16
HI = lax.Precision.HIGHEST

D = 1024
DEPTH = 4
HD = 64
H_M = 4
D_M = H_M * HD
H_S = 6
D_S = H_S * HD
G_S = 2
R_S = H_S // G_S
N_S = 64
D_XBC = D_S + 2 * G_S * N_S
H_NA = 6
D_NA = H_NA * HD
GRID_W = 64
WIN_H = 8
WIN_W = 16
N_EXPERTS = 16
EC_FACTOR = 2
EPS = 1e-6
NEG_INIT = -1e30
TOK_BLK = 128
SLOT_BLK = 128
COMBINE_BLKS = 2
GATHER_TOK = 256
GATHER_SLOT = 256
GATHER_WIN = 10
GATE_W = 128
MI0, MF0, DT0 = 0, 2 * H_M, 4 * H_M
W_PACKED = 3 * D_M + D_M + D_S + D_XBC + 3 * D_NA + GATE_W

VMEM_LIMIT = 56 * 1024 * 1024


def _cparams(sem):
    return pltpu.CompilerParams(dimension_semantics=sem, vmem_limit_bytes=VMEM_LIMIT)


def _sigmoid(x):
    return 1.0 / (1.0 + jnp.exp(-x))


def _softplus(x):
    return jnp.maximum(x, 0.0) + jnp.log1p(jnp.exp(-jnp.abs(x)))


def _dot(a, b):
    return jnp.dot(a, b, preferred_element_type=F32)


def _dot_nt(a, b):
    return lax.dot_general(a, b, (((1,), (1,)), ((), ())), preferred_element_type=F32)


def _dot_tn(a, b):
    return lax.dot_general(a, b, (((0,), (0,)), ((), ())), preferred_element_type=F32)


def _mod_kernel(c_ref, w_ref, b_ref, o_ref):
    cv = c_ref[...]
    s = cv * _sigmoid(cv)
    o_ref[...] = jnp.dot(s, w_ref[...], precision=HI, preferred_element_type=F32) + b_ref[...]


def _modulation(cvec, ada_w, ada_b):
    tn = 1536
    return pl.pallas_call(
        _mod_kernel,
        grid=(DEPTH, 6 * D // tn),
        in_specs=[pl.BlockSpec((8, D), lambda l, j: (0, 0)),
                  pl.BlockSpec((None, D, tn), lambda l, j: (l, 0, j)),
                  pl.BlockSpec((None, 1, tn), lambda l, j: (l, 0, j))],
        out_specs=pl.BlockSpec((None, 8, tn), lambda l, j: (l, 0, j)),
        out_shape=jax.ShapeDtypeStruct((DEPTH, 8, 6 * D), F32),
        compiler_params=_cparams(("arbitrary", "arbitrary")),
    )(cvec, ada_w, ada_b.reshape(DEPTH, 1, 6 * D))


_IN_SEGS = (("qkv", 0, 768), ("mo", 768, 1024), ("sz", 1024, 1408), ("xbc", 1408, 2048),
            ("nq", 2048, 2432), ("nk", 2432, 2816), ("nv", 2816, 3200), ("gate", 3200, 3328))


def _inproj_kernel(x_ref, xp_ref, xn_ref, g_ref, sh_ref, sc_ref, w_ref, cw_ref, cb_ref, *rest,
                   n_alias, n_seg, seq_len):
    outs = rest[n_alias:]
    tm = x_ref.shape[0]

    def norm_mod(x):
        ms = jnp.mean(x * x, axis=-1, keepdims=True)
        h = x * lax.rsqrt(ms + EPS) * g_ref[...]
        return (h * (1.0 + sc_ref[...]) + sh_ref[...]).astype(BF16)

    hb = norm_mod(x_ref[...])
    vals = {}
    for o_ref, (name, lo, hi) in zip(outs[:n_seg], _IN_SEGS):
        vals[name] = _dot(hb, w_ref[:, lo:hi])
        if name == "xbc":
            cur = vals[name]
            edge = _dot(norm_mod(jnp.concatenate([xp_ref[...], xn_ref[...]], axis=0)), w_ref[:, lo:hi])
            rows = lax.broadcasted_iota(jnp.int32, cur.shape, 0)
            if seq_len >= tm:
                i = pl.program_id(0)
                tps = seq_len // tm
                starts = (rows == 0) & ((i % tps) == 0)
                ends = (rows == tm - 1) & ((i % tps) == tps - 1)
            else:
                starts = functools.reduce(jnp.logical_or, [rows == k * seq_len for k in range(tm // seq_len)])
                ends = functools.reduce(jnp.logical_or, [rows == (k + 1) * seq_len - 1 for k in range(tm // seq_len)])
            prev = jnp.where(rows == 0, edge[7:8, :], pltpu.roll(cur, 1, 0))
            nxt = jnp.where(rows == tm - 1, edge[8:9, :], pltpu.roll(cur, tm - 1, 0))
            prev = jnp.where(starts, 0.0, prev)
            nxt = jnp.where(ends, 0.0, nxt)
            y = prev * cw_ref[0:1, :] + cur * cw_ref[1:2, :] + nxt * cw_ref[2:3, :] + cb_ref[...]
            vals[name] = y * _sigmoid(y)
        o_ref[...] = vals[name].astype(o_ref.dtype)
    for o_ref, name in zip(outs[n_seg:], ("nk", "nv")):
        o_ref[...] = vals[name].reshape(o_ref.shape)


def _in_projection(x, norm_g, shift, scale, w_packed, conv_w, conv_b, seq_len, attn_dtype, kv_stack=None):
    n = x.shape[0]
    tm = 512
    groups = shift.shape[0]
    if groups == 1:
        gidx = lambda i: (0, 0, 0)
    else:
        gidx = lambda i: ((i * tm) // seq_len, 0, 0)
    out_shape = [jax.ShapeDtypeStruct((n, hi - lo), attn_dtype if name in ("nq", "nk", "nv") else F32)
                 for name, lo, hi in _IN_SEGS]
    out_specs = [pl.BlockSpec((tm, hi - lo), lambda i: (i, 0)) for _, lo, hi in _IN_SEGS]
    assert seq_len % tm == 0 or tm % seq_len == 0
    nb8 = n // 8
    in_specs = [pl.BlockSpec((tm, D), lambda i: (i, 0)),
                pl.BlockSpec((8, D), lambda i: (jnp.maximum(i * (tm // 8) - 1, 0), 0)),
                pl.BlockSpec((8, D), lambda i: (jnp.minimum((i + 1) * (tm // 8), nb8 - 1), 0)),
                pl.BlockSpec((1, D), lambda i: (0, 0)),
                pl.BlockSpec((None, 1, D), gidx),
                pl.BlockSpec((None, 1, D), gidx),
                pl.BlockSpec((D, W_PACKED), lambda i: (0, 0)),
                pl.BlockSpec((3, D_XBC), lambda i: (0, 0)),
                pl.BlockSpec((1, D_XBC), lambda i: (0, 0))]
    args = [x, x, x, norm_g.reshape(1, D), shift, scale, w_packed, conv_w, conv_b.reshape(1, D_XBC)]
    aliases = {}
    n_alias = 0
    if kv_stack is not None:
        layer, depth, k_prev, v_prev = kv_stack
        bpt = tm // seq_len
        stack_shape = jax.ShapeDtypeStruct((n // seq_len, depth, seq_len, D_NA), F32)
        for prev in (k_prev, v_prev):
            out_shape.append(stack_shape)
            out_specs.append(pl.BlockSpec((bpt, None, seq_len, D_NA), lambda i: (i, layer, 0, 0)))
            if prev is not None:
                aliases[len(args)] = len(out_shape) - 1
                in_specs.append(pl.BlockSpec(memory_space=pl.ANY))
                args.append(prev)
                n_alias += 1
    return pl.pallas_call(
        functools.partial(_inproj_kernel, n_alias=n_alias, n_seg=len(_IN_SEGS), seq_len=seq_len),
        grid=(n // tm,),
        in_specs=in_specs,
        out_specs=tuple(out_specs),
        out_shape=tuple(out_shape),
        input_output_aliases=aliases,
        compiler_params=_cparams(("arbitrary",)),
    )(*args)


def _tri_masks(ch):
    r = lax.broadcasted_iota(jnp.int32, (ch, ch), 0)
    c = lax.broadcasted_iota(jnp.int32, (ch, ch), 1)
    return r >= c, r <= c


def _mlstm_kernel(qkv_f, qkv_b, gt_f, gt_b, fb_ref, c0_ref, m0_ref,
                  hf_ref, hb_ref, cfin_ref, mfin_ref, c_scr, m_scr, it_scr, bt_scr, rc_scr, vt_scr,
                  kq_scr, in_scr, *, n_tiles, ch):
    j = pl.program_id(1)

    @pl.when(j == 0)
    def _():
        c_scr[...] = c0_ref[...]
        m_scr[...] = m0_ref[...]

    tt = qkv_f.shape[0]
    nch = tt // ch
    fb = fb_ref[...]
    low, upp = _tri_masks(ch)
    ones_row = (lax.broadcasted_iota(jnp.int32, (HD, ch), 0) == 0).astype(F32)
    pad_rows = jnp.zeros((ch - 2 * H_M, ch), F32)

    dc = [(d, c) for d in range(2) for c in range(nch)]
    i_rows, logf_rows, b_rows = {}, {}, {}
    for d, c in dc:
        z_t = ((gt_f, gt_b)[d][c * ch:(c + 1) * ch, :] + fb).T
        i_rows[d, c] = z_t[MI0:MI0 + 2 * H_M, :]
        zf = z_t[MF0:MF0 + 2 * H_M, :]
        logf_rows[d, c] = jnp.minimum(zf, 0.0) - jnp.log1p(jnp.exp(-jnp.abs(zf)))
    for d, c in dc:
        mask = (upp, low)[d]
        b_rows[d, c] = jnp.dot(logf_rows[d, c], mask.astype(F32), precision=HI, preferred_element_type=F32)
        it_scr[d, c] = i_rows[d, c]
        bt_scr[d, c] = b_rows[d, c]
    for d, c in dc:
        rc_scr[d, c] = jnp.concatenate([i_rows[d, c] - b_rows[d, c], pad_rows], axis=0).T
    for d, c in dc:
        for p in range(H_M // 2):
            vt_scr[d, c, p * 2 * HD:(p + 1) * 2 * HD, :] = (
                (qkv_f, qkv_b)[d][c * ch:(c + 1) * ch, 2 * D_M + p * 2 * HD:2 * D_M + (p + 1) * 2 * HD].T)

    def chunk(ci, carry):
        for d in range(2):
            qkv_ref = (qkv_f, qkv_b)[d]
            c0 = pl.multiple_of((ci if d == 0 else nch - 1 - ci) * ch, ch)
            for h in range(H_M):
                idx = d * H_M + h
                q = qkv_ref[pl.ds(c0, ch), h * HD:(h + 1) * HD].astype(BF16)
                k = (qkv_ref[pl.ds(c0, ch), D_M + h * HD:D_M + (h + 1) * HD] * (HD ** -0.5)).astype(BF16)
                kq_scr[idx] = _dot_nt(k, q)
                in_scr[idx] = _dot_nt(c_scr[idx].astype(BF16), q)
        for d in range(2):
            qkv_ref = (qkv_f, qkv_b)[d]
            out_ref = (hf_ref, hb_ref)[d]
            mask = (upp, low)[d]
            cidx = ci if d == 0 else nch - 1 - ci
            c0 = pl.multiple_of(cidx * ch, ch)
            i_t = it_scr[d, cidx]
            b_t = bt_scr[d, cidx]
            r_cols = rc_scr[d, cidx]
            edge = ch - 1 if d == 0 else 0
            h_t = []
            for h in range(H_M):
                idx = d * H_M + h
                vaug = jnp.concatenate([vt_scr[d, cidx, h * HD:(h + 1) * HD, :], ones_row], axis=0)
                k = (qkv_ref[pl.ds(c0, ch), D_M + h * HD:D_M + (h + 1) * HD] * (HD ** -0.5)).astype(BF16)
                b_row = b_t[idx:idx + 1, :]
                i_row = i_t[idx:idx + 1, :]
                m_prev = m_scr[idx][0:1, 0:1]
                caug = c_scr[idx]
                logd = jnp.where(mask, r_cols[:, idx:idx + 1] + b_row, -jnp.inf)
                m_inter = b_row + m_prev
                m_t = jnp.maximum(jnp.max(logd, axis=0, keepdims=True), m_inter)
                dmat = jnp.exp(logd - m_t)
                w_inter = jnp.exp(m_inter - m_t)
                s = (kq_scr[idx] * dmat).astype(BF16)
                tot = _dot(vaug.astype(BF16), s) + w_inter * in_scr[idx]
                den = jnp.maximum(jnp.abs(tot[HD:HD + 1, :]), jnp.exp(-m_t))
                h_t.append(tot[0:HD, :] / den)
                if h % 2 == 1:
                    pair = jnp.concatenate(h_t[-2:], axis=0).T
                    out_ref[pl.ds(c0, ch), (h - 1) * HD:(h + 1) * HD] = pair
                b_l = b_row[:, edge:edge + 1]
                lw = b_l - b_row + i_row
                m_new = jnp.maximum(b_l + m_prev, jnp.max(lw, axis=1, keepdims=True))
                wk = jnp.exp(lw - m_new)
                decay = jnp.exp(b_l + m_prev - m_new)
                c_scr[idx] = decay * caug + _dot((vaug * wk).astype(BF16), k)
                m_scr[idx] = jnp.broadcast_to(m_new, (8, 128))
        return carry

    lax.fori_loop(0, nch, chunk, 0)

    @pl.when(j == n_tiles - 1)
    def _():
        cfin_ref[...] = c_scr[...]
        mfin_ref[...] = m_scr[...]


def _mlstm_scan(qkv, gate, f_bias, caug0, m0, batch, seq_len, layer=0, depth=1, prev=None):
    n = qkv.shape[0]
    tt = min(seq_len, 512)
    nt = seq_len // tt
    ch = 128
    fb = jnp.zeros((1, GATE_W), F32).at[0, MF0:MF0 + 2 * H_M].set(f_bias.reshape(-1))
    fwd = lambda b, j: (b * nt + j, 0)
    bwd = lambda b, j: (b * nt + nt - 1 - j, 0)
    st4 = lambda b, j: (b, 0, 0, 0)
    st5 = lambda b, j: (b, layer, 0, 0, 0)
    n_in = 7
    prev = tuple(prev or ())
    kern = functools.partial(_mlstm_kernel, n_tiles=nt, ch=ch)
    return pl.pallas_call(
        lambda *refs: kern(*refs[:n_in], *refs[n_in + len(prev):]),
        grid=(batch, nt),
        in_specs=[pl.BlockSpec((tt, 3 * D_M), fwd), pl.BlockSpec((tt, 3 * D_M), bwd),
                  pl.BlockSpec((tt, GATE_W), fwd), pl.BlockSpec((tt, GATE_W), bwd),
                  pl.BlockSpec((1, GATE_W), lambda b, j: (0, 0)),
                  pl.BlockSpec((None, 2 * H_M, 128, HD), st4),
                  pl.BlockSpec((None, 2 * H_M, 8, 128), st4)] + [pl.BlockSpec(memory_space=pl.ANY)] * len(prev),
        out_specs=(pl.BlockSpec((tt, D_M), fwd), pl.BlockSpec((tt, D_M), bwd),
                   pl.BlockSpec((None, None, 2 * H_M, 128, HD), st5),
                   pl.BlockSpec((None, None, 2 * H_M, 8, 128), st5)),
        out_shape=(jax.ShapeDtypeStruct((n, D_M), F32), jax.ShapeDtypeStruct((n, D_M), F32),
                   jax.ShapeDtypeStruct((batch, depth, 2 * H_M, 128, HD), F32),
                   jax.ShapeDtypeStruct((batch, depth, 2 * H_M, 8, 128), F32)),
        input_output_aliases={n_in + i: 2 + i for i in range(len(prev))},
        scratch_shapes=[pltpu.VMEM((2 * H_M, 128, HD), F32), pltpu.VMEM((2 * H_M, 8, 128), F32),
                        pltpu.VMEM((2, tt // ch, 2 * H_M, ch), F32), pltpu.VMEM((2, tt // ch, 2 * H_M, ch), F32),
                        pltpu.VMEM((2, tt // ch, ch, 128), F32), pltpu.VMEM((2, tt // ch, D_M, ch), F32),
                        pltpu.VMEM((2 * H_M, ch, ch), F32), pltpu.VMEM((2 * H_M, 128, ch), F32)],
        compiler_params=_cparams(("arbitrary", "arbitrary")),
    )(qkv, qkv, gate, gate, fb, caug0, m0, *prev)


def _ssd_kernel(xbc_f, xbc_b, gt_f, gt_b, dtb_ref, alog_ref, h0_ref,
                yf_ref, yb_ref, hfin_ref, h_scr, dtt_scr, at_scr, ac_scr, xt_scr, cb_scr, in_scr, *, n_tiles, ch):
    j = pl.program_id(1)

    @pl.when(j == 0)
    def _():
        h_scr[...] = h0_ref[...]

    tt = xbc_f.shape[0]
    nch = tt // ch
    dtb = dtb_ref[...]
    a_neg = -jnp.exp(alog_ref[...])
    low, upp = _tri_masks(ch)
    nrow = 16
    pad_rows = jnp.zeros((ch - nrow, ch), F32)

    dc = [(d, c) for d in range(2) for c in range(nch)]
    da_rows, a_rows = {}, {}
    for d, c in dc:
        dt = _softplus((gt_f, gt_b)[d][c * ch:(c + 1) * ch, :] + dtb)
        dtt_scr[d, c] = dt.T[DT0:DT0 + nrow, :]
        da_rows[d, c] = (dt * a_neg).T[DT0:DT0 + nrow, :]
    for d, c in dc:
        mask = (upp, low)[d]
        a_rows[d, c] = jnp.dot(da_rows[d, c], mask.astype(F32), precision=HI, preferred_element_type=F32)
        at_scr[d, c] = a_rows[d, c]
    for d, c in dc:
        ac_scr[d, c] = jnp.concatenate([a_rows[d, c], pad_rows], axis=0).T
    for d, c in dc:
        for p in range(H_S // 2):
            xt_scr[d, c, p * 2 * HD:(p + 1) * 2 * HD, :] = (
                (xbc_f, xbc_b)[d][c * ch:(c + 1) * ch, p * 2 * HD:(p + 1) * 2 * HD].T)

    def chunk(ci, carry):
        for d in range(2):
            x_ref = (xbc_f, xbc_b)[d]
            c0 = pl.multiple_of((ci if d == 0 else nch - 1 - ci) * ch, ch)
            for grp in range(G_S):
                bm = x_ref[pl.ds(c0, ch), D_S + grp * N_S:D_S + (grp + 1) * N_S].astype(BF16)
                cm = x_ref[pl.ds(c0, ch), D_S + G_S * N_S + grp * N_S:D_S + G_S * N_S + (grp + 1) * N_S].astype(BF16)
                cb_scr[d * G_S + grp] = _dot_nt(bm, cm)
                for r in range(R_S):
                    idx = d * H_S + grp * R_S + r
                    in_scr[idx] = _dot_nt(h_scr[idx].astype(BF16), cm)
        for d in range(2):
            x_ref = (xbc_f, xbc_b)[d]
            out_ref = (yf_ref, yb_ref)[d]
            mask = (upp, low)[d]
            cidx = ci if d == 0 else nch - 1 - ci
            c0 = pl.multiple_of(cidx * ch, ch)
            dt_t = dtt_scr[d, cidx]
            a_t = at_scr[d, cidx]
            a_cols = ac_scr[d, cidx]
            edge = ch - 1 if d == 0 else 0
            y_t = []
            for grp in range(G_S):
                bm = x_ref[pl.ds(c0, ch), D_S + grp * N_S:D_S + (grp + 1) * N_S].astype(BF16)
                for r in range(R_S):
                    hh = grp * R_S + r
                    idx = d * H_S + hh
                    a_row = a_t[idx:idx + 1, :]
                    dt_row = dt_t[idx:idx + 1, :]
                    seg = jnp.exp(jnp.where(mask, a_row - a_cols[:, idx:idx + 1], -jnp.inf))
                    mm = (cb_scr[d * G_S + grp] * seg).astype(BF16)
                    xh = xt_scr[d, cidx, hh * HD:(hh + 1) * HD, :]
                    h_st = h_scr[idx]
                    y_t.append(_dot((xh * dt_row).astype(BF16), mm) + in_scr[idx] * jnp.exp(a_row))
                    if hh % 2 == 1:
                        pair = jnp.concatenate(y_t[-2:], axis=0).T
                        out_ref[pl.ds(c0, ch), (hh - 1) * HD:(hh + 1) * HD] = pair
                    a_l = a_row[:, edge:edge + 1]
                    wk = jnp.exp(a_l - a_row) * dt_row
                    h_scr[idx] = jnp.exp(a_l) * h_st + _dot((xh * wk).astype(BF16), bm)
        return carry

    lax.fori_loop(0, nch, chunk, 0)

    @pl.when(j == n_tiles - 1)
    def _():
        hfin_ref[...] = h_scr[...]


def _ssd_scan(xbc, gate, dt_bias, a_log, h0_t, batch, seq_len, layer=0, depth=1, prev=None):
    n = xbc.shape[0]
    tt = min(seq_len, 512)
    nt = seq_len // tt
    ch = 128
    dtb = jnp.zeros((1, GATE_W), F32).at[0, DT0:DT0 + 2 * H_S].set(dt_bias.reshape(-1))
    alog = jnp.zeros((1, GATE_W), F32).at[0, DT0:DT0 + 2 * H_S].set(a_log.reshape(-1))
    fwd = lambda b, j: (b * nt + j, 0)
    bwd = lambda b, j: (b * nt + nt - 1 - j, 0)
    st4 = lambda b, j: (b, 0, 0, 0)
    n_in = 7
    prev = tuple(prev or ())
    kern = functools.partial(_ssd_kernel, n_tiles=nt, ch=ch)
    return pl.pallas_call(
        lambda *refs: kern(*refs[:n_in], *refs[n_in + len(prev):]),
        grid=(batch, nt),
        in_specs=[pl.BlockSpec((tt, D_XBC), fwd), pl.BlockSpec((tt, D_XBC), bwd),
                  pl.BlockSpec((tt, GATE_W), fwd), pl.BlockSpec((tt, GATE_W), bwd),
                  pl.BlockSpec((1, GATE_W), lambda b, j: (0, 0)),
                  pl.BlockSpec((1, GATE_W), lambda b, j: (0, 0)),
                  pl.BlockSpec((None, 2 * H_S, N_S, HD), st4)] + [pl.BlockSpec(memory_space=pl.ANY)] * len(prev),
        out_specs=(pl.BlockSpec((tt, D_S), fwd), pl.BlockSpec((tt, D_S), bwd),
                   pl.BlockSpec((None, None, 2 * H_S, N_S, HD), lambda b, j: (b, layer, 0, 0, 0))),
        out_shape=(jax.ShapeDtypeStruct((n, D_S), F32), jax.ShapeDtypeStruct((n, D_S), F32),
                   jax.ShapeDtypeStruct((batch, depth, 2 * H_S, N_S, HD), F32)),
        input_output_aliases={n_in + i: 2 + i for i in range(len(prev))},
        scratch_shapes=[pltpu.VMEM((2 * H_S, N_S, HD), F32),
                        pltpu.VMEM((2, tt // ch, 16, ch), F32), pltpu.VMEM((2, tt // ch, 16, ch), F32),
                        pltpu.VMEM((2, tt // ch, ch, 128), F32), pltpu.VMEM((2, tt // ch, D_S, ch), F32),
                        pltpu.VMEM((2 * G_S, ch, ch), F32), pltpu.VMEM((2 * H_S, HD, ch), F32)],
        compiler_params=_cparams(("arbitrary", "arbitrary")),
    )(xbc, xbc, gate, gate, dtb, alog, h0_t, *prev)


def _ctx_attn_kernel(q_ref, k_ref, v_ref, o_ref, s_scr):
    n = q_ref.shape[0]
    first = lax.broadcasted_iota(jnp.int32, (n, 2 * HD), 1) < HD
    ones = jnp.ones((n, 2 * HD), BF16)
    for p in range(H_NA // 2):
        sl = slice(p * 2 * HD, (p + 1) * 2 * HD)
        qp = q_ref[:, sl]
        kp = k_ref[:, sl].astype(BF16)
        for half in range(2):
            mine = first if half == 0 else jnp.logical_not(first)
            q = (jnp.where(mine, qp, jnp.zeros_like(qp)) * (HD ** -0.5)).astype(BF16)
            s_scr[2 * p + half] = _dot_nt(q, kp)
    for p in range(H_NA // 2):
        sl = slice(p * 2 * HD, (p + 1) * 2 * HD)
        vp = v_ref[:, sl].astype(BF16)
        outs = []
        for half in range(2):
            s = s_scr[2 * p + half]
            pr = jnp.exp(s - jnp.max(s, axis=-1, keepdims=True)).astype(BF16)
            outs.append(_dot(pr, vp) / _dot(pr, ones))
        o_ref[:, sl] = jnp.where(first, outs[0], outs[1])


def _ctx_attention(q, k, v, batch, seq_len):
    spec = pl.BlockSpec((seq_len, D_NA), lambda b: (b, 0))
    return pl.pallas_call(
        _ctx_attn_kernel,
        grid=(batch,),
        in_specs=[spec, spec, spec],
        out_specs=spec,
        out_shape=jax.ShapeDtypeStruct(q.shape, F32),
        scratch_shapes=[pltpu.VMEM((H_NA, seq_len, seq_len), F32)],
        compiler_params=_cparams(("arbitrary",)),
    )(q, k, v)


def _row_start(r, rows):
    return jnp.clip(r - WIN_H // 2, 0, rows - WIN_H)


def _natten_kernel(q_ref, k_ref, v_ref, ck_ref, cv_ref, bias_ref, o_ref, sl_scr, sc_scr, *, rows):
    r = pl.program_id(1)
    k0 = pl.multiple_of(_row_start(r, rows) * GRID_W, GRID_W)
    off0 = _row_start(r, rows) - r + WIN_H - 1
    nloc = WIN_H * GRID_W
    npast = ck_ref.shape[0]
    first = lax.broadcasted_iota(jnp.int32, (GRID_W, 2 * HD), 1) < HD
    ones_loc = jnp.ones((nloc, 2 * HD), BF16)
    ones_ctx = jnp.ones((npast, 2 * HD), BF16)
    for p in range(H_NA // 2):
        sl = slice(p * 2 * HD, (p + 1) * 2 * HD)
        qp = q_ref[:, sl]
        kw = k_ref[pl.ds(k0, nloc), sl].astype(BF16)
        ckp = ck_ref[:, sl].astype(BF16)
        for half in range(2):
            mine = first if half == 0 else jnp.logical_not(first)
            q = (jnp.where(mine, qp, jnp.zeros_like(qp)) * (HD ** -0.5)).astype(BF16)
            bias = jnp.concatenate([bias_ref[2 * p + half, off0 + 2 * i] for i in range(WIN_H // 2)], axis=1)
            sl_scr[2 * p + half] = _dot_nt(q, kw) + bias
            sc_scr[2 * p + half] = _dot_nt(q, ckp)
    for p in range(H_NA // 2):
        sl = slice(p * 2 * HD, (p + 1) * 2 * HD)
        vw = v_ref[pl.ds(k0, nloc), sl].astype(BF16)
        cvp = cv_ref[:, sl].astype(BF16)
        outs = []
        for half in range(2):
            s_loc = sl_scr[2 * p + half]
            s_ctx = sc_scr[2 * p + half]
            m = jnp.maximum(jnp.max(s_loc, axis=-1, keepdims=True), jnp.max(s_ctx, axis=-1, keepdims=True))
            p_loc = jnp.exp(s_loc - m).astype(BF16)
            p_ctx = jnp.exp(s_ctx - m).astype(BF16)
            l = _dot(p_loc, ones_loc) + _dot(p_ctx, ones_ctx)
            outs.append((_dot(p_loc, vw) + _dot(p_ctx, cvp)) / l)
        o_ref[:, sl] = jnp.where(first, outs[0], outs[1])


def _natten_bias(rpb, rows):
    qc = np.arange(GRID_W)[:, None]
    kc = np.arange(GRID_W)[None, :]
    cstart = np.clip(qc - WIN_W // 2, 0, GRID_W - WIN_W)
    ok = (kc >= cstart) & (kc < cstart + WIN_W)
    col_off = np.clip(kc - qc + WIN_W - 1, 0, 2 * WIN_W - 2)
    pick = jnp.asarray((col_off[None] == np.arange(2 * WIN_W - 1)[:, None, None]).astype(np.float32))
    toep = jnp.einsum("hoj,jqk->hoqk", rpb.astype(F32), pick, precision=HI)
    t = jnp.where(jnp.asarray(ok)[None, None], toep, -jnp.inf)
    return jnp.concatenate([t[:, :-1], t[:, 1:]], axis=-1)


def _neighbourhood_attention(q, k, v, ck, cv, bias, batch, seq_len):
    rows = seq_len // GRID_W
    past = ck.shape[0] // batch
    return pl.pallas_call(
        functools.partial(_natten_kernel, rows=rows),
        grid=(batch, rows),
        in_specs=[pl.BlockSpec((GRID_W, D_NA), lambda b, r: (b * rows + r, 0)),
                  pl.BlockSpec((seq_len, D_NA), lambda b, r: (b, 0)),
                  pl.BlockSpec((seq_len, D_NA), lambda b, r: (b, 0)),
                  pl.BlockSpec((past, D_NA), lambda b, r: (b, 0)),
                  pl.BlockSpec((past, D_NA), lambda b, r: (b, 0)),
                  pl.BlockSpec((H_NA, 2 * WIN_H - 2, GRID_W, 2 * GRID_W), lambda b, r: (0, 0, 0, 0))],
        out_specs=pl.BlockSpec((GRID_W, D_NA), lambda b, r: (b * rows + r, 0)),
        out_shape=jax.ShapeDtypeStruct(q.shape, F32),
        scratch_shapes=[pltpu.VMEM((H_NA, GRID_W, WIN_H * GRID_W), F32), pltpu.VMEM((H_NA, GRID_W, past), F32)],
        compiler_params=_cparams(("arbitrary", "arbitrary")),
    )(q, k, v, ck, cv, bias)


def _outproj_kernel(x_ref, hf_ref, hb_ref, mo_ref, yf_ref, yb_ref, xbc_ref, sz_ref, a_ref,
                    mg_ref, dsk_ref, sg_ref, bd_ref, w_ref, g1_ref, n2_ref, sh2_ref, sc2_ref, rw_ref,
                    xo_ref, h2_ref, aff_ref):
    hm = hf_ref[...] + hb_ref[...]
    ssq = jnp.dot(hm * hm, bd_ref[...], precision=HI, preferred_element_type=F32)
    m_out = hm * lax.rsqrt(ssq * (1.0 / HD) + EPS) * mg_ref[...] * _sigmoid(mo_ref[...])
    sz = sz_ref[...]
    ys = (yf_ref[...] + yb_ref[...] + dsk_ref[...] * xbc_ref[:, 0:D_S]) * (sz * _sigmoid(sz))
    s_out = ys * lax.rsqrt(jnp.mean(ys * ys, axis=-1, keepdims=True) + EPS) * sg_ref[...]
    y = (_dot(m_out.astype(BF16), w_ref[0:D_M, :])
         + _dot(s_out.astype(BF16), w_ref[D_M:D_M + D_S, :])
         + _dot(a_ref[...].astype(BF16), w_ref[D_M + D_S:D, :]))
    xn = x_ref[...] + g1_ref[...] * y
    xo_ref[...] = xn
    h2 = xn * lax.rsqrt(jnp.mean(xn * xn, axis=-1, keepdims=True) + EPS) * n2_ref[...]
    h2 = h2 * (1.0 + sc2_ref[...]) + sh2_ref[...]
    for c in range(h2_ref.shape[0]):
        h2_ref[c] = h2[c * GATHER_TOK:(c + 1) * GATHER_TOK, :].T.astype(BF16)
    logits = lax.dot_general(rw_ref[...], h2, (((1,), (1,)), ((), ())), precision=HI,
                             preferred_element_type=F32)
    mx = jnp.max(logits, axis=0, keepdims=True)
    ex = jnp.exp(logits - mx)
    aff_ref[...] = ex / jnp.sum(ex, axis=0, keepdims=True)


def _out_projection(x, hf, hb, mo, yf, yb, xbc, sz, a_out, mnorm_g, d_skip, snorm_g, w_out_bf,
                    g1, norm2_g, sh2, sc2, router_wt, seq_len):
    n = x.shape[0]
    tm = 512
    cpt = tm // GATHER_TOK
    groups = g1.shape[0]
    if groups == 1:
        gidx = lambda i: (0, 0, 0)
    else:
        gidx = lambda i: ((i * tm) // seq_len, 0, 0)
    row = lambda w: pl.BlockSpec((tm, w), lambda i: (i, 0))
    const = lambda s: pl.BlockSpec(s, lambda i: (0,) * len(s))
    hid = np.arange(D_M) // HD
    blockdiag = jnp.asarray((hid[:, None] == hid[None, :]).astype(np.float32))
    return pl.pallas_call(
        _outproj_kernel,
        grid=(n // tm,),
        in_specs=[row(D), row(D_M), row(D_M), row(D_M), row(D_S), row(D_S), row(D_XBC), row(D_S), row(D_NA),
                  const((1, D_M)), const((1, D_S)), const((1, D_S)), const((D_M, D_M)), const((D, D)),
                  pl.BlockSpec((None, 1, D), gidx), const((1, D)),
                  pl.BlockSpec((None, 1, D), gidx), pl.BlockSpec((None, 1, D), gidx),
                  const((N_EXPERTS, D))],
        out_specs=(row(D), pl.BlockSpec((cpt, D, GATHER_TOK), lambda i: (i, 0, 0)),
                   pl.BlockSpec((N_EXPERTS, tm), lambda i: (0, i))),
        out_shape=(jax.ShapeDtypeStruct((n, D), F32), jax.ShapeDtypeStruct((n // GATHER_TOK, D, GATHER_TOK), BF16),
                   jax.ShapeDtypeStruct((N_EXPERTS, n), F32)),
        compiler_params=_cparams(("arbitrary",)),
    )(x, hf, hb, mo, yf, yb, xbc, sz, a_out,
      mnorm_g.reshape(1, D_M), jnp.repeat(d_skip, HD).reshape(1, D_S), snorm_g.reshape(1, D_S), blockdiag,
      w_out_bf, g1, norm2_g.reshape(1, D), sh2, sc2, router_wt)


def _select_kernel(aff_ref, gate_ref, slot_ref, start_ref, *, cap):
    aff = aff_ref[...]
    bits = pltpu.bitcast(aff, jnp.int32)
    n_tok = aff.shape[1]

    def step(i, prefix):
        cand = prefix | (jnp.int32(1) << (30 - i))
        cnt = jnp.sum((bits >= cand).astype(F32), axis=1, keepdims=True)
        return jnp.where(cnt >= cap, cand, prefix)

    thr = lax.fori_loop(0, 31, step, jnp.zeros((aff.shape[0], 1), jnp.int32))
    gt = bits > thr
    eq = bits == thr
    need = cap - jnp.sum(gt.astype(F32), axis=1, keepdims=True)
    r_i = lax.broadcasted_iota(jnp.int32, (128, 128), 0)
    c_i = lax.broadcasted_iota(jnp.int32, (128, 128), 1)
    strict = (r_i < c_i).astype(BF16)
    run_eq = jnp.zeros((aff.shape[0], 1), F32)
    run_sel = jnp.zeros((aff.shape[0], 1), F32)
    lane = lax.broadcasted_iota(jnp.int32, (aff.shape[0], 128), 1)
    starts = jnp.zeros((aff.shape[0], 128), F32)
    for blk in range(n_tok // TOK_BLK):
        sl = slice(blk * TOK_BLK, (blk + 1) * TOK_BLK)
        e = eq[:, sl]
        rank = _dot(e.astype(BF16), strict) + run_eq
        keep = gt[:, sl] | (e & (rank < need))
        kf = jnp.where(keep, 1.0, 0.0)
        slot = _dot(kf.astype(BF16), strict) + run_sel
        gate_ref[:, sl] = jnp.where(keep, aff[:, sl], 0.0)
        slot_ref[:, sl] = jnp.where(keep, slot, -1.0).astype(jnp.int32)
        starts = jnp.where(lane == blk, run_sel, starts)
        run_eq = run_eq + jnp.sum(e.astype(F32), axis=1, keepdims=True)
        run_sel = run_sel + jnp.sum(kf, axis=1, keepdims=True)
    start_ref[...] = starts.astype(jnp.int32)


def _select(aff_t):
    n = aff_t.shape[1]
    cap = EC_FACTOR * n // N_EXPERTS
    assert n // TOK_BLK <= 128
    return pl.pallas_call(
        functools.partial(_select_kernel, cap=float(cap)),
        out_shape=(jax.ShapeDtypeStruct(aff_t.shape, F32), jax.ShapeDtypeStruct(aff_t.shape, jnp.int32),
                   jax.ShapeDtypeStruct((N_EXPERTS, 128), jnp.int32)),
        compiler_params=pltpu.CompilerParams(vmem_limit_bytes=VMEM_LIMIT),
    )(aff_t)


def _gather_rows(e, lo_ref, hi_ref, slot_ref, h2t_ref, xe_scr, acc_scr, col_scr, cap):
    nsb = cap // GATHER_SLOT
    nchunk = h2t_ref.shape[0]
    fill = jnp.zeros((TOK_BLK - 8, GATHER_TOK), F32)
    for c in range(nchunk):
        row = jnp.broadcast_to(slot_ref[:, c * GATHER_TOK:(c + 1) * GATHER_TOK].astype(F32), (8, GATHER_TOK))
        col_scr[c] = jnp.concatenate([row, fill], axis=0).T
    lane = lax.broadcasted_iota(jnp.int32, (GATHER_TOK, GATHER_SLOT), 1).astype(F32)
    win = min(GATHER_WIN, nchunk)

    def onehot_t(c, want):
        return jnp.where(col_scr[c][:, 0:1] == want, 1.0, 0.0).astype(BF16)

    for sb in range(nsb):
        want = lane + float(sb * GATHER_SLOT)
        first = jnp.minimum(lo_ref[e * nsb + sb], nchunk - win)
        acc = _dot(h2t_ref[first], onehot_t(first, want))
        for i in range(1, win):
            acc = acc + _dot(h2t_ref[first + i], onehot_t(first + i, want))
        acc_scr[...] = acc

        def body(c, carry):
            acc_scr[...] += _dot(h2t_ref[c], onehot_t(c, want))
            return carry

        lax.fori_loop(first + win, hi_ref[e * nsb + sb] + 1, body, 0)
        xe_scr[sb * GATHER_SLOT:(sb + 1) * GATHER_SLOT, :] = acc_scr[...].T.astype(BF16)


def _expert_kernel(lo_ref, hi_ref, slot_ref, h2t_ref, w1_ref, w3_ref, w2_ref, ye_ref,
                   xe_scr, acc_scr, col_scr, y_scr, *, cap):
    e = pl.program_id(0)
    f = pl.program_id(1)

    @pl.when(f == 0)
    def _():
        _gather_rows(e, lo_ref, hi_ref, slot_ref, h2t_ref, xe_scr, acc_scr, col_scr, cap)

    x = xe_scr[...]
    a = _dot(x, w1_ref[...].astype(BF16))
    b = _dot(x, w3_ref[...].astype(BF16))
    hid = (a * _sigmoid(a) * b).astype(BF16)
    part = _dot(hid, w2_ref[...].astype(BF16))

    @pl.when(f == 0)
    def _():
        y_scr[...] = part

    @pl.when(f == 1)
    def _():
        ye_ref[...] = (y_scr[...] + part).astype(BF16)


def _combine_kernel(win_ref, slot_ref, gate_ref, x_ref, g2_ref, fg_ref, ye_ref, o_ref, acc_scr, *, final):
    lane = lax.broadcasted_iota(jnp.int32, (TOK_BLK, 2 * SLOT_BLK), 1)
    fill = jnp.zeros((TOK_BLK - N_EXPERTS, TOK_BLK), F32)
    for s in range(COMBINE_BLKS):
        tb = pl.program_id(0) * COMBINE_BLKS + s
        rows = slice(s * TOK_BLK, (s + 1) * TOK_BLK)
        slot_t = jnp.concatenate([slot_ref[:, rows].astype(F32), fill], axis=0).T
        gate_t = jnp.concatenate([gate_ref[:, rows], fill], axis=0).T
        acc_scr[...] = jnp.zeros_like(acc_scr)
        for e in range(N_EXPERTS):
            s0 = pl.multiple_of(win_ref[tb * N_EXPERTS + e] * SLOT_BLK, SLOT_BLK)
            want = (lane + s0).astype(F32)
            onehot = jnp.where(slot_t[:, e:e + 1] == want, 1.0, 0.0).astype(BF16)
            acc_scr[...] += _dot(onehot, ye_ref[e, pl.ds(s0, 2 * SLOT_BLK), :]) * gate_t[:, e:e + 1]
        y = x_ref[rows, :] + g2_ref[...] * acc_scr[...]
        if final:
            y = y * lax.rsqrt(jnp.mean(y * y, axis=-1, keepdims=True) + EPS) * fg_ref[...]
        o_ref[rows, :] = y


def _moe(h2, aff_t, x, g2, w1, w3, w2, layer, seq_len, final_g=None):
    n = x.shape[0]
    cap = EC_FACTOR * n // N_EXPERTS
    nsb = cap // SLOT_BLK
    ntb = n // TOK_BLK
    gate, slot, starts = _select(aff_t)

    cs = starts[:, :ntb]
    win = jnp.minimum(cs // SLOT_BLK, nsb - 2).T.astype(jnp.int32)
    gs = cs[:, ::GATHER_TOK // TOK_BLK]
    gend = jnp.concatenate([gs[:, 1:], jnp.full((N_EXPERTS, 1), cap, jnp.int32)], axis=1)
    edges = jnp.arange(cap // GATHER_SLOT, dtype=jnp.int32) * GATHER_SLOT
    lo = jnp.sum(gend[:, None, :] <= edges[None, :, None], axis=-1).astype(jnp.int32)
    hi = jnp.sum(gs[:, None, :] < (edges + GATHER_SLOT)[None, :, None], axis=-1).astype(jnp.int32) - 1

    d_ff = w1.shape[-1]
    up_spec = pl.BlockSpec((None, None, D, d_ff // 2), lambda e, f, *_: (layer, e, 0, f))
    down_spec = pl.BlockSpec((None, None, d_ff // 2, D), lambda e, f, *_: (layer, e, f, 0))
    ye = pl.pallas_call(
        functools.partial(_expert_kernel, cap=cap),
        grid_spec=pltpu.PrefetchScalarGridSpec(
            num_scalar_prefetch=2,
            grid=(N_EXPERTS, 2),
            in_specs=[pl.BlockSpec((None, 1, n), lambda e, f, *_: (e, 0, 0)),
                      pl.BlockSpec((n // GATHER_TOK, D, GATHER_TOK), lambda e, f, *_: (0, 0, 0),
                                   pipeline_mode=pl.Buffered(1)),
                      up_spec, up_spec, down_spec],
            out_specs=pl.BlockSpec((None, cap, D), lambda e, f, *_: (e, 0, 0)),
            scratch_shapes=[pltpu.VMEM((cap, D), BF16), pltpu.VMEM((D, GATHER_SLOT), F32),
                            pltpu.VMEM((n // GATHER_TOK, GATHER_TOK, TOK_BLK), F32),
                            pltpu.VMEM((cap, D), F32)]),
        out_shape=jax.ShapeDtypeStruct((N_EXPERTS, cap, D), BF16),
        compiler_params=_cparams(("arbitrary", "arbitrary")),
    )(lo.reshape(-1), hi.reshape(-1), slot.reshape(N_EXPERTS, 1, n), h2, w1, w3, w2)

    groups = g2.shape[0]
    if groups == 1:
        gidx = lambda i, *_: (0, 0, 0)
    else:
        gidx = lambda i, *_: ((i * tc) // seq_len, 0, 0)
    tc = COMBINE_BLKS * TOK_BLK
    fg = jnp.ones((1, D), F32) if final_g is None else final_g.reshape(1, D)
    return pl.pallas_call(
        functools.partial(_combine_kernel, final=final_g is not None),
        grid_spec=pltpu.PrefetchScalarGridSpec(
            num_scalar_prefetch=1,
            grid=(n // tc,),
            in_specs=[pl.BlockSpec((N_EXPERTS, tc), lambda i, *_: (0, i)),
                      pl.BlockSpec((N_EXPERTS, tc), lambda i, *_: (0, i)),
                      pl.BlockSpec((tc, D), lambda i, *_: (i, 0)),
                      pl.BlockSpec((None, 1, D), gidx),
                      pl.BlockSpec((1, D), lambda i, *_: (0, 0)),
                      pl.BlockSpec((N_EXPERTS, cap, D), lambda i, *_: (0, 0, 0), pipeline_mode=pl.Buffered(1))],
            out_specs=pl.BlockSpec((tc, D), lambda i, *_: (i, 0)),
            scratch_shapes=[pltpu.VMEM((TOK_BLK, D), F32)]),
        out_shape=jax.ShapeDtypeStruct((n, D), F32),
        compiler_params=_cparams(("arbitrary",)),
    )(win.reshape(-1), slot, gate, x, g2, fg, ye)


def _layer(x, mod, prm, batch, seq_len, caug0, m0, h0_t, ck=None, cv=None, bias=None, kv_stack=None,
           state_stack=None):
    st_layer, st_depth, st_cm, st_h = state_stack or (0, 1, None, None)
    sh1, sc1, g1, sh2, sc2, g2 = [m[:, None, :] for m in jnp.split(mod, 6, axis=-1)]
    attn_dtype = F32 if ck is None else BF16
    qkv, mo, sz, xbc, nq, nk, nv, gate, *kv_out = _in_projection(
        x, prm["norm1_g"], sh1, sc1, prm["w_in"], prm["conv_w"], prm["conv_b"], seq_len, attn_dtype, kv_stack)
    hf, hb, cfin, mfin = _mlstm_scan(qkv, gate, prm["f_bias"], caug0, m0, batch, seq_len, st_layer, st_depth, st_cm)
    yf, yb, hfin = _ssd_scan(xbc, gate, prm["dt_bias"], prm["a_log"], h0_t, batch, seq_len, st_layer, st_depth, st_h)
    if ck is None:
        a_out = _ctx_attention(nq, nk, nv, batch, seq_len)
    else:
        a_out = _neighbourhood_attention(nq, nk, nv, ck, cv, bias, batch, seq_len)
    xn, h2, aff_t = _out_projection(x, hf, hb, mo, yf, yb, xbc, sz, a_out, prm["mnorm_g"], prm["d_skip"],
                                    prm["snorm_g"], prm["w_out"], g1, prm["norm2_g"], sh2, sc2,
                                    prm["router_wt"], seq_len)
    xo = _moe(h2, aff_t, xn, g2, prm["w1"], prm["w3"], prm["w2"], prm["layer"], seq_len, prm["final_g"])
    return xo, kv_out, cfin, mfin, hfin


def _pack_w_in(w_in):
    o = np.cumsum((0, D_M, D_M, D_M, D_M, 2 * H_M, 2 * H_M, D_S, D_XBC, 2 * H_S, D_NA, D_NA, D_NA))
    pad = jnp.zeros(w_in.shape[:-1] + (GATE_W - 4 * H_M - 2 * H_S,), w_in.dtype)
    parts = [w_in[..., o[0]:o[4]], w_in[..., o[6]:o[8]], w_in[..., o[9]:o[12]],
             w_in[..., o[4]:o[6]], w_in[..., o[8]:o[9]], pad]
    return jnp.concatenate(parts, axis=-1).astype(BF16)


def kernel(x_prompt, x_sample, cache_na_k, cache_na_v, state_mlstm_c, state_mlstm_n, state_mlstm_m, state_ssm, c, c_ctx, ada_w, ada_b, norm1_g, norm2_g, w_in, mlstm_f_bias, mlstm_norm_g, conv_w, conv_b, ssm_dt_bias, ssm_a_log, ssm_d, ssm_norm_g, na_rpb, w_out, router_w, exp_w1, exp_w3, exp_w2, final_g):
    bp, sp, _ = x_prompt.shape
    bs, ss, _ = x_sample.shape
    past = cache_na_k.shape[2]

    cvec = jnp.zeros((8, D), F32).at[0].set(c_ctx).at[1:1 + bs].set(c)
    mod = _modulation(cvec, ada_w, ada_b)

    w_in_p = _pack_w_in(w_in)
    w_out_bf = w_out.astype(BF16)
    router_wt = jnp.swapaxes(router_w, 1, 2)

    def aug(cs, ns):
        pad = jnp.zeros(cs.shape[:-2] + (128 - HD - 1, HD), F32)
        out = jnp.concatenate([jnp.swapaxes(cs, -1, -2), ns[..., None, :], pad], axis=-2)
        return out.reshape(cs.shape[:2] + (2 * H_M, 128, HD))

    lat_caug_all = aug(state_mlstm_c, state_mlstm_n)

    def rep_m(ms):
        b = ms.shape[0]
        return jnp.broadcast_to(ms.reshape(b, 2 * H_M, 1, 1), (b, 2 * H_M, 8, 128)).astype(F32)

    ctx_caug0 = jnp.zeros((bp, 2 * H_M, 128, HD), F32)
    ctx_m0 = jnp.full((bp, 2 * H_M, 8, 128), NEG_INIT, F32)
    ctx_h0 = jnp.zeros((bp, 2 * H_S, N_S, HD), F32)

    xp = x_prompt.reshape(bp * sp, D)
    xs = x_sample.reshape(bs * ss, D)
    k_stack = jnp.zeros((bp, DEPTH, sp, D_NA), F32)
    v_stack = jnp.zeros((bp, DEPTH, sp, D_NA), F32)
    cm_stack = (jnp.zeros((bp, DEPTH, 2 * H_M, 128, HD), F32), jnp.zeros((bp, DEPTH, 2 * H_M, 8, 128), F32))
    h_stack = (jnp.zeros((bp, DEPTH, 2 * H_S, N_S, HD), F32),)
    for l in range(DEPTH):
        prm = dict(norm1_g=norm1_g[l], norm2_g=norm2_g[l], w_in=w_in_p[l], f_bias=mlstm_f_bias[l],
                   mnorm_g=mlstm_norm_g[l], conv_w=conv_w[l], conv_b=conv_b[l], dt_bias=ssm_dt_bias[l],
                   a_log=ssm_a_log[l], d_skip=ssm_d[l], snorm_g=ssm_norm_g[l], w_out=w_out_bf[l],
                   router_wt=router_wt[l], w1=exp_w1, w3=exp_w3, w2=exp_w2, layer=l,
                   final_g=final_g if l == DEPTH - 1 else None)
        xp, (k_stack, v_stack), c_all, m_all, h_all = _layer(
            xp, mod[l, 0:1], prm, bp, sp, ctx_caug0, ctx_m0, ctx_h0, kv_stack=(l, DEPTH, k_stack, v_stack),
            state_stack=(l, DEPTH, cm_stack, h_stack))
        cm_stack, h_stack = (c_all, m_all), (h_all,)

        lat_caug0 = lat_caug_all[:, l]
        lat_m0 = rep_m(state_mlstm_m[:, l])
        lat_h0 = state_ssm[:, l].reshape(bs, 2 * H_S, HD, N_S)
        ck = cache_na_k[:, l].reshape(bs * past, D_NA).astype(BF16)
        cv = cache_na_v[:, l].reshape(bs * past, D_NA).astype(BF16)
        bias = _natten_bias(na_rpb[l], ss // GRID_W)
        xs = _layer(xs, mod[l, 1:1 + bs], prm, bs, ss, lat_caug0, lat_m0, lat_h0, ck, cv, bias)[0]

    y_prompt = xp.reshape(bp, sp, D)
    y_sample = xs.reshape(bs, ss, D)
    new_c = jnp.swapaxes(c_all[..., 0:HD, :], -1, -2).reshape(bp, DEPTH, 2, H_M, HD, HD)
    new_n = c_all[..., HD, :].reshape(bp, DEPTH, 2, H_M, HD)
    new_m = m_all[..., 0, 0].reshape(bp, DEPTH, 2, H_M)
    return (y_prompt, y_sample, k_stack.reshape(bp, DEPTH, sp, H_NA, HD), v_stack.reshape(bp, DEPTH, sp, H_NA, HD),
            new_c, new_n, new_m, h_all.reshape(bp, DEPTH, 2, H_S, HD, N_S))
```

```python
import functools

import numpy as np
import jax
import jax.numpy as jnp
from jax import lax
from jax.experimental import pallas as pl
from jax.experimental.pallas import tpu as pltpu

F32 = jnp.float32
BF16 = jnp.bfloat16
HI = lax.Precision.HIGHEST

D = 1024
DEPTH = 4
HD = 64
H_M = 4
D_M = H_M * HD
H_S = 6
D_S = H_S * HD
G_S = 2
R_S = H_S // G_S
N_S = 64
D_XBC = D_S + 2 * G_S * N_S
H_NA = 6
D_NA = H_NA * HD
GRID_W = 64
WIN_H = 8
WIN_W = 16
N_EXPERTS = 16
EC_FACTOR = 2
EPS = 1e-6
NEG_INIT = -1e30
TOK_BLK = 128
SLOT_BLK = 128
COMBINE_BLKS = 2
GATHER_TOK = 256
GATHER_SLOT = 256
GATHER_WIN = 10
GATE_W = 128
MI0, MF0, DT0 = 0, 2 * H_M, 4 * H_M
W_PACKED = 3 * D_M + D_M + D_S + D_XBC + 3 * D_NA + GATE_W

VMEM_LIMIT = 56 * 1024 * 1024


def _cparams(sem):
    return pltpu.CompilerParams(dimension_semantics=sem, vmem_limit_bytes=VMEM_LIMIT)


def _sigmoid(x):
    return 1.0 / (1.0 + jnp.exp(-x))


def _softplus(x):
    return jnp.maximum(x, 0.0) + jnp.log1p(jnp.exp(-jnp.abs(x)))


def _dot(a, b):
    return jnp.dot(a, b, preferred_element_type=F32)


def _dot_nt(a, b):
    return lax.dot_general(a, b, (((1,), (1,)), ((), ())), preferred_element_type=F32)


def _dot_tn(a, b):
    return lax.dot_general(a, b, (((0,), (0,)), ((), ())), preferred_element_type=F32)


def _mod_kernel(c_ref, w_ref, b_ref, o_ref):
    cv = c_ref[...]
    s = cv * _sigmoid(cv)
    o_ref[...] = jnp.dot(s, w_ref[...], precision=HI, preferred_element_type=F32) + b_ref[...]


def _modulation(cvec, ada_w, ada_b):
    tn = 1536
    return pl.pallas_call(
        _mod_kernel,
        grid=(DEPTH, 6 * D // tn),
        in_specs=[pl.BlockSpec((8, D), lambda l, j: (0, 0)),
                  pl.BlockSpec((None, D, tn), lambda l, j: (l, 0, j)),
                  pl.BlockSpec((None, 1, tn), lambda l, j: (l, 0, j))],
        out_specs=pl.BlockSpec((None, 8, tn), lambda l, j: (l, 0, j)),
        out_shape=jax.ShapeDtypeStruct((DEPTH, 8, 6 * D), F32),
        compiler_params=_cparams(("arbitrary", "arbitrary")),
    )(cvec, ada_w, ada_b.reshape(DEPTH, 1, 6 * D))


def _mod_operand(m, tm, seq_len):
    if not isinstance(m, tuple):
        per_seq = m.shape[0] > 1
        return m, pl.BlockSpec((None, 1, D), lambda i, *_: ((i * tm) // seq_len if per_seq else 0, 0, 0))
    table, layer, row0, col, per_seq = m
    if per_seq:
        idx = lambda i, *_: (layer, row0 + (i * tm) // seq_len, 0, col)
    else:
        idx = lambda i, *_: (layer, row0, 0, col)
    return table, pl.BlockSpec((None, None, 1, D), idx)


_IN_SEGS = (("qkv", 0, 768), ("mo", 768, 1024), ("sz", 1024, 1408), ("xbc", 1408, 2048),
            ("nq", 2048, 2432), ("nk", 2432, 2816), ("nv", 2816, 3200), ("gate", 3200, 3328))


def _inproj_kernel(x_ref, xp_ref, xn_ref, g_ref, sh_ref, sc_ref, w_ref, cw_ref, cb_ref, *rest,
                   n_alias, n_seg, seq_len):
    outs = rest[n_alias:]
    tm = x_ref.shape[0]

    def norm_mod(x):
        ms = jnp.mean(x * x, axis=-1, keepdims=True)
        h = x * lax.rsqrt(ms + EPS) * g_ref[...]
        return (h * (1.0 + sc_ref[...]) + sh_ref[...]).astype(BF16)

    hb = norm_mod(x_ref[...])
    vals = {}
    for o_ref, (name, lo, hi) in zip(outs[:n_seg], _IN_SEGS):
        vals[name] = _dot(hb, w_ref[:, lo:hi])
        if name == "xbc":
            cur = vals[name]
            edge = _dot(norm_mod(jnp.concatenate([xp_ref[...], xn_ref[...]], axis=0)), w_ref[:, lo:hi])
            rows = lax.broadcasted_iota(jnp.int32, cur.shape, 0)
            if seq_len >= tm:
                i = pl.program_id(0)
                tps = seq_len // tm
                starts = (rows == 0) & ((i % tps) == 0)
                ends = (rows == tm - 1) & ((i % tps) == tps - 1)
            else:
                starts = functools.reduce(jnp.logical_or, [rows == k * seq_len for k in range(tm // seq_len)])
                ends = functools.reduce(jnp.logical_or, [rows == (k + 1) * seq_len - 1 for k in range(tm // seq_len)])
            prev = jnp.where(rows == 0, edge[7:8, :], pltpu.roll(cur, 1, 0))
            nxt = jnp.where(rows == tm - 1, edge[8:9, :], pltpu.roll(cur, tm - 1, 0))
            prev = jnp.where(starts, 0.0, prev)
            nxt = jnp.where(ends, 0.0, nxt)
            y = prev * cw_ref[0:1, :] + cur * cw_ref[1:2, :] + nxt * cw_ref[2:3, :] + cb_ref[...]
            vals[name] = y * _sigmoid(y)
        o_ref[...] = vals[name].astype(o_ref.dtype)
    for o_ref, name in zip(outs[n_seg:], ("nk", "nv")):
        o_ref[...] = vals[name].reshape(o_ref.shape)


def _in_projection(x, norm_g, shift, scale, w_packed, conv_w, conv_b, seq_len, attn_dtype, kv_stack=None):
    n = x.shape[0]
    tm = 512
    shift, shift_spec = _mod_operand(shift, tm, seq_len)
    scale, scale_spec = _mod_operand(scale, tm, seq_len)
    out_shape = [jax.ShapeDtypeStruct((n, hi - lo), attn_dtype if name in ("nq", "nk", "nv") else F32)
                 for name, lo, hi in _IN_SEGS]
    out_specs = [pl.BlockSpec((tm, hi - lo), lambda i: (i, 0)) for _, lo, hi in _IN_SEGS]
    assert seq_len % tm == 0 or tm % seq_len == 0
    nb8 = n // 8
    in_specs = [pl.BlockSpec((tm, D), lambda i: (i, 0)),
                pl.BlockSpec((8, D), lambda i: (jnp.maximum(i * (tm // 8) - 1, 0), 0)),
                pl.BlockSpec((8, D), lambda i: (jnp.minimum((i + 1) * (tm // 8), nb8 - 1), 0)),
                pl.BlockSpec((1, D), lambda i: (0, 0)),
                shift_spec,
                scale_spec,
                pl.BlockSpec((D, W_PACKED), lambda i: (0, 0)),
                pl.BlockSpec((3, D_XBC), lambda i: (0, 0)),
                pl.BlockSpec((1, D_XBC), lambda i: (0, 0))]
    args = [x, x, x, norm_g.reshape(1, D), shift, scale, w_packed, conv_w, conv_b.reshape(1, D_XBC)]
    aliases = {}
    n_alias = 0
    if kv_stack is not None:
        layer, depth, k_prev, v_prev = kv_stack
        bpt = tm // seq_len
        stack_shape = jax.ShapeDtypeStruct((n // seq_len, depth, seq_len, D_NA), F32)
        for prev in (k_prev, v_prev):
            out_shape.append(stack_shape)
            out_specs.append(pl.BlockSpec((bpt, None, seq_len, D_NA), lambda i: (i, layer, 0, 0)))
            if prev is not None:
                aliases[len(args)] = len(out_shape) - 1
                in_specs.append(pl.BlockSpec(memory_space=pl.ANY))
                args.append(prev)
                n_alias += 1
    return pl.pallas_call(
        functools.partial(_inproj_kernel, n_alias=n_alias, n_seg=len(_IN_SEGS), seq_len=seq_len),
        grid=(n // tm,),
        in_specs=in_specs,
        out_specs=tuple(out_specs),
        out_shape=tuple(out_shape),
        input_output_aliases=aliases,
        compiler_params=_cparams(("arbitrary",)),
    )(*args)


def _tri_masks(ch):
    r = lax.broadcasted_iota(jnp.int32, (ch, ch), 0)
    c = lax.broadcasted_iota(jnp.int32, (ch, ch), 1)
    return r >= c, r <= c


def _mlstm_kernel(qkv_f, qkv_b, gt_f, gt_b, fb_ref, c0_ref, m0_ref,
                  hf_ref, hb_ref, cfin_ref, mfin_ref, c_scr, m_scr, it_scr, bt_scr, rc_scr, vt_scr,
                  kq_scr, in_scr, *, n_tiles, ch):
    j = pl.program_id(1)

    @pl.when(j == 0)
    def _():
        c_scr[...] = c0_ref[...]
        m_scr[...] = m0_ref[...]

    tt = qkv_f.shape[0]
    nch = tt // ch
    fb = fb_ref[...]
    low, upp = _tri_masks(ch)
    ones_row = (lax.broadcasted_iota(jnp.int32, (HD, ch), 0) == 0).astype(F32)
    pad_rows = jnp.zeros((ch - 2 * H_M, ch), F32)

    dc = [(d, c) for d in range(2) for c in range(nch)]
    i_rows, logf_rows, b_rows = {}, {}, {}
    for d, c in dc:
        z_t = ((gt_f, gt_b)[d][c * ch:(c + 1) * ch, :] + fb).T
        i_rows[d, c] = z_t[MI0:MI0 + 2 * H_M, :]
        zf = z_t[MF0:MF0 + 2 * H_M, :]
        logf_rows[d, c] = jnp.minimum(zf, 0.0) - jnp.log1p(jnp.exp(-jnp.abs(zf)))
    for d, c in dc:
        mask = (upp, low)[d]
        b_rows[d, c] = jnp.dot(logf_rows[d, c], mask.astype(F32), precision=HI, preferred_element_type=F32)
        it_scr[d, c] = i_rows[d, c]
        bt_scr[d, c] = b_rows[d, c]
    for d, c in dc:
        rc_scr[d, c] = jnp.concatenate([i_rows[d, c] - b_rows[d, c], pad_rows], axis=0).T
    for d, c in dc:
        for p in range(H_M // 2):
            vt_scr[d, c, p * 2 * HD:(p + 1) * 2 * HD, :] = (
                (qkv_f, qkv_b)[d][c * ch:(c + 1) * ch, 2 * D_M + p * 2 * HD:2 * D_M + (p + 1) * 2 * HD].T)

    def chunk(ci, carry):
        for d in range(2):
            qkv_ref = (qkv_f, qkv_b)[d]
            c0 = pl.multiple_of((ci if d == 0 else nch - 1 - ci) * ch, ch)
            for h in range(H_M):
                idx = d * H_M + h
                q = qkv_ref[pl.ds(c0, ch), h * HD:(h + 1) * HD].astype(BF16)
                k = (qkv_ref[pl.ds(c0, ch), D_M + h * HD:D_M + (h + 1) * HD] * (HD ** -0.5)).astype(BF16)
                kq_scr[idx] = _dot_nt(k, q)
                in_scr[idx] = _dot_nt(c_scr[idx].astype(BF16), q)
        for d in range(2):
            qkv_ref = (qkv_f, qkv_b)[d]
            out_ref = (hf_ref, hb_ref)[d]
            mask = (upp, low)[d]
            cidx = ci if d == 0 else nch - 1 - ci
            c0 = pl.multiple_of(cidx * ch, ch)
            i_t = it_scr[d, cidx]
            b_t = bt_scr[d, cidx]
            r_cols = rc_scr[d, cidx]
            edge = ch - 1 if d == 0 else 0
            h_t = []
            for h in range(H_M):
                idx = d * H_M + h
                vaug = jnp.concatenate([vt_scr[d, cidx, h * HD:(h + 1) * HD, :], ones_row], axis=0)
                k = (qkv_ref[pl.ds(c0, ch), D_M + h * HD:D_M + (h + 1) * HD] * (HD ** -0.5)).astype(BF16)
                b_row = b_t[idx:idx + 1, :]
                i_row = i_t[idx:idx + 1, :]
                m_prev = m_scr[idx][0:1, 0:1]
                caug = c_scr[idx]
                logd = jnp.where(mask, r_cols[:, idx:idx + 1] + b_row, -jnp.inf)
                m_inter = b_row + m_prev
                m_t = jnp.maximum(jnp.max(logd, axis=0, keepdims=True), m_inter)
                dmat = jnp.exp(logd - m_t)
                w_inter = jnp.exp(m_inter - m_t)
                s = (kq_scr[idx] * dmat).astype(BF16)
                tot = _dot(vaug.astype(BF16), s) + w_inter * in_scr[idx]
                den = jnp.maximum(jnp.abs(tot[HD:HD + 1, :]), jnp.exp(-m_t))
                h_t.append(tot[0:HD, :] / den)
                if h % 2 == 1:
                    pair = jnp.concatenate(h_t[-2:], axis=0).T
                    out_ref[pl.ds(c0, ch), (h - 1) * HD:(h + 1) * HD] = pair
                b_l = b_row[:, edge:edge + 1]
                lw = b_l - b_row + i_row
                m_new = jnp.maximum(b_l + m_prev, jnp.max(lw, axis=1, keepdims=True))
                wk = jnp.exp(lw - m_new)
                decay = jnp.exp(b_l + m_prev - m_new)
                c_scr[idx] = decay * caug + _dot((vaug * wk).astype(BF16), k)
                m_scr[idx] = jnp.broadcast_to(m_new, (8, 128))
        return carry

    lax.fori_loop(0, nch, chunk, 0)

    @pl.when(j == n_tiles - 1)
    def _():
        cfin_ref[...] = c_scr[...]
        mfin_ref[...] = m_scr[...]


def _mlstm_scan(qkv, gate, f_bias, caug0, m0, batch, seq_len, layer=0, depth=1, prev=None):
    n = qkv.shape[0]
    tt = min(seq_len, 512)
    nt = seq_len // tt
    ch = 128
    fb = jnp.zeros((1, GATE_W), F32).at[0, MF0:MF0 + 2 * H_M].set(f_bias.reshape(-1))
    fwd = lambda b, j: (b * nt + j, 0)
    bwd = lambda b, j: (b * nt + nt - 1 - j, 0)
    st4 = lambda b, j: (b, 0, 0, 0)
    st5 = lambda b, j: (b, layer, 0, 0, 0)
    n_in = 7
    prev = tuple(prev or ())
    kern = functools.partial(_mlstm_kernel, n_tiles=nt, ch=ch)
    return pl.pallas_call(
        lambda *refs: kern(*refs[:n_in], *refs[n_in + len(prev):]),
        grid=(batch, nt),
        in_specs=[pl.BlockSpec((tt, 3 * D_M), fwd), pl.BlockSpec((tt, 3 * D_M), bwd),
                  pl.BlockSpec((tt, GATE_W), fwd), pl.BlockSpec((tt, GATE_W), bwd),
                  pl.BlockSpec((1, GATE_W), lambda b, j: (0, 0)),
                  pl.BlockSpec((None, 2 * H_M, 128, HD), st4),
                  pl.BlockSpec((None, 2 * H_M, 8, 128), st4)] + [pl.BlockSpec(memory_space=pl.ANY)] * len(prev),
        out_specs=(pl.BlockSpec((tt, D_M), fwd), pl.BlockSpec((tt, D_M), bwd),
                   pl.BlockSpec((None, None, 2 * H_M, 128, HD), st5),
                   pl.BlockSpec((None, None, 2 * H_M, 8, 128), st5)),
        out_shape=(jax.ShapeDtypeStruct((n, D_M), F32), jax.ShapeDtypeStruct((n, D_M), F32),
                   jax.ShapeDtypeStruct((batch, depth, 2 * H_M, 128, HD), F32),
                   jax.ShapeDtypeStruct((batch, depth, 2 * H_M, 8, 128), F32)),
        input_output_aliases={n_in + i: 2 + i for i in range(len(prev))},
        scratch_shapes=[pltpu.VMEM((2 * H_M, 128, HD), F32), pltpu.VMEM((2 * H_M, 8, 128), F32),
                        pltpu.VMEM((2, tt // ch, 2 * H_M, ch), F32), pltpu.VMEM((2, tt // ch, 2 * H_M, ch), F32),
                        pltpu.VMEM((2, tt // ch, ch, 128), F32), pltpu.VMEM((2, tt // ch, D_M, ch), F32),
                        pltpu.VMEM((2 * H_M, ch, ch), F32), pltpu.VMEM((2 * H_M, 128, ch), F32)],
        compiler_params=_cparams(("arbitrary", "arbitrary")),
    )(qkv, qkv, gate, gate, fb, caug0, m0, *prev)


def _ssd_kernel(xbc_f, xbc_b, gt_f, gt_b, dtb_ref, alog_ref, h0_ref,
                yf_ref, yb_ref, hfin_ref, h_scr, dtt_scr, at_scr, ac_scr, xt_scr, cb_scr, in_scr, *, n_tiles, ch):
    j = pl.program_id(1)

    @pl.when(j == 0)
    def _():
        h_scr[...] = h0_ref[...]

    tt = xbc_f.shape[0]
    nch = tt // ch
    dtb = dtb_ref[...]
    a_neg = -jnp.exp(alog_ref[...])
    low, upp = _tri_masks(ch)
    nrow = 16
    pad_rows = jnp.zeros((ch - nrow, ch), F32)

    dc = [(d, c) for d in range(2) for c in range(nch)]
    da_rows, a_rows = {}, {}
    for d, c in dc:
        dt = _softplus((gt_f, gt_b)[d][c * ch:(c + 1) * ch, :] + dtb)
        dtt_scr[d, c] = dt.T[DT0:DT0 + nrow, :]
        da_rows[d, c] = (dt * a_neg).T[DT0:DT0 + nrow, :]
    for d, c in dc:
        mask = (upp, low)[d]
        a_rows[d, c] = jnp.dot(da_rows[d, c], mask.astype(F32), precision=HI, preferred_element_type=F32)
        at_scr[d, c] = a_rows[d, c]
    for d, c in dc:
        ac_scr[d, c] = jnp.concatenate([a_rows[d, c], pad_rows], axis=0).T
    for d, c in dc:
        for p in range(H_S // 2):
            xt_scr[d, c, p * 2 * HD:(p + 1) * 2 * HD, :] = (
                (xbc_f, xbc_b)[d][c * ch:(c + 1) * ch, p * 2 * HD:(p + 1) * 2 * HD].T)

    def chunk(ci, carry):
        for d in range(2):
            x_ref = (xbc_f, xbc_b)[d]
            c0 = pl.multiple_of((ci if d == 0 else nch - 1 - ci) * ch, ch)
            for grp in range(G_S):
                bm = x_ref[pl.ds(c0, ch), D_S + grp * N_S:D_S + (grp + 1) * N_S].astype(BF16)
                cm = x_ref[pl.ds(c0, ch), D_S + G_S * N_S + grp * N_S:D_S + G_S * N_S + (grp + 1) * N_S].astype(BF16)
                cb_scr[d * G_S + grp] = _dot_nt(bm, cm)
                for r in range(R_S):
                    idx = d * H_S + grp * R_S + r
                    in_scr[idx] = _dot_nt(h_scr[idx].astype(BF16), cm)
        for d in range(2):
            x_ref = (xbc_f, xbc_b)[d]
            out_ref = (yf_ref, yb_ref)[d]
            mask = (upp, low)[d]
            cidx = ci if d == 0 else nch - 1 - ci
            c0 = pl.multiple_of(cidx * ch, ch)
            dt_t = dtt_scr[d, cidx]
            a_t = at_scr[d, cidx]
            a_cols = ac_scr[d, cidx]
            edge = ch - 1 if d == 0 else 0
            y_t = []
            for grp in range(G_S):
                bm = x_ref[pl.ds(c0, ch), D_S + grp * N_S:D_S + (grp + 1) * N_S].astype(BF16)
                for r in range(R_S):
                    hh = grp * R_S + r
                    idx = d * H_S + hh
                    a_row = a_t[idx:idx + 1, :]
                    dt_row = dt_t[idx:idx + 1, :]
                    seg = jnp.exp(jnp.where(mask, a_row - a_cols[:, idx:idx + 1], -jnp.inf))
                    mm = (cb_scr[d * G_S + grp] * seg).astype(BF16)
                    xh = xt_scr[d, cidx, hh * HD:(hh + 1) * HD, :]
                    h_st = h_scr[idx]
                    y_t.append(_dot((xh * dt_row).astype(BF16), mm) + in_scr[idx] * jnp.exp(a_row))
                    if hh % 2 == 1:
                        pair = jnp.concatenate(y_t[-2:], axis=0).T
                        out_ref[pl.ds(c0, ch), (hh - 1) * HD:(hh + 1) * HD] = pair
                    a_l = a_row[:, edge:edge + 1]
                    wk = jnp.exp(a_l - a_row) * dt_row
                    h_scr[idx] = jnp.exp(a_l) * h_st + _dot((xh * wk).astype(BF16), bm)
        return carry

    lax.fori_loop(0, nch, chunk, 0)

    @pl.when(j == n_tiles - 1)
    def _():
        hfin_ref[...] = h_scr[...]


def _ssd_scan(xbc, gate, dt_bias, a_log, h0_t, batch, seq_len, layer=0, depth=1, prev=None):
    n = xbc.shape[0]
    tt = min(seq_len, 512)
    nt = seq_len // tt
    ch = 128
    dtb = jnp.zeros((1, GATE_W), F32).at[0, DT0:DT0 + 2 * H_S].set(dt_bias.reshape(-1))
    alog = jnp.zeros((1, GATE_W), F32).at[0, DT0:DT0 + 2 * H_S].set(a_log.reshape(-1))
    fwd = lambda b, j: (b * nt + j, 0)
    bwd = lambda b, j: (b * nt + nt - 1 - j, 0)
    st4 = lambda b, j: (b, 0, 0, 0)
    n_in = 7
    prev = tuple(prev or ())
    kern = functools.partial(_ssd_kernel, n_tiles=nt, ch=ch)
    return pl.pallas_call(
        lambda *refs: kern(*refs[:n_in], *refs[n_in + len(prev):]),
        grid=(batch, nt),
        in_specs=[pl.BlockSpec((tt, D_XBC), fwd), pl.BlockSpec((tt, D_XBC), bwd),
                  pl.BlockSpec((tt, GATE_W), fwd), pl.BlockSpec((tt, GATE_W), bwd),
                  pl.BlockSpec((1, GATE_W), lambda b, j: (0, 0)),
                  pl.BlockSpec((1, GATE_W), lambda b, j: (0, 0)),
                  pl.BlockSpec((None, 2 * H_S, N_S, HD), st4)] + [pl.BlockSpec(memory_space=pl.ANY)] * len(prev),
        out_specs=(pl.BlockSpec((tt, D_S), fwd), pl.BlockSpec((tt, D_S), bwd),
                   pl.BlockSpec((None, None, 2 * H_S, N_S, HD), lambda b, j: (b, layer, 0, 0, 0))),
        out_shape=(jax.ShapeDtypeStruct((n, D_S), F32), jax.ShapeDtypeStruct((n, D_S), F32),
                   jax.ShapeDtypeStruct((batch, depth, 2 * H_S, N_S, HD), F32)),
        input_output_aliases={n_in + i: 2 + i for i in range(len(prev))},
        scratch_shapes=[pltpu.VMEM((2 * H_S, N_S, HD), F32),
                        pltpu.VMEM((2, tt // ch, 16, ch), F32), pltpu.VMEM((2, tt // ch, 16, ch), F32),
                        pltpu.VMEM((2, tt // ch, ch, 128), F32), pltpu.VMEM((2, tt // ch, D_S, ch), F32),
                        pltpu.VMEM((2 * G_S, ch, ch), F32), pltpu.VMEM((2 * H_S, HD, ch), F32)],
        compiler_params=_cparams(("arbitrary", "arbitrary")),
    )(xbc, xbc, gate, gate, dtb, alog, h0_t, *prev)


def _ctx_attn_kernel(q_ref, k_ref, v_ref, o_ref, s_scr):
    n = q_ref.shape[0]
    first = lax.broadcasted_iota(jnp.int32, (n, 2 * HD), 1) < HD
    ones = jnp.ones((n, 2 * HD), BF16)
    for p in range(H_NA // 2):
        sl = slice(p * 2 * HD, (p + 1) * 2 * HD)
        qp = q_ref[:, sl]
        kp = k_ref[:, sl].astype(BF16)
        for half in range(2):
            mine = first if half == 0 else jnp.logical_not(first)
            q = (jnp.where(mine, qp, jnp.zeros_like(qp)) * (HD ** -0.5)).astype(BF16)
            s_scr[2 * p + half] = _dot_nt(q, kp)
    for p in range(H_NA // 2):
        sl = slice(p * 2 * HD, (p + 1) * 2 * HD)
        vp = v_ref[:, sl].astype(BF16)
        outs = []
        for half in range(2):
            s = s_scr[2 * p + half]
            pr = jnp.exp(s - jnp.max(s, axis=-1, keepdims=True)).astype(BF16)
            outs.append(_dot(pr, vp) / _dot(pr, ones))
        o_ref[:, sl] = jnp.where(first, outs[0], outs[1])


def _ctx_attention(q, k, v, batch, seq_len):
    spec = pl.BlockSpec((seq_len, D_NA), lambda b: (b, 0))
    return pl.pallas_call(
        _ctx_attn_kernel,
        grid=(batch,),
        in_specs=[spec, spec, spec],
        out_specs=spec,
        out_shape=jax.ShapeDtypeStruct(q.shape, F32),
        scratch_shapes=[pltpu.VMEM((H_NA, seq_len, seq_len), F32)],
        compiler_params=_cparams(("arbitrary",)),
    )(q, k, v)


def _row_start(r, rows):
    return jnp.clip(r - WIN_H // 2, 0, rows - WIN_H)


def _natten_kernel(q_ref, k_ref, v_ref, ck_ref, cv_ref, bias_ref, o_ref, sl_scr, sc_scr, *, rows):
    r = pl.program_id(1)
    k0 = pl.multiple_of(_row_start(r, rows) * GRID_W, GRID_W)
    off0 = _row_start(r, rows) - r + WIN_H - 1
    nloc = WIN_H * GRID_W
    npast = ck_ref.shape[0]
    first = lax.broadcasted_iota(jnp.int32, (GRID_W, 2 * HD), 1) < HD
    ones_loc = jnp.ones((nloc, 2 * HD), BF16)
    ones_ctx = jnp.ones((npast, 2 * HD), BF16)
    for p in range(H_NA // 2):
        sl = slice(p * 2 * HD, (p + 1) * 2 * HD)
        qp = q_ref[:, sl]
        kw = k_ref[pl.ds(k0, nloc), sl].astype(BF16)
        ckp = ck_ref[:, sl].astype(BF16)
        for half in range(2):
            mine = first if half == 0 else jnp.logical_not(first)
            q = (jnp.where(mine, qp, jnp.zeros_like(qp)) * (HD ** -0.5)).astype(BF16)
            bias = jnp.concatenate([bias_ref[2 * p + half, off0 + 2 * i] for i in range(WIN_H // 2)], axis=1)
            sl_scr[2 * p + half] = _dot_nt(q, kw) + bias
            sc_scr[2 * p + half] = _dot_nt(q, ckp)
    for p in range(H_NA // 2):
        sl = slice(p * 2 * HD, (p + 1) * 2 * HD)
        vw = v_ref[pl.ds(k0, nloc), sl].astype(BF16)
        cvp = cv_ref[:, sl].astype(BF16)
        outs = []
        for half in range(2):
            s_loc = sl_scr[2 * p + half]
            s_ctx = sc_scr[2 * p + half]
            m = jnp.maximum(jnp.max(s_loc, axis=-1, keepdims=True), jnp.max(s_ctx, axis=-1, keepdims=True))
            p_loc = jnp.exp(s_loc - m).astype(BF16)
            p_ctx = jnp.exp(s_ctx - m).astype(BF16)
            l = _dot(p_loc, ones_loc) + _dot(p_ctx, ones_ctx)
            outs.append((_dot(p_loc, vw) + _dot(p_ctx, cvp)) / l)
        o_ref[:, sl] = jnp.where(first, outs[0], outs[1])


def _natten_bias(rpb, rows):
    qc = np.arange(GRID_W)[:, None]
    kc = np.arange(GRID_W)[None, :]
    cstart = np.clip(qc - WIN_W // 2, 0, GRID_W - WIN_W)
    ok = (kc >= cstart) & (kc < cstart + WIN_W)
    col_off = np.clip(kc - qc + WIN_W - 1, 0, 2 * WIN_W - 2)
    pick = jnp.asarray((col_off[None] == np.arange(2 * WIN_W - 1)[:, None, None]).astype(np.float32))
    toep = jnp.einsum("hoj,jqk->hoqk", rpb.astype(F32), pick, precision=HI)
    t = jnp.where(jnp.asarray(ok)[None, None], toep, -jnp.inf)
    return jnp.concatenate([t[:, :-1], t[:, 1:]], axis=-1)


def _neighbourhood_attention(q, k, v, ck, cv, bias, batch, seq_len):
    rows = seq_len // GRID_W
    past = ck.shape[0] // batch
    return pl.pallas_call(
        functools.partial(_natten_kernel, rows=rows),
        grid=(batch, rows),
        in_specs=[pl.BlockSpec((GRID_W, D_NA), lambda b, r: (b * rows + r, 0)),
                  pl.BlockSpec((seq_len, D_NA), lambda b, r: (b, 0)),
                  pl.BlockSpec((seq_len, D_NA), lambda b, r: (b, 0)),
                  pl.BlockSpec((past, D_NA), lambda b, r: (b, 0)),
                  pl.BlockSpec((past, D_NA), lambda b, r: (b, 0)),
                  pl.BlockSpec((H_NA, 2 * WIN_H - 2, GRID_W, 2 * GRID_W), lambda b, r: (0, 0, 0, 0))],
        out_specs=pl.BlockSpec((GRID_W, D_NA), lambda b, r: (b * rows + r, 0)),
        out_shape=jax.ShapeDtypeStruct(q.shape, F32),
        scratch_shapes=[pltpu.VMEM((H_NA, GRID_W, WIN_H * GRID_W), F32), pltpu.VMEM((H_NA, GRID_W, past), F32)],
        compiler_params=_cparams(("arbitrary", "arbitrary")),
    )(q, k, v, ck, cv, bias)


def _outproj_kernel(x_ref, hf_ref, hb_ref, mo_ref, yf_ref, yb_ref, xbc_ref, sz_ref, a_ref,
                    mg_ref, dsk_ref, sg_ref, bd_ref, w_ref, g1_ref, n2_ref, sh2_ref, sc2_ref, rw_ref,
                    xo_ref, h2_ref, aff_ref):
    hm = hf_ref[...] + hb_ref[...]
    ssq = jnp.dot(hm * hm, bd_ref[...], precision=HI, preferred_element_type=F32)
    m_out = hm * lax.rsqrt(ssq * (1.0 / HD) + EPS) * mg_ref[...] * _sigmoid(mo_ref[...])
    sz = sz_ref[...]
    ys = (yf_ref[...] + yb_ref[...] + dsk_ref[...] * xbc_ref[:, 0:D_S]) * (sz * _sigmoid(sz))
    s_out = ys * lax.rsqrt(jnp.mean(ys * ys, axis=-1, keepdims=True) + EPS) * sg_ref[...]
    y = (_dot(m_out.astype(BF16), w_ref[0:D_M, :])
         + _dot(s_out.astype(BF16), w_ref[D_M:D_M + D_S, :])
         + _dot(a_ref[...].astype(BF16), w_ref[D_M + D_S:D, :]))
    xn = x_ref[...] + g1_ref[...] * y
    xo_ref[...] = xn
    h2 = xn * lax.rsqrt(jnp.mean(xn * xn, axis=-1, keepdims=True) + EPS) * n2_ref[...]
    h2 = h2 * (1.0 + sc2_ref[...]) + sh2_ref[...]
    for c in range(h2_ref.shape[0]):
        h2_ref[c] = h2[c * GATHER_TOK:(c + 1) * GATHER_TOK, :].T.astype(BF16)
    logits = lax.dot_general(rw_ref[...], h2, (((1,), (1,)), ((), ())), precision=HI,
                             preferred_element_type=F32)
    mx = jnp.max(logits, axis=0, keepdims=True)
    ex = jnp.exp(logits - mx)
    aff_ref[...] = ex / jnp.sum(ex, axis=0, keepdims=True)


def _out_projection(x, hf, hb, mo, yf, yb, xbc, sz, a_out, mnorm_g, d_skip, snorm_g, w_out_bf,
                    g1, norm2_g, sh2, sc2, router_wt, seq_len):
    n = x.shape[0]
    tm = 512
    cpt = tm // GATHER_TOK
    g1, g1_spec = _mod_operand(g1, tm, seq_len)
    sh2, sh2_spec = _mod_operand(sh2, tm, seq_len)
    sc2, sc2_spec = _mod_operand(sc2, tm, seq_len)
    row = lambda w: pl.BlockSpec((tm, w), lambda i: (i, 0))
    const = lambda s: pl.BlockSpec(s, lambda i: (0,) * len(s))
    hid = np.arange(D_M) // HD
    blockdiag = jnp.asarray((hid[:, None] == hid[None, :]).astype(np.float32))
    return pl.pallas_call(
        _outproj_kernel,
        grid=(n // tm,),
        in_specs=[row(D), row(D_M), row(D_M), row(D_M), row(D_S), row(D_S), row(D_XBC), row(D_S), row(D_NA),
                  const((1, D_M)), const((1, D_S)), const((1, D_S)), const((D_M, D_M)), const((D, D)),
                  g1_spec, const((1, D)), sh2_spec, sc2_spec,
                  const((N_EXPERTS, D))],
        out_specs=(row(D), pl.BlockSpec((cpt, D, GATHER_TOK), lambda i: (i, 0, 0)),
                   pl.BlockSpec((N_EXPERTS, tm), lambda i: (0, i))),
        out_shape=(jax.ShapeDtypeStruct((n, D), F32), jax.ShapeDtypeStruct((n // GATHER_TOK, D, GATHER_TOK), BF16),
                   jax.ShapeDtypeStruct((N_EXPERTS, n), F32)),
        compiler_params=_cparams(("arbitrary",)),
    )(x, hf, hb, mo, yf, yb, xbc, sz, a_out,
      mnorm_g.reshape(1, D_M), jnp.repeat(d_skip, HD).reshape(1, D_S), snorm_g.reshape(1, D_S), blockdiag,
      w_out_bf, g1, norm2_g.reshape(1, D), sh2, sc2, router_wt)


def _select_kernel(aff_ref, gate_ref, slot_ref, start_ref, *, cap):
    aff = aff_ref[...]
    bits = pltpu.bitcast(aff, jnp.int32)
    n_tok = aff.shape[1]

    def step(i, prefix):
        cand = prefix | (jnp.int32(1) << (30 - i))
        cnt = jnp.sum((bits >= cand).astype(F32), axis=1, keepdims=True)
        return jnp.where(cnt >= cap, cand, prefix)

    thr = lax.fori_loop(0, 31, step, jnp.zeros((aff.shape[0], 1), jnp.int32))
    gt = bits > thr
    eq = bits == thr
    need = cap - jnp.sum(gt.astype(F32), axis=1, keepdims=True)
    r_i = lax.broadcasted_iota(jnp.int32, (128, 128), 0)
    c_i = lax.broadcasted_iota(jnp.int32, (128, 128), 1)
    strict = (r_i < c_i).astype(BF16)
    run_eq = jnp.zeros((aff.shape[0], 1), F32)
    run_sel = jnp.zeros((aff.shape[0], 1), F32)
    lane = lax.broadcasted_iota(jnp.int32, (aff.shape[0], 128), 1)
    starts = jnp.zeros((aff.shape[0], 128), F32)
    for blk in range(n_tok // TOK_BLK):
        sl = slice(blk * TOK_BLK, (blk + 1) * TOK_BLK)
        e = eq[:, sl]
        rank = _dot(e.astype(BF16), strict) + run_eq
        keep = gt[:, sl] | (e & (rank < need))
        kf = jnp.where(keep, 1.0, 0.0)
        slot = _dot(kf.astype(BF16), strict) + run_sel
        gate_ref[:, sl] = jnp.where(keep, aff[:, sl], 0.0)
        slot_ref[:, sl] = jnp.where(keep, slot, -1.0).astype(jnp.int32)
        starts = jnp.where(lane == blk, run_sel, starts)
        run_eq = run_eq + jnp.sum(e.astype(F32), axis=1, keepdims=True)
        run_sel = run_sel + jnp.sum(kf, axis=1, keepdims=True)
    start_ref[...] = starts.astype(jnp.int32)


def _select(aff_t):
    n = aff_t.shape[1]
    cap = EC_FACTOR * n // N_EXPERTS
    assert n // TOK_BLK <= 128
    return pl.pallas_call(
        functools.partial(_select_kernel, cap=float(cap)),
        out_shape=(jax.ShapeDtypeStruct(aff_t.shape, F32), jax.ShapeDtypeStruct(aff_t.shape, jnp.int32),
                   jax.ShapeDtypeStruct((N_EXPERTS, 128), jnp.int32)),
        compiler_params=pltpu.CompilerParams(vmem_limit_bytes=VMEM_LIMIT),
    )(aff_t)


def _gather_rows(e, lo_ref, hi_ref, slot_ref, h2t_ref, xe_scr, acc_scr, col_scr, cap):
    nsb = cap // GATHER_SLOT
    nchunk = h2t_ref.shape[0]
    fill = jnp.zeros((TOK_BLK - 8, GATHER_TOK), F32)
    for c in range(nchunk):
        row = jnp.broadcast_to(slot_ref[:, c * GATHER_TOK:(c + 1) * GATHER_TOK].astype(F32), (8, GATHER_TOK))
        col_scr[c] = jnp.concatenate([row, fill], axis=0).T
    lane = lax.broadcasted_iota(jnp.int32, (GATHER_TOK, GATHER_SLOT), 1).astype(F32)
    win = min(GATHER_WIN, nchunk)

    def onehot_t(c, want):
        return jnp.where(col_scr[c][:, 0:1] == want, 1.0, 0.0).astype(BF16)

    for sb in range(nsb):
        want = lane + float(sb * GATHER_SLOT)
        first = jnp.minimum(lo_ref[e * nsb + sb], nchunk - win)
        acc = _dot(h2t_ref[first], onehot_t(first, want))
        for i in range(1, win):
            acc = acc + _dot(h2t_ref[first + i], onehot_t(first + i, want))
        acc_scr[...] = acc

        def body(c, carry):
            acc_scr[...] += _dot(h2t_ref[c], onehot_t(c, want))
            return carry

        lax.fori_loop(first + win, hi_ref[e * nsb + sb] + 1, body, 0)
        xe_scr[sb * GATHER_SLOT:(sb + 1) * GATHER_SLOT, :] = acc_scr[...].T.astype(BF16)


def _expert_kernel(lo_ref, hi_ref, slot_ref, h2t_ref, w1_ref, w3_ref, w2_ref, ye_ref,
                   xe_scr, acc_scr, col_scr, y_scr, *, cap):
    e = pl.program_id(0)
    f = pl.program_id(1)

    @pl.when(f == 0)
    def _():
        _gather_rows(e, lo_ref, hi_ref, slot_ref, h2t_ref, xe_scr, acc_scr, col_scr, cap)

    x = xe_scr[...]
    a = _dot(x, w1_ref[...].astype(BF16))
    b = _dot(x, w3_ref[...].astype(BF16))
    hid = (a * _sigmoid(a) * b).astype(BF16)
    part = _dot(hid, w2_ref[...].astype(BF16))

    @pl.when(f == 0)
    def _():
        y_scr[...] = part

    @pl.when(f == 1)
    def _():
        ye_ref[...] = (y_scr[...] + part).astype(BF16)


def _combine_kernel(win_ref, slot_ref, gate_ref, x_ref, g2_ref, fg_ref, ye_ref, o_ref, acc_scr, *, final):
    lane = lax.broadcasted_iota(jnp.int32, (TOK_BLK, 2 * SLOT_BLK), 1)
    fill = jnp.zeros((TOK_BLK - N_EXPERTS, TOK_BLK), F32)
    for s in range(COMBINE_BLKS):
        tb = pl.program_id(0) * COMBINE_BLKS + s
        rows = slice(s * TOK_BLK, (s + 1) * TOK_BLK)
        slot_t = jnp.concatenate([slot_ref[:, rows].astype(F32), fill], axis=0).T
        gate_t = jnp.concatenate([gate_ref[:, rows], fill], axis=0).T
        acc_scr[...] = jnp.zeros_like(acc_scr)
        for e in range(N_EXPERTS):
            s0 = pl.multiple_of(win_ref[tb * N_EXPERTS + e] * SLOT_BLK, SLOT_BLK)
            want = (lane + s0).astype(F32)
            onehot = jnp.where(slot_t[:, e:e + 1] == want, 1.0, 0.0).astype(BF16)
            acc_scr[...] += _dot(onehot, ye_ref[e, pl.ds(s0, 2 * SLOT_BLK), :]) * gate_t[:, e:e + 1]
        y = x_ref[rows, :] + g2_ref[...] * acc_scr[...]
        if final:
            y = y * lax.rsqrt(jnp.mean(y * y, axis=-1, keepdims=True) + EPS) * fg_ref[...]
        o_ref[rows, :] = y


def _moe(h2, aff_t, x, g2, w1, w3, w2, layer, seq_len, final_g=None):
    n = x.shape[0]
    cap = EC_FACTOR * n // N_EXPERTS
    nsb = cap // SLOT_BLK
    ntb = n // TOK_BLK
    gate, slot, starts = _select(aff_t)

    cs = starts[:, :ntb]
    win = jnp.minimum(cs // SLOT_BLK, nsb - 2).T.astype(jnp.int32)
    gs = cs[:, ::GATHER_TOK // TOK_BLK]
    gend = jnp.concatenate([gs[:, 1:], jnp.full((N_EXPERTS, 1), cap, jnp.int32)], axis=1)
    edges = jnp.arange(cap // GATHER_SLOT, dtype=jnp.int32) * GATHER_SLOT
    lo = jnp.sum(gend[:, None, :] <= edges[None, :, None], axis=-1).astype(jnp.int32)
    hi = jnp.sum(gs[:, None, :] < (edges + GATHER_SLOT)[None, :, None], axis=-1).astype(jnp.int32) - 1

    d_ff = w1.shape[-1]
    up_spec = pl.BlockSpec((None, None, D, d_ff // 2), lambda e, f, *_: (layer, e, 0, f))
    down_spec = pl.BlockSpec((None, None, d_ff // 2, D), lambda e, f, *_: (layer, e, f, 0))
    ye = pl.pallas_call(
        functools.partial(_expert_kernel, cap=cap),
        grid_spec=pltpu.PrefetchScalarGridSpec(
            num_scalar_prefetch=2,
            grid=(N_EXPERTS, 2),
            in_specs=[pl.BlockSpec((None, 1, n), lambda e, f, *_: (e, 0, 0)),
                      pl.BlockSpec((n // GATHER_TOK, D, GATHER_TOK), lambda e, f, *_: (0, 0, 0),
                                   pipeline_mode=pl.Buffered(1)),
                      up_spec, up_spec, down_spec],
            out_specs=pl.BlockSpec((None, cap, D), lambda e, f, *_: (e, 0, 0)),
            scratch_shapes=[pltpu.VMEM((cap, D), BF16), pltpu.VMEM((D, GATHER_SLOT), F32),
                            pltpu.VMEM((n // GATHER_TOK, GATHER_TOK, TOK_BLK), F32),
                            pltpu.VMEM((cap, D), F32)]),
        out_shape=jax.ShapeDtypeStruct((N_EXPERTS, cap, D), BF16),
        compiler_params=_cparams(("arbitrary", "arbitrary")),
    )(lo.reshape(-1), hi.reshape(-1), slot.reshape(N_EXPERTS, 1, n), h2, w1, w3, w2)

    tc = COMBINE_BLKS * TOK_BLK
    g2, g2_spec = _mod_operand(g2, tc, seq_len)
    fg = jnp.ones((1, D), F32) if final_g is None else final_g.reshape(1, D)
    return pl.pallas_call(
        functools.partial(_combine_kernel, final=final_g is not None),
        grid_spec=pltpu.PrefetchScalarGridSpec(
            num_scalar_prefetch=1,
            grid=(n // tc,),
            in_specs=[pl.BlockSpec((N_EXPERTS, tc), lambda i, *_: (0, i)),
                      pl.BlockSpec((N_EXPERTS, tc), lambda i, *_: (0, i)),
                      pl.BlockSpec((tc, D), lambda i, *_: (i, 0)),
                      g2_spec,
                      pl.BlockSpec((1, D), lambda i, *_: (0, 0)),
                      pl.BlockSpec((N_EXPERTS, cap, D), lambda i, *_: (0, 0, 0), pipeline_mode=pl.Buffered(1))],
            out_specs=pl.BlockSpec((tc, D), lambda i, *_: (i, 0)),
            scratch_shapes=[pltpu.VMEM((TOK_BLK, D), F32)]),
        out_shape=jax.ShapeDtypeStruct((n, D), F32),
        compiler_params=_cparams(("arbitrary",)),
    )(win.reshape(-1), slot, gate, x, g2, fg, ye)


def _layer(x, mod, prm, batch, seq_len, caug0, m0, h0_t, ck=None, cv=None, bias=None, kv_stack=None,
           state_stack=None):
    st_layer, st_depth, st_cm, st_h = state_stack or (0, 1, None, None)
    sh1, sc1, g1, sh2, sc2, g2 = [(mod[0], prm["layer"], mod[1], col, mod[2]) for col in range(6)]
    attn_dtype = F32 if ck is None else BF16
    qkv, mo, sz, xbc, nq, nk, nv, gate, *kv_out = _in_projection(
        x, prm["norm1_g"], sh1, sc1, prm["w_in"], prm["conv_w"], prm["conv_b"], seq_len, attn_dtype, kv_stack)
    hf, hb, cfin, mfin = _mlstm_scan(qkv, gate, prm["f_bias"], caug0, m0, batch, seq_len, st_layer, st_depth, st_cm)
    yf, yb, hfin = _ssd_scan(xbc, gate, prm["dt_bias"], prm["a_log"], h0_t, batch, seq_len, st_layer, st_depth, st_h)
    if ck is None:
        a_out = _ctx_attention(nq, nk, nv, batch, seq_len)
    else:
        a_out = _neighbourhood_attention(nq, nk, nv, ck, cv, bias, batch, seq_len)
    xn, h2, aff_t = _out_projection(x, hf, hb, mo, yf, yb, xbc, sz, a_out, prm["mnorm_g"], prm["d_skip"],
                                    prm["snorm_g"], prm["w_out"], g1, prm["norm2_g"], sh2, sc2,
                                    prm["router_wt"], seq_len)
    xo = _moe(h2, aff_t, xn, g2, prm["w1"], prm["w3"], prm["w2"], prm["layer"], seq_len, prm["final_g"])
    return xo, kv_out, cfin, mfin, hfin


def _pack_w_in(w_in):
    o = np.cumsum((0, D_M, D_M, D_M, D_M, 2 * H_M, 2 * H_M, D_S, D_XBC, 2 * H_S, D_NA, D_NA, D_NA))
    pad = jnp.zeros(w_in.shape[:-1] + (GATE_W - 4 * H_M - 2 * H_S,), w_in.dtype)
    parts = [w_in[..., o[0]:o[4]], w_in[..., o[6]:o[8]], w_in[..., o[9]:o[12]],
             w_in[..., o[4]:o[6]], w_in[..., o[8]:o[9]], pad]
    return jnp.concatenate(parts, axis=-1).astype(BF16)


def kernel(x_prompt, x_sample, cache_na_k, cache_na_v, state_mlstm_c, state_mlstm_n, state_mlstm_m, state_ssm, c, c_ctx, ada_w, ada_b, norm1_g, norm2_g, w_in, mlstm_f_bias, mlstm_norm_g, conv_w, conv_b, ssm_dt_bias, ssm_a_log, ssm_d, ssm_norm_g, na_rpb, w_out, router_w, exp_w1, exp_w3, exp_w2, final_g):
    bp, sp, _ = x_prompt.shape
    bs, ss, _ = x_sample.shape
    past = cache_na_k.shape[2]

    cvec = jnp.zeros((8, D), F32).at[0].set(c_ctx).at[1:1 + bs].set(c)
    mod4 = _modulation(cvec, ada_w, ada_b).reshape(DEPTH, 8, 1, 6 * D)

    w_in_p = _pack_w_in(w_in)
    w_out_bf = w_out.astype(BF16)
    router_wt = jnp.swapaxes(router_w, 1, 2)

    def aug(cs, ns):
        pad = jnp.zeros(cs.shape[:-2] + (128 - HD - 1, HD), F32)
        out = jnp.concatenate([jnp.swapaxes(cs, -1, -2), ns[..., None, :], pad], axis=-2)
        return out.reshape(cs.shape[:2] + (2 * H_M, 128, HD))

    lat_caug_all = aug(state_mlstm_c, state_mlstm_n)

    def rep_m(ms):
        b = ms.shape[0]
        return jnp.broadcast_to(ms.reshape(b, 2 * H_M, 1, 1), (b, 2 * H_M, 8, 128)).astype(F32)

    ctx_caug0 = jnp.zeros((bp, 2 * H_M, 128, HD), F32)
    ctx_m0 = jnp.full((bp, 2 * H_M, 8, 128), NEG_INIT, F32)
    ctx_h0 = jnp.zeros((bp, 2 * H_S, N_S, HD), F32)

    xp = x_prompt.reshape(bp * sp, D)
    xs = x_sample.reshape(bs * ss, D)
    k_stack = jnp.zeros((bp, DEPTH, sp, D_NA), F32)
    v_stack = jnp.zeros((bp, DEPTH, sp, D_NA), F32)
    cm_stack = (jnp.zeros((bp, DEPTH, 2 * H_M, 128, HD), F32), jnp.zeros((bp, DEPTH, 2 * H_M, 8, 128), F32))
    h_stack = (jnp.zeros((bp, DEPTH, 2 * H_S, N_S, HD), F32),)
    for l in range(DEPTH):
        prm = dict(norm1_g=norm1_g[l], norm2_g=norm2_g[l], w_in=w_in_p[l], f_bias=mlstm_f_bias[l],
                   mnorm_g=mlstm_norm_g[l], conv_w=conv_w[l], conv_b=conv_b[l], dt_bias=ssm_dt_bias[l],
                   a_log=ssm_a_log[l], d_skip=ssm_d[l], snorm_g=ssm_norm_g[l], w_out=w_out_bf[l],
                   router_wt=router_wt[l], w1=exp_w1, w3=exp_w3, w2=exp_w2, layer=l,
                   final_g=final_g if l == DEPTH - 1 else None)
        xp, (k_stack, v_stack), c_all, m_all, h_all = _layer(
            xp, (mod4, 0, False), prm, bp, sp, ctx_caug0, ctx_m0, ctx_h0, kv_stack=(l, DEPTH, k_stack, v_stack),
            state_stack=(l, DEPTH, cm_stack, h_stack))
        cm_stack, h_stack = (c_all, m_all), (h_all,)

        lat_caug0 = lat_caug_all[:, l]
        lat_m0 = rep_m(state_mlstm_m[:, l])
        lat_h0 = state_ssm[:, l].reshape(bs, 2 * H_S, HD, N_S)
        ck = cache_na_k[:, l].reshape(bs * past, D_NA).astype(BF16)
        cv = cache_na_v[:, l].reshape(bs * past, D_NA).astype(BF16)
        bias = _natten_bias(na_rpb[l], ss // GRID_W)
        xs = _layer(xs, (mod4, 1, True), prm, bs, ss, lat_caug0, lat_m0, lat_h0, ck, cv, bias)[0]

    y_prompt = xp.reshape(bp, sp, D)
    y_sample = xs.reshape(bs, ss, D)
    new_c = jnp.swapaxes(c_all[..., 0:HD, :], -1, -2).reshape(bp, DEPTH, 2, H_M, HD, HD)
    new_n = c_all[..., HD, :].reshape(bp, DEPTH, 2, H_M, HD)
    new_m = m_all[..., 0, 0].reshape(bp, DEPTH, 2, H_M)
    return (y_prompt, y_sample, k_stack.reshape(bp, DEPTH, sp, H_NA, HD), v_stack.reshape(bp, DEPTH, sp, H_NA, HD),
            new_c, new_n, new_m, h_all.reshape(bp, DEPTH, 2, H_S, HD, N_S))
```

```python
import functools

import numpy as np
import jax
import jax.numpy as jnp
from jax import lax
from jax.experimental import pallas as pl
from jax.experimental.pallas import tpu as pltpu

F32 = jnp.float32
BF16 = jnp.bfloat16
HI = lax.Precision.HIGHEST

D = 1024
DEPTH = 4
HD = 64
H_M = 4
D_M = H_M * HD
H_S = 6
D_S = H_S * HD
G_S = 2
R_S = H_S // G_S
N_S = 64
D_XBC = D_S + 2 * G_S * N_S
H_NA = 6
D_NA = H_NA * HD
GRID_W = 64
WIN_H = 8
WIN_W = 16
N_EXPERTS = 16
EC_FACTOR = 2
EPS = 1e-6
NEG_INIT = -1e30
TOK_BLK = 128
SLOT_BLK = 128
COMBINE_BLKS = 1
NATTEN_ROWS = 2
GATHER_TOK = 256
GATHER_SLOT = 256
GATHER_WIN = 10
GATE_W = 128
MI0, MF0, DT0 = 0, 2 * H_M, 4 * H_M
W_PACKED = 3 * D_M + D_M + D_S + D_XBC + 3 * D_NA + GATE_W

VMEM_LIMIT = 56 * 1024 * 1024


def _cparams(sem):
    return pltpu.CompilerParams(dimension_semantics=sem, vmem_limit_bytes=VMEM_LIMIT)


def _sigmoid(x):
    return 1.0 / (1.0 + jnp.exp(-x))


def _softplus(x):
    return jnp.maximum(x, 0.0) + jnp.log1p(jnp.exp(-jnp.abs(x)))


def _dot(a, b):
    return jnp.dot(a, b, preferred_element_type=F32)


def _dot_nt(a, b):
    return lax.dot_general(a, b, (((1,), (1,)), ((), ())), preferred_element_type=F32)


def _dot_tn(a, b):
    return lax.dot_general(a, b, (((0,), (0,)), ((), ())), preferred_element_type=F32)


def _mod_kernel(c_ref, w_ref, b_ref, o_ref):
    cv = c_ref[...]
    s = cv * _sigmoid(cv)
    o_ref[...] = jnp.dot(s, w_ref[...], precision=HI, preferred_element_type=F32) + b_ref[...]


def _modulation(cvec, ada_w, ada_b):
    tn = 1536
    return pl.pallas_call(
        _mod_kernel,
        grid=(DEPTH, 6 * D // tn),
        in_specs=[pl.BlockSpec((8, D), lambda l, j: (0, 0)),
                  pl.BlockSpec((None, D, tn), lambda l, j: (l, 0, j)),
                  pl.BlockSpec((None, 1, tn), lambda l, j: (l, 0, j))],
        out_specs=pl.BlockSpec((None, 8, tn), lambda l, j: (l, 0, j)),
        out_shape=jax.ShapeDtypeStruct((DEPTH, 8, 6 * D), F32),
        compiler_params=_cparams(("arbitrary", "arbitrary")),
    )(cvec, ada_w, ada_b.reshape(DEPTH, 1, 6 * D))


def _mod_operand(m, tm, seq_len):
    if not isinstance(m, tuple):
        per_seq = m.shape[0] > 1
        return m, pl.BlockSpec((None, 1, D), lambda i, *_: ((i * tm) // seq_len if per_seq else 0, 0, 0))
    table, layer, row0, col, per_seq = m
    if per_seq:
        idx = lambda i, *_: (layer, row0 + (i * tm) // seq_len, 0, col)
    else:
        idx = lambda i, *_: (layer, row0, 0, col)
    return table, pl.BlockSpec((None, None, 1, D), idx)


_IN_SEGS = (("qkv", 0, 768), ("mo", 768, 1024), ("sz", 1024, 1408), ("xbc", 1408, 2048),
            ("nq", 2048, 2432), ("nk", 2432, 2816), ("nv", 2816, 3200), ("gate", 3200, 3328))


def _inproj_kernel(x_ref, xp_ref, xn_ref, g_ref, sh_ref, sc_ref, w_ref, cw_ref, cb_ref, *rest,
                   n_alias, n_seg, seq_len):
    outs = rest[n_alias:]
    tm = x_ref.shape[0]

    def norm_mod(x):
        ms = jnp.mean(x * x, axis=-1, keepdims=True)
        h = x * lax.rsqrt(ms + EPS) * g_ref[...]
        return (h * (1.0 + sc_ref[...]) + sh_ref[...]).astype(BF16)

    hb = norm_mod(x_ref[...])
    vals = {}
    for o_ref, (name, lo, hi) in zip(outs[:n_seg], _IN_SEGS):
        vals[name] = _dot(hb, w_ref[:, lo:hi])
        if name == "xbc":
            cur = vals[name]
            edge = _dot(norm_mod(jnp.concatenate([xp_ref[...], xn_ref[...]], axis=0)), w_ref[:, lo:hi])
            rows = lax.broadcasted_iota(jnp.int32, cur.shape, 0)
            if seq_len >= tm:
                i = pl.program_id(0)
                tps = seq_len // tm
                starts = (rows == 0) & ((i % tps) == 0)
                ends = (rows == tm - 1) & ((i % tps) == tps - 1)
            else:
                starts = functools.reduce(jnp.logical_or, [rows == k * seq_len for k in range(tm // seq_len)])
                ends = functools.reduce(jnp.logical_or, [rows == (k + 1) * seq_len - 1 for k in range(tm // seq_len)])
            prev = jnp.where(rows == 0, edge[7:8, :], pltpu.roll(cur, 1, 0))
            nxt = jnp.where(rows == tm - 1, edge[8:9, :], pltpu.roll(cur, tm - 1, 0))
            prev = jnp.where(starts, 0.0, prev)
            nxt = jnp.where(ends, 0.0, nxt)
            y = prev * cw_ref[0:1, :] + cur * cw_ref[1:2, :] + nxt * cw_ref[2:3, :] + cb_ref[...]
            vals[name] = y * _sigmoid(y)
        o_ref[...] = vals[name].astype(o_ref.dtype)
    for o_ref, name in zip(outs[n_seg:], ("nk", "nv")):
        o_ref[...] = vals[name].reshape(o_ref.shape)


def _in_projection(x, norm_g, shift, scale, w_packed, conv_w, conv_b, seq_len, attn_dtype, kv_stack=None):
    n = x.shape[0]
    tm = 512
    shift, shift_spec = _mod_operand(shift, tm, seq_len)
    scale, scale_spec = _mod_operand(scale, tm, seq_len)
    out_shape = [jax.ShapeDtypeStruct((n, hi - lo), attn_dtype if name in ("nq", "nk", "nv") else F32)
                 for name, lo, hi in _IN_SEGS]
    out_specs = [pl.BlockSpec((tm, hi - lo), lambda i: (i, 0)) for _, lo, hi in _IN_SEGS]
    assert seq_len % tm == 0 or tm % seq_len == 0
    nb8 = n // 8
    in_specs = [pl.BlockSpec((tm, D), lambda i: (i, 0)),
                pl.BlockSpec((8, D), lambda i: (jnp.maximum(i * (tm // 8) - 1, 0), 0)),
                pl.BlockSpec((8, D), lambda i: (jnp.minimum((i + 1) * (tm // 8), nb8 - 1), 0)),
                pl.BlockSpec((1, D), lambda i: (0, 0)),
                shift_spec,
                scale_spec,
                pl.BlockSpec((D, W_PACKED), lambda i: (0, 0)),
                pl.BlockSpec((3, D_XBC), lambda i: (0, 0)),
                pl.BlockSpec((1, D_XBC), lambda i: (0, 0))]
    args = [x, x, x, norm_g.reshape(1, D), shift, scale, w_packed, conv_w, conv_b.reshape(1, D_XBC)]
    aliases = {}
    n_alias = 0
    if kv_stack is not None:
        layer, depth, k_prev, v_prev = kv_stack
        bpt = tm // seq_len
        stack_shape = jax.ShapeDtypeStruct((n // seq_len, depth, seq_len, D_NA), F32)
        for prev in (k_prev, v_prev):
            out_shape.append(stack_shape)
            out_specs.append(pl.BlockSpec((bpt, None, seq_len, D_NA), lambda i: (i, layer, 0, 0)))
            if prev is not None:
                aliases[len(args)] = len(out_shape) - 1
                in_specs.append(pl.BlockSpec(memory_space=pl.ANY))
                args.append(prev)
                n_alias += 1
    return pl.pallas_call(
        functools.partial(_inproj_kernel, n_alias=n_alias, n_seg=len(_IN_SEGS), seq_len=seq_len),
        grid=(n // tm,),
        in_specs=in_specs,
        out_specs=tuple(out_specs),
        out_shape=tuple(out_shape),
        input_output_aliases=aliases,
        compiler_params=_cparams(("arbitrary",)),
    )(*args)


def _tri_masks(ch):
    r = lax.broadcasted_iota(jnp.int32, (ch, ch), 0)
    c = lax.broadcasted_iota(jnp.int32, (ch, ch), 1)
    return r >= c, r <= c


def _mlstm_kernel(qkv_f, qkv_b, gt_f, gt_b, fb_ref, c0_ref, m0_ref,
                  hf_ref, hb_ref, cfin_ref, mfin_ref, c_scr, m_scr, it_scr, bt_scr, rc_scr, vt_scr,
                  kq_scr, in_scr, *, n_tiles, ch):
    j = pl.program_id(1)

    @pl.when(j == 0)
    def _():
        c_scr[...] = c0_ref[...]
        m_scr[...] = m0_ref[...]

    tt = qkv_f.shape[0]
    nch = tt // ch
    fb = fb_ref[...]
    low, upp = _tri_masks(ch)
    ones_row = (lax.broadcasted_iota(jnp.int32, (HD, ch), 0) == 0).astype(F32)
    pad_rows = jnp.zeros((ch - 2 * H_M, ch), F32)

    dc = [(d, c) for d in range(2) for c in range(nch)]
    i_rows, logf_rows, b_rows = {}, {}, {}
    for d, c in dc:
        z_t = ((gt_f, gt_b)[d][c * ch:(c + 1) * ch, :] + fb).T
        i_rows[d, c] = z_t[MI0:MI0 + 2 * H_M, :]
        zf = z_t[MF0:MF0 + 2 * H_M, :]
        logf_rows[d, c] = jnp.minimum(zf, 0.0) - jnp.log1p(jnp.exp(-jnp.abs(zf)))
    for d, c in dc:
        mask = (upp, low)[d]
        b_rows[d, c] = jnp.dot(logf_rows[d, c], mask.astype(F32), precision=HI, preferred_element_type=F32)
        it_scr[d, c] = i_rows[d, c]
        bt_scr[d, c] = b_rows[d, c]
    for d, c in dc:
        rc_scr[d, c] = jnp.concatenate([i_rows[d, c] - b_rows[d, c], pad_rows], axis=0).T
    for d, c in dc:
        for p in range(H_M // 2):
            vt_scr[d, c, p * 2 * HD:(p + 1) * 2 * HD, :] = (
                (qkv_f, qkv_b)[d][c * ch:(c + 1) * ch, 2 * D_M + p * 2 * HD:2 * D_M + (p + 1) * 2 * HD].T)

    def chunk(ci, carry):
        for d in range(2):
            qkv_ref = (qkv_f, qkv_b)[d]
            c0 = pl.multiple_of((ci if d == 0 else nch - 1 - ci) * ch, ch)
            for h in range(H_M):
                idx = d * H_M + h
                q = qkv_ref[pl.ds(c0, ch), h * HD:(h + 1) * HD].astype(BF16)
                k = (qkv_ref[pl.ds(c0, ch), D_M + h * HD:D_M + (h + 1) * HD] * (HD ** -0.5)).astype(BF16)
                kq_scr[idx] = _dot_nt(k, q)
                in_scr[idx] = _dot_nt(c_scr[idx].astype(BF16), q)
        for d in range(2):
            qkv_ref = (qkv_f, qkv_b)[d]
            out_ref = (hf_ref, hb_ref)[d]
            mask = (upp, low)[d]
            cidx = ci if d == 0 else nch - 1 - ci
            c0 = pl.multiple_of(cidx * ch, ch)
            i_t = it_scr[d, cidx]
            b_t = bt_scr[d, cidx]
            r_cols = rc_scr[d, cidx]
            edge = ch - 1 if d == 0 else 0
            h_t = []
            for h in range(H_M):
                idx = d * H_M + h
                vaug = jnp.concatenate([vt_scr[d, cidx, h * HD:(h + 1) * HD, :], ones_row], axis=0)
                k = (qkv_ref[pl.ds(c0, ch), D_M + h * HD:D_M + (h + 1) * HD] * (HD ** -0.5)).astype(BF16)
                b_row = b_t[idx:idx + 1, :]
                i_row = i_t[idx:idx + 1, :]
                m_prev = m_scr[idx][0:1, 0:1]
                caug = c_scr[idx]
                logd = jnp.where(mask, r_cols[:, idx:idx + 1] + b_row, -jnp.inf)
                m_inter = b_row + m_prev
                m_t = jnp.maximum(jnp.max(logd, axis=0, keepdims=True), m_inter)
                dmat = jnp.exp(logd - m_t)
                w_inter = jnp.exp(m_inter - m_t)
                s = (kq_scr[idx] * dmat).astype(BF16)
                tot = _dot(vaug.astype(BF16), s) + w_inter * in_scr[idx]
                den = jnp.maximum(jnp.abs(tot[HD:HD + 1, :]), jnp.exp(-m_t))
                h_t.append(tot[0:HD, :] / den)
                if h % 2 == 1:
                    pair = jnp.concatenate(h_t[-2:], axis=0).T
                    out_ref[pl.ds(c0, ch), (h - 1) * HD:(h + 1) * HD] = pair
                b_l = b_row[:, edge:edge + 1]
                lw = b_l - b_row + i_row
                m_new = jnp.maximum(b_l + m_prev, jnp.max(lw, axis=1, keepdims=True))
                wk = jnp.exp(lw - m_new)
                decay = jnp.exp(b_l + m_prev - m_new)
                c_scr[idx] = decay * caug + _dot((vaug * wk).astype(BF16), k)
                m_scr[idx] = jnp.broadcast_to(m_new, (8, 128))
        return carry

    lax.fori_loop(0, nch, chunk, 0)

    @pl.when(j == n_tiles - 1)
    def _():
        cfin_ref[...] = c_scr[...]
        mfin_ref[...] = m_scr[...]


def _mlstm_scan(qkv, gate, f_bias, caug0, m0, batch, seq_len, layer=0, depth=1, prev=None):
    n = qkv.shape[0]
    tt = min(seq_len, 512)
    nt = seq_len // tt
    ch = 128
    fb = jnp.zeros((1, GATE_W), F32).at[0, MF0:MF0 + 2 * H_M].set(f_bias.reshape(-1))
    fwd = lambda b, j: (b * nt + j, 0)
    bwd = lambda b, j: (b * nt + nt - 1 - j, 0)
    st4 = lambda b, j: (b, 0, 0, 0)
    st5 = lambda b, j: (b, layer, 0, 0, 0)
    n_in = 7
    prev = tuple(prev or ())
    kern = functools.partial(_mlstm_kernel, n_tiles=nt, ch=ch)
    return pl.pallas_call(
        lambda *refs: kern(*refs[:n_in], *refs[n_in + len(prev):]),
        grid=(batch, nt),
        in_specs=[pl.BlockSpec((tt, 3 * D_M), fwd), pl.BlockSpec((tt, 3 * D_M), bwd),
                  pl.BlockSpec((tt, GATE_W), fwd), pl.BlockSpec((tt, GATE_W), bwd),
                  pl.BlockSpec((1, GATE_W), lambda b, j: (0, 0)),
                  pl.BlockSpec((None, 2 * H_M, 128, HD), st4),
                  pl.BlockSpec((None, 2 * H_M, 8, 128), st4)] + [pl.BlockSpec(memory_space=pl.ANY)] * len(prev),
        out_specs=(pl.BlockSpec((tt, D_M), fwd), pl.BlockSpec((tt, D_M), bwd),
                   pl.BlockSpec((None, None, 2 * H_M, 128, HD), st5),
                   pl.BlockSpec((None, None, 2 * H_M, 8, 128), st5)),
        out_shape=(jax.ShapeDtypeStruct((n, D_M), F32), jax.ShapeDtypeStruct((n, D_M), F32),
                   jax.ShapeDtypeStruct((batch, depth, 2 * H_M, 128, HD), F32),
                   jax.ShapeDtypeStruct((batch, depth, 2 * H_M, 8, 128), F32)),
        input_output_aliases={n_in + i: 2 + i for i in range(len(prev))},
        scratch_shapes=[pltpu.VMEM((2 * H_M, 128, HD), F32), pltpu.VMEM((2 * H_M, 8, 128), F32),
                        pltpu.VMEM((2, tt // ch, 2 * H_M, ch), F32), pltpu.VMEM((2, tt // ch, 2 * H_M, ch), F32),
                        pltpu.VMEM((2, tt // ch, ch, 128), F32), pltpu.VMEM((2, tt // ch, D_M, ch), F32),
                        pltpu.VMEM((2 * H_M, ch, ch), F32), pltpu.VMEM((2 * H_M, 128, ch), F32)],
        compiler_params=_cparams(("arbitrary", "arbitrary")),
    )(qkv, qkv, gate, gate, fb, caug0, m0, *prev)


def _ssd_kernel(xbc_f, xbc_b, gt_f, gt_b, dtb_ref, alog_ref, h0_ref,
                yf_ref, yb_ref, hfin_ref, h_scr, dtt_scr, at_scr, ac_scr, xt_scr, cb_scr, in_scr, *, n_tiles, ch):
    j = pl.program_id(1)

    @pl.when(j == 0)
    def _():
        h_scr[...] = h0_ref[...]

    tt = xbc_f.shape[0]
    nch = tt // ch
    dtb = dtb_ref[...]
    a_neg = -jnp.exp(alog_ref[...])
    low, upp = _tri_masks(ch)
    nrow = 16
    pad_rows = jnp.zeros((ch - nrow, ch), F32)

    dc = [(d, c) for d in range(2) for c in range(nch)]
    da_rows, a_rows = {}, {}
    for d, c in dc:
        dt = _softplus((gt_f, gt_b)[d][c * ch:(c + 1) * ch, :] + dtb)
        dtt_scr[d, c] = dt.T[DT0:DT0 + nrow, :]
        da_rows[d, c] = (dt * a_neg).T[DT0:DT0 + nrow, :]
    for d, c in dc:
        mask = (upp, low)[d]
        a_rows[d, c] = jnp.dot(da_rows[d, c], mask.astype(F32), precision=HI, preferred_element_type=F32)
        at_scr[d, c] = a_rows[d, c]
    for d, c in dc:
        ac_scr[d, c] = jnp.concatenate([a_rows[d, c], pad_rows], axis=0).T
    for d, c in dc:
        for p in range(H_S // 2):
            xt_scr[d, c, p * 2 * HD:(p + 1) * 2 * HD, :] = (
                (xbc_f, xbc_b)[d][c * ch:(c + 1) * ch, p * 2 * HD:(p + 1) * 2 * HD].T)

    def chunk(ci, carry):
        for d in range(2):
            x_ref = (xbc_f, xbc_b)[d]
            c0 = pl.multiple_of((ci if d == 0 else nch - 1 - ci) * ch, ch)
            for grp in range(G_S):
                bm = x_ref[pl.ds(c0, ch), D_S + grp * N_S:D_S + (grp + 1) * N_S].astype(BF16)
                cm = x_ref[pl.ds(c0, ch), D_S + G_S * N_S + grp * N_S:D_S + G_S * N_S + (grp + 1) * N_S].astype(BF16)
                cb_scr[d * G_S + grp] = _dot_nt(bm, cm)
                for r in range(R_S):
                    idx = d * H_S + grp * R_S + r
                    in_scr[idx] = _dot_nt(h_scr[idx].astype(BF16), cm)
        for d in range(2):
            x_ref = (xbc_f, xbc_b)[d]
            out_ref = (yf_ref, yb_ref)[d]
            mask = (upp, low)[d]
            cidx = ci if d == 0 else nch - 1 - ci
            c0 = pl.multiple_of(cidx * ch, ch)
            dt_t = dtt_scr[d, cidx]
            a_t = at_scr[d, cidx]
            a_cols = ac_scr[d, cidx]
            edge = ch - 1 if d == 0 else 0
            y_t = []
            for grp in range(G_S):
                bm = x_ref[pl.ds(c0, ch), D_S + grp * N_S:D_S + (grp + 1) * N_S].astype(BF16)
                for r in range(R_S):
                    hh = grp * R_S + r
                    idx = d * H_S + hh
                    a_row = a_t[idx:idx + 1, :]
                    dt_row = dt_t[idx:idx + 1, :]
                    seg = jnp.exp(jnp.where(mask, a_row - a_cols[:, idx:idx + 1], -jnp.inf))
                    mm = (cb_scr[d * G_S + grp] * seg).astype(BF16)
                    xh = xt_scr[d, cidx, hh * HD:(hh + 1) * HD, :]
                    h_st = h_scr[idx]
                    y_t.append(_dot((xh * dt_row).astype(BF16), mm) + in_scr[idx] * jnp.exp(a_row))
                    if hh % 2 == 1:
                        pair = jnp.concatenate(y_t[-2:], axis=0).T
                        out_ref[pl.ds(c0, ch), (hh - 1) * HD:(hh + 1) * HD] = pair
                    a_l = a_row[:, edge:edge + 1]
                    wk = jnp.exp(a_l - a_row) * dt_row
                    h_scr[idx] = jnp.exp(a_l) * h_st + _dot((xh * wk).astype(BF16), bm)
        return carry

    lax.fori_loop(0, nch, chunk, 0)

    @pl.when(j == n_tiles - 1)
    def _():
        hfin_ref[...] = h_scr[...]


def _ssd_scan(xbc, gate, dt_bias, a_log, h0_t, batch, seq_len, layer=0, depth=1, prev=None):
    n = xbc.shape[0]
    tt = min(seq_len, 512)
    nt = seq_len // tt
    ch = 128
    dtb = jnp.zeros((1, GATE_W), F32).at[0, DT0:DT0 + 2 * H_S].set(dt_bias.reshape(-1))
    alog = jnp.zeros((1, GATE_W), F32).at[0, DT0:DT0 + 2 * H_S].set(a_log.reshape(-1))
    fwd = lambda b, j: (b * nt + j, 0)
    bwd = lambda b, j: (b * nt + nt - 1 - j, 0)
    st4 = lambda b, j: (b, 0, 0, 0)
    n_in = 7
    prev = tuple(prev or ())
    kern = functools.partial(_ssd_kernel, n_tiles=nt, ch=ch)
    return pl.pallas_call(
        lambda *refs: kern(*refs[:n_in], *refs[n_in + len(prev):]),
        grid=(batch, nt),
        in_specs=[pl.BlockSpec((tt, D_XBC), fwd), pl.BlockSpec((tt, D_XBC), bwd),
                  pl.BlockSpec((tt, GATE_W), fwd), pl.BlockSpec((tt, GATE_W), bwd),
                  pl.BlockSpec((1, GATE_W), lambda b, j: (0, 0)),
                  pl.BlockSpec((1, GATE_W), lambda b, j: (0, 0)),
                  pl.BlockSpec((None, 2 * H_S, N_S, HD), st4)] + [pl.BlockSpec(memory_space=pl.ANY)] * len(prev),
        out_specs=(pl.BlockSpec((tt, D_S), fwd), pl.BlockSpec((tt, D_S), bwd),
                   pl.BlockSpec((None, None, 2 * H_S, N_S, HD), lambda b, j: (b, layer, 0, 0, 0))),
        out_shape=(jax.ShapeDtypeStruct((n, D_S), F32), jax.ShapeDtypeStruct((n, D_S), F32),
                   jax.ShapeDtypeStruct((batch, depth, 2 * H_S, N_S, HD), F32)),
        input_output_aliases={n_in + i: 2 + i for i in range(len(prev))},
        scratch_shapes=[pltpu.VMEM((2 * H_S, N_S, HD), F32),
                        pltpu.VMEM((2, tt // ch, 16, ch), F32), pltpu.VMEM((2, tt // ch, 16, ch), F32),
                        pltpu.VMEM((2, tt // ch, ch, 128), F32), pltpu.VMEM((2, tt // ch, D_S, ch), F32),
                        pltpu.VMEM((2 * G_S, ch, ch), F32), pltpu.VMEM((2 * H_S, HD, ch), F32)],
        compiler_params=_cparams(("arbitrary", "arbitrary")),
    )(xbc, xbc, gate, gate, dtb, alog, h0_t, *prev)


def _ctx_attn_kernel(q_ref, k_ref, v_ref, o_ref, s_scr):
    n = q_ref.shape[0]
    first = lax.broadcasted_iota(jnp.int32, (n, 2 * HD), 1) < HD
    ones = jnp.ones((n, 2 * HD), BF16)
    for p in range(H_NA // 2):
        sl = slice(p * 2 * HD, (p + 1) * 2 * HD)
        qp = q_ref[:, sl]
        kp = k_ref[:, sl].astype(BF16)
        for half in range(2):
            mine = first if half == 0 else jnp.logical_not(first)
            q = (jnp.where(mine, qp, jnp.zeros_like(qp)) * (HD ** -0.5)).astype(BF16)
            s_scr[2 * p + half] = _dot_nt(q, kp)
    for p in range(H_NA // 2):
        sl = slice(p * 2 * HD, (p + 1) * 2 * HD)
        vp = v_ref[:, sl].astype(BF16)
        outs = []
        for half in range(2):
            s = s_scr[2 * p + half]
            pr = jnp.exp(s - jnp.max(s, axis=-1, keepdims=True)).astype(BF16)
            outs.append(_dot(pr, vp) / _dot(pr, ones))
        o_ref[:, sl] = jnp.where(first, outs[0], outs[1])


def _ctx_attention(q, k, v, batch, seq_len):
    spec = pl.BlockSpec((seq_len, D_NA), lambda b: (b, 0))
    return pl.pallas_call(
        _ctx_attn_kernel,
        grid=(batch,),
        in_specs=[spec, spec, spec],
        out_specs=spec,
        out_shape=jax.ShapeDtypeStruct(q.shape, F32),
        scratch_shapes=[pltpu.VMEM((H_NA, seq_len, seq_len), F32)],
        compiler_params=_cparams(("arbitrary",)),
    )(q, k, v)


def _row_start(r, rows):
    return jnp.clip(r - WIN_H // 2, 0, rows - WIN_H)


def _natten_kernel(q_ref, k_ref, v_ref, ck_ref, cv_ref, bias_ref, o_ref, sl_scr, sc_scr, *, rows):
    for sub in range(NATTEN_ROWS):
        _natten_row(pl.program_id(1) * NATTEN_ROWS + sub, sub, q_ref, k_ref, v_ref, ck_ref, cv_ref, bias_ref, o_ref,
                    sl_scr, sc_scr, rows)


def _natten_row(r, sub, q_ref, k_ref, v_ref, ck_ref, cv_ref, bias_ref, o_ref, sl_scr, sc_scr, rows):
    qrows = slice(sub * GRID_W, (sub + 1) * GRID_W)
    hbase = sub * H_NA
    k0 = pl.multiple_of(_row_start(r, rows) * GRID_W, GRID_W)
    off0 = _row_start(r, rows) - r + WIN_H - 1
    nloc = WIN_H * GRID_W
    npast = ck_ref.shape[0]
    first = lax.broadcasted_iota(jnp.int32, (GRID_W, 2 * HD), 1) < HD
    ones_loc = jnp.ones((nloc, 2 * HD), BF16)
    ones_ctx = jnp.ones((npast, 2 * HD), BF16)
    for p in range(H_NA // 2):
        sl = slice(p * 2 * HD, (p + 1) * 2 * HD)
        qp = q_ref[qrows, sl]
        kw = k_ref[pl.ds(k0, nloc), sl].astype(BF16)
        ckp = ck_ref[:, sl].astype(BF16)
        for half in range(2):
            mine = first if half == 0 else jnp.logical_not(first)
            q = (jnp.where(mine, qp, jnp.zeros_like(qp)) * (HD ** -0.5)).astype(BF16)
            bias = jnp.concatenate([bias_ref[2 * p + half, off0 + 2 * i] for i in range(WIN_H // 2)], axis=1)
            sl_scr[hbase + 2 * p + half] = _dot_nt(q, kw) + bias
            sc_scr[hbase + 2 * p + half] = _dot_nt(q, ckp)
    for p in range(H_NA // 2):
        sl = slice(p * 2 * HD, (p + 1) * 2 * HD)
        vw = v_ref[pl.ds(k0, nloc), sl].astype(BF16)
        cvp = cv_ref[:, sl].astype(BF16)
        outs = []
        for half in range(2):
            s_loc = sl_scr[hbase + 2 * p + half]
            s_ctx = sc_scr[hbase + 2 * p + half]
            m = jnp.maximum(jnp.max(s_loc, axis=-1, keepdims=True), jnp.max(s_ctx, axis=-1, keepdims=True))
            p_loc = jnp.exp(s_loc - m).astype(BF16)
            p_ctx = jnp.exp(s_ctx - m).astype(BF16)
            l = _dot(p_loc, ones_loc) + _dot(p_ctx, ones_ctx)
            outs.append((_dot(p_loc, vw) + _dot(p_ctx, cvp)) / l)
        o_ref[qrows, sl] = jnp.where(first, outs[0], outs[1])


def _natten_bias(rpb, rows):
    qc = np.arange(GRID_W)[:, None]
    kc = np.arange(GRID_W)[None, :]
    cstart = np.clip(qc - WIN_W // 2, 0, GRID_W - WIN_W)
    ok = (kc >= cstart) & (kc < cstart + WIN_W)
    col_off = np.clip(kc - qc + WIN_W - 1, 0, 2 * WIN_W - 2)
    pick = jnp.asarray((col_off[None] == np.arange(2 * WIN_W - 1)[:, None, None]).astype(np.float32))
    toep = jnp.einsum("hoj,jqk->hoqk", rpb.astype(F32), pick, precision=HI)
    t = jnp.where(jnp.asarray(ok)[None, None], toep, -jnp.inf)
    return jnp.concatenate([t[:, :-1], t[:, 1:]], axis=-1)


def _neighbourhood_attention(q, k, v, ck, cv, bias, batch, seq_len):
    rows = seq_len // GRID_W
    past = ck.shape[0] // batch
    steps = rows // NATTEN_ROWS
    qblk = NATTEN_ROWS * GRID_W
    return pl.pallas_call(
        functools.partial(_natten_kernel, rows=rows),
        grid=(batch, steps),
        in_specs=[pl.BlockSpec((qblk, D_NA), lambda b, r: (b * steps + r, 0)),
                  pl.BlockSpec((seq_len, D_NA), lambda b, r: (b, 0)),
                  pl.BlockSpec((seq_len, D_NA), lambda b, r: (b, 0)),
                  pl.BlockSpec((past, D_NA), lambda b, r: (b, 0)),
                  pl.BlockSpec((past, D_NA), lambda b, r: (b, 0)),
                  pl.BlockSpec((H_NA, 2 * WIN_H - 2, GRID_W, 2 * GRID_W), lambda b, r: (0, 0, 0, 0))],
        out_specs=pl.BlockSpec((qblk, D_NA), lambda b, r: (b * steps + r, 0)),
        out_shape=jax.ShapeDtypeStruct(q.shape, F32),
        scratch_shapes=[pltpu.VMEM((NATTEN_ROWS * H_NA, GRID_W, WIN_H * GRID_W), F32),
                        pltpu.VMEM((NATTEN_ROWS * H_NA, GRID_W, past), F32)],
        compiler_params=_cparams(("arbitrary", "arbitrary")),
    )(q, k, v, ck, cv, bias)


def _outproj_kernel(x_ref, hf_ref, hb_ref, mo_ref, yf_ref, yb_ref, xbc_ref, sz_ref, a_ref,
                    mg_ref, dsk_ref, sg_ref, bd_ref, w_ref, g1_ref, n2_ref, sh2_ref, sc2_ref, rw_ref,
                    xo_ref, h2_ref, aff_ref):
    hm = hf_ref[...] + hb_ref[...]
    ssq = jnp.dot(hm * hm, bd_ref[...], precision=HI, preferred_element_type=F32)
    m_out = hm * lax.rsqrt(ssq * (1.0 / HD) + EPS) * mg_ref[...] * _sigmoid(mo_ref[...])
    sz = sz_ref[...]
    ys = (yf_ref[...] + yb_ref[...] + dsk_ref[...] * xbc_ref[:, 0:D_S]) * (sz * _sigmoid(sz))
    s_out = ys * lax.rsqrt(jnp.mean(ys * ys, axis=-1, keepdims=True) + EPS) * sg_ref[...]
    y = (_dot(m_out.astype(BF16), w_ref[0:D_M, :])
         + _dot(s_out.astype(BF16), w_ref[D_M:D_M + D_S, :])
         + _dot(a_ref[...].astype(BF16), w_ref[D_M + D_S:D, :]))
    xn = x_ref[...] + g1_ref[...] * y
    xo_ref[...] = xn
    h2 = xn * lax.rsqrt(jnp.mean(xn * xn, axis=-1, keepdims=True) + EPS) * n2_ref[...]
    h2 = h2 * (1.0 + sc2_ref[...]) + sh2_ref[...]
    for c in range(h2_ref.shape[0]):
        h2_ref[c] = h2[c * GATHER_TOK:(c + 1) * GATHER_TOK, :].T.astype(BF16)
    logits = lax.dot_general(rw_ref[...], h2, (((1,), (1,)), ((), ())), precision=HI,
                             preferred_element_type=F32)
    mx = jnp.max(logits, axis=0, keepdims=True)
    ex = jnp.exp(logits - mx)
    aff_ref[...] = ex / jnp.sum(ex, axis=0, keepdims=True)


def _out_projection(x, hf, hb, mo, yf, yb, xbc, sz, a_out, mnorm_g, d_skip, snorm_g, w_out_bf,
                    g1, norm2_g, sh2, sc2, router_wt, seq_len):
    n = x.shape[0]
    tm = 512
    cpt = tm // GATHER_TOK
    g1, g1_spec = _mod_operand(g1, tm, seq_len)
    sh2, sh2_spec = _mod_operand(sh2, tm, seq_len)
    sc2, sc2_spec = _mod_operand(sc2, tm, seq_len)
    row = lambda w: pl.BlockSpec((tm, w), lambda i: (i, 0))
    const = lambda s: pl.BlockSpec(s, lambda i: (0,) * len(s))
    hid = np.arange(D_M) // HD
    blockdiag = jnp.asarray((hid[:, None] == hid[None, :]).astype(np.float32))
    return pl.pallas_call(
        _outproj_kernel,
        grid=(n // tm,),
        in_specs=[row(D), row(D_M), row(D_M), row(D_M), row(D_S), row(D_S), row(D_XBC), row(D_S), row(D_NA),
                  const((1, D_M)), const((1, D_S)), const((1, D_S)), const((D_M, D_M)), const((D, D)),
                  g1_spec, const((1, D)), sh2_spec, sc2_spec,
                  const((N_EXPERTS, D))],
        out_specs=(row(D), pl.BlockSpec((cpt, D, GATHER_TOK), lambda i: (i, 0, 0)),
                   pl.BlockSpec((N_EXPERTS, tm), lambda i: (0, i))),
        out_shape=(jax.ShapeDtypeStruct((n, D), F32), jax.ShapeDtypeStruct((n // GATHER_TOK, D, GATHER_TOK), BF16),
                   jax.ShapeDtypeStruct((N_EXPERTS, n), F32)),
        compiler_params=_cparams(("arbitrary",)),
    )(x, hf, hb, mo, yf, yb, xbc, sz, a_out,
      mnorm_g.reshape(1, D_M), jnp.repeat(d_skip, HD).reshape(1, D_S), snorm_g.reshape(1, D_S), blockdiag,
      w_out_bf, g1, norm2_g.reshape(1, D), sh2, sc2, router_wt)


def _select_kernel(aff_ref, gate_ref, slot_ref, start_ref, *, cap):
    aff = aff_ref[...]
    bits = pltpu.bitcast(aff, jnp.int32)
    n_tok = aff.shape[1]

    def step(i, prefix):
        cand = prefix | (jnp.int32(1) << (30 - i))
        cnt = jnp.sum((bits >= cand).astype(F32), axis=1, keepdims=True)
        return jnp.where(cnt >= cap, cand, prefix)

    thr = lax.fori_loop(0, 31, step, jnp.zeros((aff.shape[0], 1), jnp.int32))
    gt = bits > thr
    eq = bits == thr
    need = cap - jnp.sum(gt.astype(F32), axis=1, keepdims=True)
    r_i = lax.broadcasted_iota(jnp.int32, (128, 128), 0)
    c_i = lax.broadcasted_iota(jnp.int32, (128, 128), 1)
    strict = (r_i < c_i).astype(BF16)
    run_eq = jnp.zeros((aff.shape[0], 1), F32)
    run_sel = jnp.zeros((aff.shape[0], 1), F32)
    lane = lax.broadcasted_iota(jnp.int32, (aff.shape[0], 128), 1)
    starts = jnp.zeros((aff.shape[0], 128), F32)
    for blk in range(n_tok // TOK_BLK):
        sl = slice(blk * TOK_BLK, (blk + 1) * TOK_BLK)
        e = eq[:, sl]
        rank = _dot(e.astype(BF16), strict) + run_eq
        keep = gt[:, sl] | (e & (rank < need))
        kf = jnp.where(keep, 1.0, 0.0)
        slot = _dot(kf.astype(BF16), strict) + run_sel
        gate_ref[:, sl] = jnp.where(keep, aff[:, sl], 0.0)
        slot_ref[:, sl] = jnp.where(keep, slot, -1.0).astype(jnp.int32)
        starts = jnp.where(lane == blk, run_sel, starts)
        run_eq = run_eq + jnp.sum(e.astype(F32), axis=1, keepdims=True)
        run_sel = run_sel + jnp.sum(kf, axis=1, keepdims=True)
    start_ref[...] = starts.astype(jnp.int32)


def _select(aff_t):
    n = aff_t.shape[1]
    cap = EC_FACTOR * n // N_EXPERTS
    assert n // TOK_BLK <= 128
    return pl.pallas_call(
        functools.partial(_select_kernel, cap=float(cap)),
        out_shape=(jax.ShapeDtypeStruct(aff_t.shape, F32), jax.ShapeDtypeStruct(aff_t.shape, jnp.int32),
                   jax.ShapeDtypeStruct((N_EXPERTS, 128), jnp.int32)),
        compiler_params=pltpu.CompilerParams(vmem_limit_bytes=VMEM_LIMIT),
    )(aff_t)


def _gather_rows(e, lo_ref, hi_ref, slot_ref, h2t_ref, xe_scr, acc_scr, col_scr, cap):
    nsb = cap // GATHER_SLOT
    nchunk = h2t_ref.shape[0]
    fill = jnp.zeros((TOK_BLK - 8, GATHER_TOK), F32)
    for c in range(nchunk):
        row = jnp.broadcast_to(slot_ref[:, c * GATHER_TOK:(c + 1) * GATHER_TOK].astype(F32), (8, GATHER_TOK))
        col_scr[c] = jnp.concatenate([row, fill], axis=0).T
    lane = lax.broadcasted_iota(jnp.int32, (GATHER_TOK, GATHER_SLOT), 1).astype(F32)
    win = min(GATHER_WIN, nchunk)

    def onehot_t(c, want):
        return jnp.where(col_scr[c][:, 0:1] == want, 1.0, 0.0).astype(BF16)

    for sb in range(nsb):
        want = lane + float(sb * GATHER_SLOT)
        first = jnp.minimum(lo_ref[e * nsb + sb], nchunk - win)
        acc = _dot(h2t_ref[first], onehot_t(first, want))
        for i in range(1, win):
            acc = acc + _dot(h2t_ref[first + i], onehot_t(first + i, want))
        acc_scr[...] = acc

        def body(c, carry):
            acc_scr[...] += _dot(h2t_ref[c], onehot_t(c, want))
            return carry

        lax.fori_loop(first + win, hi_ref[e * nsb + sb] + 1, body, 0)
        xe_scr[sb * GATHER_SLOT:(sb + 1) * GATHER_SLOT, :] = acc_scr[...].T.astype(BF16)


def _expert_kernel(lo_ref, hi_ref, slot_ref, h2t_ref, w1_ref, w3_ref, w2_ref, ye_ref,
                   xe_scr, acc_scr, col_scr, y_scr, *, cap):
    e = pl.program_id(0)
    f = pl.program_id(1)

    @pl.when(f == 0)
    def _():
        _gather_rows(e, lo_ref, hi_ref, slot_ref, h2t_ref, xe_scr, acc_scr, col_scr, cap)

    x = xe_scr[...]
    a = _dot(x, w1_ref[...].astype(BF16))
    b = _dot(x, w3_ref[...].astype(BF16))
    hid = (a * _sigmoid(a) * b).astype(BF16)
    part = _dot(hid, w2_ref[...].astype(BF16))

    @pl.when(f == 0)
    def _():
        y_scr[...] = part

    @pl.when(f == 1)
    def _():
        ye_ref[...] = (y_scr[...] + part).astype(BF16)


def _combine_kernel(win_ref, slot_ref, gate_ref, x_ref, g2_ref, fg_ref, ye_ref, o_ref, acc_scr, *, final):
    lane = lax.broadcasted_iota(jnp.int32, (TOK_BLK, 2 * SLOT_BLK), 1)
    fill = jnp.zeros((TOK_BLK - N_EXPERTS, TOK_BLK), F32)
    for s in range(COMBINE_BLKS):
        tb = pl.program_id(0) * COMBINE_BLKS + s
        rows = slice(s * TOK_BLK, (s + 1) * TOK_BLK)
        slot_t = jnp.concatenate([slot_ref[:, rows].astype(F32), fill], axis=0).T
        gate_t = jnp.concatenate([gate_ref[:, rows], fill], axis=0).T
        acc_scr[...] = jnp.zeros_like(acc_scr)
        for e in range(N_EXPERTS):
            s0 = pl.multiple_of(win_ref[tb * N_EXPERTS + e] * SLOT_BLK, SLOT_BLK)
            want = (lane + s0).astype(F32)
            onehot = jnp.where(slot_t[:, e:e + 1] == want, 1.0, 0.0).astype(BF16)
            acc_scr[...] += _dot(onehot, ye_ref[e, pl.ds(s0, 2 * SLOT_BLK), :]) * gate_t[:, e:e + 1]
        y = x_ref[rows, :] + g2_ref[...] * acc_scr[...]
        if final:
            y = y * lax.rsqrt(jnp.mean(y * y, axis=-1, keepdims=True) + EPS) * fg_ref[...]
        o_ref[rows, :] = y


def _moe(h2, aff_t, x, g2, w1, w3, w2, layer, seq_len, final_g=None):
    n = x.shape[0]
    cap = EC_FACTOR * n // N_EXPERTS
    nsb = cap // SLOT_BLK
    ntb = n // TOK_BLK
    gate, slot, starts = _select(aff_t)

    cs = starts[:, :ntb]
    win = jnp.minimum(cs // SLOT_BLK, nsb - 2).T.astype(jnp.int32)
    gs = cs[:, ::GATHER_TOK // TOK_BLK]
    gend = jnp.concatenate([gs[:, 1:], jnp.full((N_EXPERTS, 1), cap, jnp.int32)], axis=1)
    edges = jnp.arange(cap // GATHER_SLOT, dtype=jnp.int32) * GATHER_SLOT
    lo = jnp.sum(gend[:, None, :] <= edges[None, :, None], axis=-1).astype(jnp.int32)
    hi = jnp.sum(gs[:, None, :] < (edges + GATHER_SLOT)[None, :, None], axis=-1).astype(jnp.int32) - 1

    d_ff = w1.shape[-1]
    up_spec = pl.BlockSpec((None, None, D, d_ff // 2), lambda e, f, *_: (layer, e, 0, f))
    down_spec = pl.BlockSpec((None, None, d_ff // 2, D), lambda e, f, *_: (layer, e, f, 0))
    ye = pl.pallas_call(
        functools.partial(_expert_kernel, cap=cap),
        grid_spec=pltpu.PrefetchScalarGridSpec(
            num_scalar_prefetch=2,
            grid=(N_EXPERTS, 2),
            in_specs=[pl.BlockSpec((None, 1, n), lambda e, f, *_: (e, 0, 0)),
                      pl.BlockSpec((n // GATHER_TOK, D, GATHER_TOK), lambda e, f, *_: (0, 0, 0),
                                   pipeline_mode=pl.Buffered(1)),
                      up_spec, up_spec, down_spec],
            out_specs=pl.BlockSpec((None, cap, D), lambda e, f, *_: (e, 0, 0)),
            scratch_shapes=[pltpu.VMEM((cap, D), BF16), pltpu.VMEM((D, GATHER_SLOT), F32),
                            pltpu.VMEM((n // GATHER_TOK, GATHER_TOK, TOK_BLK), F32),
                            pltpu.VMEM((cap, D), F32)]),
        out_shape=jax.ShapeDtypeStruct((N_EXPERTS, cap, D), BF16),
        compiler_params=_cparams(("arbitrary", "arbitrary")),
    )(lo.reshape(-1), hi.reshape(-1), slot.reshape(N_EXPERTS, 1, n), h2, w1, w3, w2)

    tc = COMBINE_BLKS * TOK_BLK
    g2, g2_spec = _mod_operand(g2, tc, seq_len)
    fg = jnp.ones((1, D), F32) if final_g is None else final_g.reshape(1, D)
    return pl.pallas_call(
        functools.partial(_combine_kernel, final=final_g is not None),
        grid_spec=pltpu.PrefetchScalarGridSpec(
            num_scalar_prefetch=1,
            grid=(n // tc,),
            in_specs=[pl.BlockSpec((N_EXPERTS, tc), lambda i, *_: (0, i)),
                      pl.BlockSpec((N_EXPERTS, tc), lambda i, *_: (0, i)),
                      pl.BlockSpec((tc, D), lambda i, *_: (i, 0)),
                      g2_spec,
                      pl.BlockSpec((1, D), lambda i, *_: (0, 0)),
                      pl.BlockSpec((N_EXPERTS, cap, D), lambda i, *_: (0, 0, 0), pipeline_mode=pl.Buffered(1))],
            out_specs=pl.BlockSpec((tc, D), lambda i, *_: (i, 0)),
            scratch_shapes=[pltpu.VMEM((TOK_BLK, D), F32)]),
        out_shape=jax.ShapeDtypeStruct((n, D), F32),
        compiler_params=_cparams(("arbitrary",)),
    )(win.reshape(-1), slot, gate, x, g2, fg, ye)


def _layer(x, mod, prm, batch, seq_len, caug0, m0, h0_t, ck=None, cv=None, bias=None, kv_stack=None,
           state_stack=None):
    st_layer, st_depth, st_cm, st_h = state_stack or (0, 1, None, None)
    sh1, sc1, g1, sh2, sc2, g2 = [(mod[0], prm["layer"], mod[1], col, mod[2]) for col in range(6)]
    attn_dtype = F32 if ck is None else BF16
    qkv, mo, sz, xbc, nq, nk, nv, gate, *kv_out = _in_projection(
        x, prm["norm1_g"], sh1, sc1, prm["w_in"], prm["conv_w"], prm["conv_b"], seq_len, attn_dtype, kv_stack)
    hf, hb, cfin, mfin = _mlstm_scan(qkv, gate, prm["f_bias"], caug0, m0, batch, seq_len, st_layer, st_depth, st_cm)
    yf, yb, hfin = _ssd_scan(xbc, gate, prm["dt_bias"], prm["a_log"], h0_t, batch, seq_len, st_layer, st_depth, st_h)
    if ck is None:
        a_out = _ctx_attention(nq, nk, nv, batch, seq_len)
    else:
        a_out = _neighbourhood_attention(nq, nk, nv, ck, cv, bias, batch, seq_len)
    xn, h2, aff_t = _out_projection(x, hf, hb, mo, yf, yb, xbc, sz, a_out, prm["mnorm_g"], prm["d_skip"],
                                    prm["snorm_g"], prm["w_out"], g1, prm["norm2_g"], sh2, sc2,
                                    prm["router_wt"], seq_len)
    xo = _moe(h2, aff_t, xn, g2, prm["w1"], prm["w3"], prm["w2"], prm["layer"], seq_len, prm["final_g"])
    return xo, kv_out, cfin, mfin, hfin


def _pack_w_in(w_in):
    o = np.cumsum((0, D_M, D_M, D_M, D_M, 2 * H_M, 2 * H_M, D_S, D_XBC, 2 * H_S, D_NA, D_NA, D_NA))
    pad = jnp.zeros(w_in.shape[:-1] + (GATE_W - 4 * H_M - 2 * H_S,), w_in.dtype)
    parts = [w_in[..., o[0]:o[4]], w_in[..., o[6]:o[8]], w_in[..., o[9]:o[12]],
             w_in[..., o[4]:o[6]], w_in[..., o[8]:o[9]], pad]
    return jnp.concatenate(parts, axis=-1).astype(BF16)


def kernel(x_prompt, x_sample, cache_na_k, cache_na_v, state_mlstm_c, state_mlstm_n, state_mlstm_m, state_ssm, c, c_ctx, ada_w, ada_b, norm1_g, norm2_g, w_in, mlstm_f_bias, mlstm_norm_g, conv_w, conv_b, ssm_dt_bias, ssm_a_log, ssm_d, ssm_norm_g, na_rpb, w_out, router_w, exp_w1, exp_w3, exp_w2, final_g):
    bp, sp, _ = x_prompt.shape
    bs, ss, _ = x_sample.shape
    past = cache_na_k.shape[2]

    cvec = jnp.zeros((8, D), F32).at[0].set(c_ctx).at[1:1 + bs].set(c)
    mod4 = _modulation(cvec, ada_w, ada_b).reshape(DEPTH, 8, 1, 6 * D)

    w_in_p = _pack_w_in(w_in)
    w_out_bf = w_out.astype(BF16)
    router_wt = jnp.swapaxes(router_w, 1, 2)

    def aug(cs, ns):
        pad = jnp.zeros(cs.shape[:-2] + (128 - HD - 1, HD), F32)
        out = jnp.concatenate([jnp.swapaxes(cs, -1, -2), ns[..., None, :], pad], axis=-2)
        return out.reshape(cs.shape[:2] + (2 * H_M, 128, HD))

    lat_caug_all = aug(state_mlstm_c, state_mlstm_n)

    def rep_m(ms):
        b = ms.shape[0]
        return jnp.broadcast_to(ms.reshape(b, 2 * H_M, 1, 1), (b, 2 * H_M, 8, 128)).astype(F32)

    ctx_caug0 = jnp.zeros((bp, 2 * H_M, 128, HD), F32)
    ctx_m0 = jnp.full((bp, 2 * H_M, 8, 128), NEG_INIT, F32)
    ctx_h0 = jnp.zeros((bp, 2 * H_S, N_S, HD), F32)

    xp = x_prompt.reshape(bp * sp, D)
    xs = x_sample.reshape(bs * ss, D)
    k_stack = jnp.zeros((bp, DEPTH, sp, D_NA), F32)
    v_stack = jnp.zeros((bp, DEPTH, sp, D_NA), F32)
    cm_stack = (jnp.zeros((bp, DEPTH, 2 * H_M, 128, HD), F32), jnp.zeros((bp, DEPTH, 2 * H_M, 8, 128), F32))
    h_stack = (jnp.zeros((bp, DEPTH, 2 * H_S, N_S, HD), F32),)
    for l in range(DEPTH):
        prm = dict(norm1_g=norm1_g[l], norm2_g=norm2_g[l], w_in=w_in_p[l], f_bias=mlstm_f_bias[l],
                   mnorm_g=mlstm_norm_g[l], conv_w=conv_w[l], conv_b=conv_b[l], dt_bias=ssm_dt_bias[l],
                   a_log=ssm_a_log[l], d_skip=ssm_d[l], snorm_g=ssm_norm_g[l], w_out=w_out_bf[l],
                   router_wt=router_wt[l], w1=exp_w1, w3=exp_w3, w2=exp_w2, layer=l,
                   final_g=final_g if l == DEPTH - 1 else None)
        xp, (k_stack, v_stack), c_all, m_all, h_all = _layer(
            xp, (mod4, 0, False), prm, bp, sp, ctx_caug0, ctx_m0, ctx_h0, kv_stack=(l, DEPTH, k_stack, v_stack),
            state_stack=(l, DEPTH, cm_stack, h_stack))
        cm_stack, h_stack = (c_all, m_all), (h_all,)

        lat_caug0 = lat_caug_all[:, l]
        lat_m0 = rep_m(state_mlstm_m[:, l])
        lat_h0 = state_ssm[:, l].reshape(bs, 2 * H_S, HD, N_S)
        ck = cache_na_k[:, l].reshape(bs * past, D_NA).astype(BF16)
        cv = cache_na_v[:, l].reshape(bs * past, D_NA).astype(BF16)
        bias = _natten_bias(na_rpb[l], ss // GRID_W)
        xs = _layer(xs, (mod4, 1, True), prm, bs, ss, lat_caug0, lat_m0, lat_h0, ck, cv, bias)[0]

    y_prompt = xp.reshape(bp, sp, D)
    y_sample = xs.reshape(bs, ss, D)
    new_c = jnp.swapaxes(c_all[..., 0:HD, :], -1, -2).reshape(bp, DEPTH, 2, H_M, HD, HD)
    new_n = c_all[..., HD, :].reshape(bp, DEPTH, 2, H_M, HD)
    new_m = m_all[..., 0, 0].reshape(bp, DEPTH, 2, H_M)
    return (y_prompt, y_sample, k_stack.reshape(bp, DEPTH, sp, H_NA, HD), v_stack.reshape(bp, DEPTH, sp, H_NA, HD),
            new_c, new_n, new_m, h_all.reshape(bp, DEPTH, 2, H_S, HD, N_S))
```
